```python
import math
import jax, jax.numpy as jnp
from jax import lax
import numpy as np

D_MODEL = 1024
BATCH = 16
SEQ = 2048
DEPTH = 2

D_CONV = 512
CONV_WIDTH = 31
N_HEADS = 8
D_HEAD_OUT = 64
D_ATTN = N_HEADS * D_HEAD_OUT
D_MIX = D_CONV + D_ATTN
Q_LORA = 256
KV_LORA = 128
N_IDX_HEADS = 8
D_IDX = 64
TOPK_MAX = 256
N_BUCKETS = 32
MAX_DISTANCE = 128
Q_BLOCK = 128
EPS = 1e-6
SPLIT_SIZES = (D_CONV, D_CONV, D_CONV, Q_LORA, KV_LORA, D_IDX, N_IDX_HEADS, D_ATTN)
D_IN_PROJ = sum(SPLIT_SIZES)

kernel_name = "hymba_conformer_dsa_hybrid"


def rmsnorm(x, g):
    xf = x.astype(jnp.float32)
    y = xf * lax.rsqrt(jnp.mean(xf * xf, axis=-1, keepdims=True) + EPS)
    return (y * g.astype(jnp.float32)).astype(x.dtype)


def layernorm(x, g, b):
    xf = x.astype(jnp.float32)
    mu = jnp.mean(xf, axis=-1, keepdims=True)
    var = jnp.mean(jnp.square(xf - mu), axis=-1, keepdims=True)
    y = (xf - mu) * lax.rsqrt(var + EPS)
    return (y * g.astype(jnp.float32) + b.astype(jnp.float32)).astype(x.dtype)


def t5_bucket(n):
    max_exact = N_BUCKETS // 2
    n = jnp.maximum(n, 0)
    nf = jnp.maximum(n, 1).astype(jnp.float32)
    large = max_exact + (jnp.log(nf / max_exact) / math.log(MAX_DISTANCE / max_exact)
                         * (N_BUCKETS - max_exact)).astype(jnp.int32)
    large = jnp.minimum(large, N_BUCKETS - 1)
    return jnp.where(n < max_exact, n, large)


def conformer_conv(u_val, u_gate, conv_w, conv_b, ln_g, ln_b, w_pw2):
    a = u_val * jax.nn.sigmoid(u_gate)
    y = lax.conv_general_dilated(
        a, conv_w[:, None, :].astype(a.dtype), window_strides=(1,),
        padding=[(CONV_WIDTH - 1, 0)], dimension_numbers=("NWC", "WIO", "NWC"),
        feature_group_count=D_CONV) + conv_b
    y = jax.nn.silu(layernorm(y, ln_g, ln_b))
    return y @ w_pw2


def dsa_attention(c_q, c_kv, k_idx, w_idx, q_norm_g, w_uq, w_qidx, kv_norm_g, w_uv, rel_bias):
    B, L, _ = c_q.shape
    k_top = min(TOPK_MAX, L // 4)
    cq = rmsnorm(c_q, q_norm_g)
    q = (cq @ w_uq).reshape(B, L, N_HEADS, KV_LORA)
    q_idx = (cq @ w_qidx).reshape(B, L, N_IDX_HEADS, D_IDX)
    kv = rmsnorm(c_kv, kv_norm_g)
    w = w_idx.astype(jnp.float32) * (N_IDX_HEADS ** -0.5 * D_IDX ** -0.5)
    k_idx_f = k_idx.astype(jnp.float32)
    nb = L // Q_BLOCK
    key_pos = jnp.arange(L)

    def to_blocks(a):
        return a.reshape(B, nb, Q_BLOCK, *a.shape[2:]).swapaxes(0, 1)

    def block_fn(args):
        blk, q_b, qi_b, w_b = args
        t_pos = blk * Q_BLOCK + jnp.arange(Q_BLOCK)
        s_raw = jnp.einsum("bthd,bsd->bths", qi_b.astype(jnp.float32), k_idx_f)
        score = jnp.einsum("bth,bths->bts", w_b, jax.nn.relu(s_raw))
        causal = key_pos[None, :] <= t_pos[:, None]
        score = jnp.where(causal[None], score, -jnp.inf)
        _, sel = lax.top_k(score, k_top)
        kv_sel = jax.vmap(lambda kv_b, i_b: kv_b[i_b])(kv, sel)
        logits = jnp.einsum("bthc,btkc->bthk", q_b, kv_sel).astype(jnp.float32) * (KV_LORA ** -0.5)
        dist = t_pos[None, :, None] - sel
        bias = jnp.moveaxis(rel_bias[t5_bucket(dist)], -1, 2)
        logits = logits + bias.astype(jnp.float32)
        valid = (dist >= 0)[:, :, None, :]
        probs = jax.nn.softmax(jnp.where(valid, logits, -jnp.inf), axis=-1)
        return jnp.einsum("bthk,btkc->bthc", probs.astype(kv.dtype), kv_sel)

    o = lax.map(block_fn, (jnp.arange(nb), to_blocks(q), to_blocks(q_idx), to_blocks(w)))
    o = o.swapaxes(0, 1).reshape(B, L, N_HEADS, KV_LORA)
    o = jnp.einsum("bshc,hcd->bshd", o, w_uv)
    return o.reshape(B, L, D_ATTN)


def setup_inputs(seed: int = 0) -> dict:
    key = jax.random.key(seed)
    ks = jax.random.split(key, 24)
    f32 = jnp.float32
    nrm = lambda k, shape, s: jax.random.normal(k, shape, f32) * s
    return {
        "x": nrm(ks[0], (BATCH, SEQ, D_MODEL), 1.0),
        "c": nrm(ks[1], (BATCH, D_MODEL), 1.0),
        "w_ada": nrm(ks[2], (DEPTH, D_MODEL, 3 * D_MODEL), D_MODEL ** -0.5),
        "b_ada": nrm(ks[3], (DEPTH, 3 * D_MODEL), 0.02),
        "g_pre": 1.0 + nrm(ks[4], (DEPTH, D_MODEL), 0.05),
        "w_in": nrm(ks[5], (DEPTH, D_MODEL, D_IN_PROJ), D_MODEL ** -0.5),
        "conv_w": nrm(ks[6], (DEPTH, CONV_WIDTH, D_CONV), CONV_WIDTH ** -0.5),
        "conv_b": nrm(ks[7], (DEPTH, D_CONV), 0.02),
        "conv_ln_g": 1.0 + nrm(ks[8], (DEPTH, D_CONV), 0.05),
        "conv_ln_b": nrm(ks[9], (DEPTH, D_CONV), 0.02),
        "w_pw2": nrm(ks[10], (DEPTH, D_CONV, D_CONV), D_CONV ** -0.5),
        "q_norm_g": 1.0 + nrm(ks[11], (DEPTH, Q_LORA), 0.05),
        "w_uq": nrm(ks[12], (DEPTH, Q_LORA, N_HEADS * KV_LORA), Q_LORA ** -0.5),
        "w_qidx": nrm(ks[13], (DEPTH, Q_LORA, N_IDX_HEADS * D_IDX), Q_LORA ** -0.5),
        "kv_norm_g": 1.0 + nrm(ks[14], (DEPTH, KV_LORA), 0.05),
        "w_uv": nrm(ks[15], (DEPTH, N_HEADS, KV_LORA, D_HEAD_OUT), KV_LORA ** -0.5),
        "rel_bias": nrm(ks[16], (N_BUCKETS, N_HEADS), 0.5),
        "w_out": nrm(ks[17], (DEPTH, D_MIX, D_MODEL), D_MIX ** -0.5),
        "g_post": 1.0 + nrm(ks[18], (DEPTH, D_MODEL), 0.05),
    }


def reference(x, c, w_ada, b_ada, g_pre, w_in, conv_w, conv_b, conv_ln_g, conv_ln_b, w_pw2,
              q_norm_g, w_uq, w_qidx, kv_norm_g, w_uv, rel_bias, w_out, g_post):
    offsets = [int(v) for v in np.cumsum(SPLIT_SIZES)[:-1]]
    c_act = jax.nn.silu(c)
    for l in range(DEPTH):
        mod = c_act @ w_ada[l] + b_ada[l]
        shift, scale, gate = jnp.split(mod, 3, axis=-1)
        h = rmsnorm(x, g_pre[l]) * (1.0 + scale[:, None, :]) + shift[:, None, :]
        p = h @ w_in[l]
        u_val, u_gate, g_conv, c_q, c_kv, k_idx, w_idx, g_attn = jnp.split(p, offsets, axis=-1)
        y_conv = conformer_conv(u_val, u_gate, conv_w[l], conv_b[l], conv_ln_g[l], conv_ln_b[l],
                                w_pw2[l]) * jax.nn.silu(g_conv)
        y_attn = dsa_attention(c_q, c_kv, k_idx, w_idx, q_norm_g[l], w_uq[l], w_qidx[l],
                               kv_norm_g[l], w_uv[l], rel_bias) * jax.nn.silu(g_attn)
        y = jnp.concatenate([y_conv, y_attn], axis=-1) @ w_out[l]
        x = x + gate[:, None, :] * rmsnorm(y, g_post[l])
    return x
```

```python
import functools
import math

import jax
import jax.numpy as jnp
from jax import lax
from jax.experimental import pallas as pl
from jax.experimental.pallas import tpu as pltpu

F32 = jnp.float32
BF16 = jnp.bfloat16
I32 = jnp.int32

D_CONV = 512
CONV_WIDTH = 31
N_HEADS = 8
D_HEAD_OUT = 64
D_ATTN = N_HEADS * D_HEAD_OUT
Q_LORA = 256
KV_LORA = 128
N_IDX_HEADS = 8
D_IDX = 64
TOPK_MAX = 256
N_BUCKETS = 32
MAX_DISTANCE = 128
EPS = 1e-6

KW_PAD = 128 - D_IDX - N_IDX_HEADS
OFF_UVAL = 0
OFF_UGATE = OFF_UVAL + D_CONV
OFF_GCONV = OFF_UGATE + D_CONV
OFF_CQ = OFF_GCONV + D_CONV
OFF_CKV = OFF_CQ + Q_LORA
OFF_KW = OFF_CKV + KV_LORA
OFF_GATTN = OFF_KW + 128
D_IN_PACKED = OFF_GATTN + D_ATTN

HALO = 32
CONV_ROWS = 64
LANES = 128
SUBLANES = 8
VMEM_LIMIT = 48 * 1024 * 1024

INT_MIN = -(2 ** 31)
NEG_KEY = -0x7F800000


def _dot(a, b):
    return jnp.dot(a, b, preferred_element_type=F32)


def _dot_nt(a, b):
    return lax.dot_general(a, b, (((1,), (1,)), ((), ())), preferred_element_type=F32)


def _dot_tn(a, b):
    return lax.dot_general(a, b, (((0,), (0,)), ((), ())), preferred_element_type=F32)


def _silu(v):
    return v * jax.nn.sigmoid(v)


def _mod_kernel(c_ref, w_ref, b_ref, o_ref):
    c = c_ref[...]
    o_ref[0] = _dot(_silu(c).astype(BF16), w_ref[0].astype(BF16)) + b_ref[0]


def _modulation(c, w_ada, b_ada):
    depth, d, d3 = w_ada.shape
    bsz = c.shape[0]
    tn = 1024
    return pl.pallas_call(
        _mod_kernel,
        name="adaln_mod",
        out_shape=jax.ShapeDtypeStruct((depth, bsz, d3), F32),
        grid=(depth, d3 // tn),
        in_specs=[
            pl.BlockSpec((bsz, d), lambda l, n: (0, 0)),
            pl.BlockSpec((1, d, tn), lambda l, n: (l, 0, n)),
            pl.BlockSpec((1, 1, tn), lambda l, n: (l, 0, n)),
        ],
        out_specs=pl.BlockSpec((1, bsz, tn), lambda l, n: (l, 0, n)),
        compiler_params=pltpu.CompilerParams(
            dimension_semantics=("arbitrary", "arbitrary"), vmem_limit_bytes=VMEM_LIMIT),
    )(c, w_ada, b_ada.reshape(depth, 1, d3))


def _t5_bucket(n):
    max_exact = N_BUCKETS // 2
    n = jnp.maximum(n, 0)
    nf = jnp.maximum(n, 1).astype(F32)
    large = max_exact + (jnp.log(nf / max_exact) / math.log(MAX_DISTANCE / max_exact)
                         * (N_BUCKETS - max_exact)).astype(I32)
    large = jnp.minimum(large, N_BUCKETS - 1)
    return jnp.where(n < max_exact, n, large)


def _bias_kernel(rb_ref, o_ref, *, t):
    sk = lax.broadcasted_iota(I32, (t, t), 0)
    tq = lax.broadcasted_iota(I32, (t, t), 1)
    for kind in range(2):
        bucket = _t5_bucket(tq - sk + kind * t)
        for h in range(N_HEADS):
            val = jnp.zeros((t, t), F32)
            for b in range(N_BUCKETS):
                val = jnp.where(bucket == b, rb_ref[b, h], val)
            o_ref[kind, h] = val


def _bias_tiles(rel_bias, t):
    return pl.pallas_call(
        functools.partial(_bias_kernel, t=t),
        name="t5_bias_tiles",
        out_shape=jax.ShapeDtypeStruct((2, N_HEADS, t, t), F32),
        in_specs=[pl.BlockSpec(memory_space=pltpu.SMEM)],
        out_specs=pl.BlockSpec(memory_space=pltpu.VMEM),
    )(rel_bias)


def _inproj_kernel(x_ref, mod_ref, gpre_ref, w_ref, cw_ref, cb_ref, lng_ref, lnb_ref, wpw2_ref,
                   qg_ref, wuq_ref, wqi_ref, kvg_ref,
                   yc_ref, q_ref, qi_ref, kv_ref, kvt_ref, ki_ref, wt_ref, sga_ref,
                   hbuf, abuf, ybuf, *, tm, d_model):
    j = pl.program_id(1)

    x = x_ref[0]
    ms = jnp.mean(x * x, axis=-1, keepdims=True)
    xn = (x * lax.rsqrt(ms + EPS)) * gpre_ref[...]
    shift = mod_ref[0, :, 0:d_model]
    scale = mod_ref[0, :, d_model:2 * d_model]
    hbuf[...] = (xn * (1.0 + scale) + shift).astype(BF16)

    u_val = _dot(hbuf[...], w_ref[:, OFF_UVAL:OFF_UVAL + D_CONV])
    u_gate = _dot(hbuf[...], w_ref[:, OFF_UGATE:OFF_UGATE + D_CONV])

    @pl.when(j == 0)
    def _():
        abuf[0:HALO, :] = jnp.zeros((HALO, D_CONV), F32)

    abuf[HALO:HALO + tm, :] = u_val * jax.nn.sigmoid(u_gate)

    first = HALO - (CONV_WIDTH - 1)
    for r0 in range(0, tm, CONV_ROWS):
        acc = jnp.broadcast_to(cb_ref[...], (CONV_ROWS, D_CONV))
        for k in range(CONV_WIDTH):
            acc = acc + cw_ref[k:k + 1, :] * abuf[r0 + first + k:r0 + first + k + CONV_ROWS, :]
        mu = jnp.mean(acc, axis=-1, keepdims=True)
        cen = acc - mu
        var = jnp.mean(cen * cen, axis=-1, keepdims=True)
        yn = cen * lax.rsqrt(var + EPS) * lng_ref[...] + lnb_ref[...]
        ybuf[r0:r0 + CONV_ROWS, :] = _silu(yn).astype(BF16)

    abuf[0:HALO, :] = abuf[tm:tm + HALO, :]

    g_conv = _dot(hbuf[...], w_ref[:, OFF_GCONV:OFF_GCONV + D_CONV])
    yc_ref[0] = (_dot(ybuf[...], wpw2_ref[...]) * _silu(g_conv)).astype(BF16)

    c_q = _dot(hbuf[...], w_ref[:, OFF_CQ:OFF_CQ + Q_LORA])
    cq = (c_q * lax.rsqrt(jnp.mean(c_q * c_q, axis=-1, keepdims=True) + EPS)) * qg_ref[...]
    cq = cq.astype(BF16)
    q_ref[0] = (_dot(cq, wuq_ref[...]) * (KV_LORA ** -0.5)).astype(BF16)
    qi_ref[0] = _dot(cq, wqi_ref[...]).astype(BF16)

    c_kv = _dot(hbuf[...], w_ref[:, OFF_CKV:OFF_CKV + KV_LORA])
    kvn = (c_kv * lax.rsqrt(jnp.mean(c_kv * c_kv, axis=-1, keepdims=True) + EPS)) * kvg_ref[...]
    kv_ref[0] = kvn.astype(BF16)
    kvt_ref[0] = jnp.transpose(kvn).astype(BF16)

    kw = _dot(hbuf[...], w_ref[:, OFF_KW:OFF_KW + 128])
    ki_ref[0] = kw[:, 0:D_IDX].astype(BF16)
    kwt = jnp.transpose(kw)
    wt_ref[0] = kwt[D_IDX:D_IDX + N_IDX_HEADS, :] * (N_IDX_HEADS ** -0.5 * D_IDX ** -0.5)

    g_attn = _dot(hbuf[...], w_ref[:, OFF_GATTN:OFF_GATTN + D_ATTN])
    sga_ref[0] = _silu(g_attn)


def _in_projection(x, mod_l, g_pre, w_packed, conv_w, conv_b, ln_g, ln_b, w_pw2, q_g, w_uq, w_qi,
                   kv_g, *, tm):
    bsz, s, d = x.shape
    row = lambda v: v.reshape(1, -1)
    full = lambda a: pl.BlockSpec(a.shape, lambda b, j: (0,) * a.ndim)
    args = (x, mod_l, row(g_pre), w_packed, conv_w, row(conv_b), row(ln_g), row(ln_b), w_pw2,
            row(q_g), w_uq, w_qi, row(kv_g))
    in_specs = [
        pl.BlockSpec((1, tm, d), lambda b, j: (b, j, 0)),
        pl.BlockSpec((1, 1, 3 * d), lambda b, j: (b, 0, 0)),
    ] + [full(a) for a in args[2:]]
    blk = lambda w: pl.BlockSpec((1, tm, w), lambda b, j: (b, j, 0))
    out_shape = (
        jax.ShapeDtypeStruct((bsz, s, D_CONV), BF16),
        jax.ShapeDtypeStruct((bsz, s, N_HEADS * KV_LORA), BF16),
        jax.ShapeDtypeStruct((bsz, s, N_IDX_HEADS * D_IDX), BF16),
        jax.ShapeDtypeStruct((bsz, s, KV_LORA), BF16),
        jax.ShapeDtypeStruct((bsz, KV_LORA, s), BF16),
        jax.ShapeDtypeStruct((bsz, s, D_IDX), BF16),
        jax.ShapeDtypeStruct((bsz, N_IDX_HEADS, s), F32),
        jax.ShapeDtypeStruct((bsz, s, D_ATTN), F32),
    )
    out_specs = (
        blk(D_CONV), blk(N_HEADS * KV_LORA), blk(N_IDX_HEADS * D_IDX), blk(KV_LORA),
        pl.BlockSpec((1, KV_LORA, tm), lambda b, j: (b, 0, j)),
        blk(D_IDX),
        pl.BlockSpec((1, N_IDX_HEADS, tm), lambda b, j: (b, 0, j)),
        blk(D_ATTN),
    )
    return pl.pallas_call(
        functools.partial(_inproj_kernel, tm=tm, d_model=d),
        name="in_projection",
        out_shape=out_shape,
        grid=(bsz, s // tm),
        in_specs=in_specs,
        out_specs=out_specs,
        scratch_shapes=[
            pltpu.VMEM((tm, d), BF16),
            pltpu.VMEM((tm + HALO, D_CONV), F32),
            pltpu.VMEM((tm, D_CONV), BF16),
        ],
        compiler_params=pltpu.CompilerParams(
            dimension_semantics=("arbitrary", "arbitrary"), vmem_limit_bytes=VMEM_LIMIT),
    )(*args)


def _sortable(v):
    b = pltpu.bitcast(v, I32)
    return jnp.where(b < 0, INT_MIN - b, b)


def _attn_kernel(q_ref, qi_ref, wt_ref, kv_ref, kvt_ref, ki_ref, sga_ref, wuv_ref, btile_ref, rb_ref,
                 o_ref, keys_ref, qis_ref, tau_ref, *, t, k_top):
    j = pl.program_id(1)
    nchunk = j + 1
    groups = t // SUBLANES

    def chunk_start(c):
        return pl.multiple_of(c * t, t)

    for h in range(N_IDX_HEADS):
        qis_ref[h] = qi_ref[0, :, h * D_IDX:(h + 1) * D_IDX]

    def score_chunk(c):
        kc = ki_ref[0, pl.ds(chunk_start(c), t), :]
        acc = jnp.zeros((t, t), F32)
        for h in range(N_IDX_HEADS):
            s = _dot_nt(kc, qis_ref[h])
            acc = acc + wt_ref[0, h:h + 1, :] * jnp.maximum(s, 0.0)
        return _sortable(acc)

    def score_body(c, carry):
        keys_ref[pl.ds(chunk_start(c), t), :] = score_chunk(c)
        return carry

    lax.fori_loop(0, j, score_body, 0)
    sk = lax.broadcasted_iota(I32, (t, t), 0)
    tq = lax.broadcasted_iota(I32, (t, t), 1)
    keys_ref[pl.ds(chunk_start(j), t), :] = jnp.where(sk <= tq, score_chunk(j), NEG_KEY)

    tau_ref[...] = jnp.full((SUBLANES, t), NEG_KEY + 1, I32)

    def count(pred):
        def body(c, cnt):
            kk = keys_ref[pl.ds(chunk_start(c), t), :].reshape(groups, SUBLANES, t)
            return cnt + jnp.sum(jnp.where(pred(kk, c * t), 1, 0), axis=0)
        cnt = lax.fori_loop(0, nchunk, body, jnp.zeros((SUBLANES, t), I32))
        return jnp.sum(cnt, axis=0, keepdims=True)

    @pl.when(nchunk * t > k_top)
    def _():
        def bit_body(i, carry):
            prefix, n_ge = carry
            cand = prefix | jnp.left_shift(jnp.int32(1), 31 - i)
            cand_s = jnp.broadcast_to(cand ^ INT_MIN, (SUBLANES, t))
            cnt = count(lambda kk, base: kk >= cand_s[None])
            ok = cnt >= k_top
            return jnp.where(ok, cand, prefix), jnp.where(ok, cnt, n_ge)

        prefix, n_ge = lax.fori_loop(
            0, 32, bit_body, (jnp.zeros((1, t), I32), jnp.full((1, t), nchunk * t, I32)))
        tau = prefix ^ INT_MIN
        tau_ref[...] = jnp.broadcast_to(jnp.maximum(tau, NEG_KEY + 1), (SUBLANES, t))

        tie = jnp.logical_and(n_ge > k_top, tau > NEG_KEY)

        @pl.when(jnp.max(jnp.where(tie, 1, 0)) > 0)
        def _():
            tau_b = jnp.broadcast_to(tau, (SUBLANES, t))
            n_gt = count(lambda kk, base: kk > tau_b[None])
            want = k_top - n_gt
            row = lax.broadcasted_iota(I32, (groups, SUBLANES, t), 0) * SUBLANES + \
                lax.broadcasted_iota(I32, (groups, SUBLANES, t), 1)

            def pos_body(i, pos):
                cand = pos | jnp.left_shift(jnp.int32(1), 15 - i)
                cand_b = jnp.broadcast_to(cand, (SUBLANES, t))
                cnt = count(lambda kk, base: jnp.logical_and(kk == tau_b[None],
                                                            row + base < cand_b[None]))
                return jnp.where(cnt < want, cand, pos)

            pos = lax.fori_loop(0, 16, pos_body, jnp.zeros((1, t), I32))
            pos_b = jnp.broadcast_to(jnp.where(tie, pos, jnp.int32(2 ** 30)), (SUBLANES, t))

            def demote_body(c, carry):
                kk = keys_ref[pl.ds(chunk_start(c), t), :].reshape(groups, SUBLANES, t)
                drop = jnp.logical_and(kk == tau_b[None], row + c * t > pos_b[None])
                keys_ref[pl.ds(chunk_start(c), t), :] = jnp.where(drop, NEG_KEY, kk).reshape(t, t)
                return carry

            lax.fori_loop(0, nchunk, demote_body, 0)

    tau_sel = tau_ref[...]
    ys = []
    for h in range(N_HEADS):
        qh = q_ref[0, :, h * KV_LORA:(h + 1) * KV_LORA]

        def attend(c, carry, bias):
            m, l, acc = carry
            kvc = kv_ref[0, pl.ds(chunk_start(c), t), :]
            s = _dot_nt(kvc, qh) + bias
            kk = keys_ref[pl.ds(chunk_start(c), t), :].reshape(groups, SUBLANES, t)
            sel = (kk >= tau_sel[None]).reshape(t, t)
            s = jnp.where(sel, s, -jnp.inf)
            m_new = jnp.maximum(m, jnp.max(s, axis=0, keepdims=True))
            m_safe = jnp.where(m_new == -jnp.inf, 0.0, m_new)
            alpha = jnp.exp(m - m_safe)
            p = jnp.exp(s - m_safe)
            l_new = alpha * l + jnp.sum(p, axis=0, keepdims=True)
            pv = _dot(kvt_ref[0, :, pl.ds(chunk_start(c), t)], p.astype(BF16))
            return m_new, l_new, alpha * acc + pv

        carry = (jnp.full((1, t), -jnp.inf, F32), jnp.zeros((1, t), F32),
                 jnp.zeros((KV_LORA, t), F32))
        far_bias = rb_ref[N_BUCKETS - 1, h]
        carry = lax.fori_loop(0, jnp.maximum(j - 1, 0),
                              lambda c, cr: attend(c, cr, far_bias), carry)
        carry = lax.fori_loop(jnp.maximum(j - 1, 0), j,
                              lambda c, cr: attend(c, cr, btile_ref[1, h]), carry)
        m, l, acc = attend(j, carry, btile_ref[0, h])
        o_t = (acc * (1.0 / l)).astype(BF16)
        ys.append(_dot_tn(o_t, wuv_ref[h]))

    y = jnp.concatenate(ys, axis=-1)
    o_ref[0] = (y * sga_ref[0]).astype(BF16)


def _sparse_attention(q, qi, wt, kv, kvt, ki, sga, w_uv, btiles, rel_bias, *, t):
    bsz, s, _ = q.shape
    k_top = min(TOPK_MAX, s // 4)
    blk = lambda w: pl.BlockSpec((1, t, w), lambda b, j: (b, j, 0))
    in_specs = [
        blk(N_HEADS * KV_LORA),
        blk(N_IDX_HEADS * D_IDX),
        pl.BlockSpec((1, N_IDX_HEADS, t), lambda b, j: (b, 0, j)),
        pl.BlockSpec((1, s, KV_LORA), lambda b, j: (b, 0, 0)),
        pl.BlockSpec((1, KV_LORA, s), lambda b, j: (b, 0, 0)),
        pl.BlockSpec((1, s, D_IDX), lambda b, j: (b, 0, 0)),
        blk(D_ATTN),
        pl.BlockSpec(w_uv.shape, lambda b, j: (0, 0, 0)),
        pl.BlockSpec(btiles.shape, lambda b, j: (0, 0, 0, 0)),
        pl.BlockSpec(memory_space=pltpu.SMEM),
    ]
    return pl.pallas_call(
        functools.partial(_attn_kernel, t=t, k_top=k_top),
        name="sparse_attention",
        out_shape=jax.ShapeDtypeStruct((bsz, s, D_ATTN), BF16),
        grid=(bsz, s // t),
        in_specs=in_specs,
        out_specs=blk(D_ATTN),
        scratch_shapes=[
            pltpu.VMEM((s, t), I32),
            pltpu.VMEM((N_IDX_HEADS, t, D_IDX), BF16),
            pltpu.VMEM((SUBLANES, t), I32),
        ],
        compiler_params=pltpu.CompilerParams(
            dimension_semantics=("arbitrary", "arbitrary"), vmem_limit_bytes=VMEM_LIMIT),
    )(q, qi, wt, kv, kvt, ki, sga, w_uv, btiles, rel_bias)


def _outproj_kernel(x_ref, yc_ref, ya_ref, wo_ref, mod_ref, gpost_ref, o_ref, *, d_model):
    y = _dot(jnp.concatenate([yc_ref[0], ya_ref[0]], axis=-1), wo_ref[...])
    yn = (y * lax.rsqrt(jnp.mean(y * y, axis=-1, keepdims=True) + EPS)) * gpost_ref[...]
    gate = mod_ref[0, :, 2 * d_model:3 * d_model]
    o_ref[0] = x_ref[0] + gate * yn


def _out_projection(x, yc, ya, w_out, mod_l, g_post, *, tm):
    bsz, s, d = x.shape
    return pl.pallas_call(
        functools.partial(_outproj_kernel, d_model=d),
        name="out_projection",
        out_shape=jax.ShapeDtypeStruct((bsz, s, d), F32),
        grid=(bsz, s // tm),
        in_specs=[
            pl.BlockSpec((1, tm, d), lambda b, j: (b, j, 0)),
            pl.BlockSpec((1, tm, D_CONV), lambda b, j: (b, j, 0)),
            pl.BlockSpec((1, tm, D_ATTN), lambda b, j: (b, j, 0)),
            pl.BlockSpec(w_out.shape, lambda b, j: (0, 0)),
            pl.BlockSpec((1, 1, 3 * d), lambda b, j: (b, 0, 0)),
            pl.BlockSpec((1, d), lambda b, j: (0, 0)),
        ],
        out_specs=pl.BlockSpec((1, tm, d), lambda b, j: (b, j, 0)),
        compiler_params=pltpu.CompilerParams(
            dimension_semantics=("arbitrary", "arbitrary"), vmem_limit_bytes=VMEM_LIMIT),
    )(x, yc, ya, w_out, mod_l, g_post.reshape(1, d))


def _pack_w_in(w_in_l):
    d = w_in_l.shape[0]
    split = OFF_KW + D_IDX + N_IDX_HEADS
    return jnp.concatenate(
        [w_in_l[:, :split], jnp.zeros((d, KW_PAD), w_in_l.dtype), w_in_l[:, split:]],
        axis=-1).astype(BF16)


def kernel(x, c, w_ada, b_ada, g_pre, w_in, conv_w, conv_b, conv_ln_g, conv_ln_b, w_pw2, q_norm_g,
           w_uq, w_qidx, kv_norm_g, w_uv, rel_bias, w_out, g_post):
    depth = w_ada.shape[0]
    bsz, s, d = x.shape
    tm = min(512, s)
    t = LANES
    assert s % tm == 0 and s % t == 0 and tm % CONV_ROWS == 0 and d % LANES == 0

    mod = _modulation(c, w_ada, b_ada)
    btiles = _bias_tiles(rel_bias, t)
    for l in range(depth):
        mod_l = mod[l].reshape(bsz, 1, 3 * d)
        yc, q, qi, kv, kvt, ki, wt, sga = _in_projection(
            x, mod_l, g_pre[l], _pack_w_in(w_in[l]), conv_w[l], conv_b[l], conv_ln_g[l],
            conv_ln_b[l], w_pw2[l].astype(BF16), q_norm_g[l], w_uq[l].astype(BF16),
            w_qidx[l].astype(BF16), kv_norm_g[l], tm=tm)
        ya = _sparse_attention(q, qi, wt, kv, kvt, ki, sga, w_uv[l].astype(BF16), btiles, rel_bias,
                               t=t)
        x = _out_projection(x, yc, ya, w_out[l].astype(BF16), mod_l, g_post[l], tm=tm)
    return x
```

```python
import functools
import math

import jax
import jax.numpy as jnp
from jax import lax
from jax.experimental import pallas as pl
from jax.experimental.pallas import tpu as pltpu

F32 = jnp.float32
BF16 = jnp.bfloat16
I32 = jnp.int32

D_CONV = 512
CONV_WIDTH = 31
N_HEADS = 8
D_HEAD_OUT = 64
D_ATTN = N_HEADS * D_HEAD_OUT
Q_LORA = 256
KV_LORA = 128
N_IDX_HEADS = 8
D_IDX = 64
TOPK_MAX = 256
N_BUCKETS = 32
MAX_DISTANCE = 128
EPS = 1e-6

KW_PAD = 128 - D_IDX - N_IDX_HEADS
OFF_UVAL = 0
OFF_UGATE = OFF_UVAL + D_CONV
OFF_GCONV = OFF_UGATE + D_CONV
OFF_CQ = OFF_GCONV + D_CONV
OFF_CKV = OFF_CQ + Q_LORA
OFF_KW = OFF_CKV + KV_LORA
OFF_GATTN = OFF_KW + 128
D_IN_PACKED = OFF_GATTN + D_ATTN

HALO = 32
CONV_ROWS = 64
ATTN_BLOCK = 256
LANES = 128
SUBLANES = 8
VMEM_LIMIT = 48 * 1024 * 1024

INT_MIN = -(2 ** 31)
NEG_KEY = -0x7F800000


def _dot(a, b):
    return jnp.dot(a, b, preferred_element_type=F32)


def _dot_nt(a, b):
    return lax.dot_general(a, b, (((1,), (1,)), ((), ())), preferred_element_type=F32)


def _dot_tn(a, b):
    return lax.dot_general(a, b, (((0,), (0,)), ((), ())), preferred_element_type=F32)


def _silu(v):
    return v * jax.nn.sigmoid(v)


def _mod_kernel(c_ref, w_ref, b_ref, o_ref):
    c = c_ref[...]
    o_ref[0] = _dot(_silu(c).astype(BF16), w_ref[0].astype(BF16)) + b_ref[0]


def _modulation(c, w_ada, b_ada):
    depth, d, d3 = w_ada.shape
    bsz = c.shape[0]
    tn = 1024
    return pl.pallas_call(
        _mod_kernel,
        name="adaln_mod",
        out_shape=jax.ShapeDtypeStruct((depth, bsz, d3), F32),
        grid=(depth, d3 // tn),
        in_specs=[
            pl.BlockSpec((bsz, d), lambda l, n: (0, 0)),
            pl.BlockSpec((1, d, tn), lambda l, n: (l, 0, n)),
            pl.BlockSpec((1, 1, tn), lambda l, n: (l, 0, n)),
        ],
        out_specs=pl.BlockSpec((1, bsz, tn), lambda l, n: (l, 0, n)),
        compiler_params=pltpu.CompilerParams(
            dimension_semantics=("arbitrary", "arbitrary"), vmem_limit_bytes=VMEM_LIMIT),
    )(c, w_ada, b_ada.reshape(depth, 1, d3))


def _t5_bucket(n):
    max_exact = N_BUCKETS // 2
    n = jnp.maximum(n, 0)
    nf = jnp.maximum(n, 1).astype(F32)
    large = max_exact + (jnp.log(nf / max_exact) / math.log(MAX_DISTANCE / max_exact)
                         * (N_BUCKETS - max_exact)).astype(I32)
    large = jnp.minimum(large, N_BUCKETS - 1)
    return jnp.where(n < max_exact, n, large)


def _bias_kernel(rb_ref, o_ref, *, t):
    sk = lax.broadcasted_iota(I32, (t, t), 0)
    tq = lax.broadcasted_iota(I32, (t, t), 1)
    for kind in range(2):
        bucket = _t5_bucket(tq - sk + kind * t)
        for h in range(N_HEADS):
            val = jnp.zeros((t, t), F32)
            for b in range(N_BUCKETS):
                val = jnp.where(bucket == b, rb_ref[b, h], val)
            o_ref[kind, h] = val


def _bias_tiles(rel_bias, t):
    return pl.pallas_call(
        functools.partial(_bias_kernel, t=t),
        name="t5_bias_tiles",
        out_shape=jax.ShapeDtypeStruct((2, N_HEADS, t, t), F32),
        in_specs=[pl.BlockSpec(memory_space=pltpu.SMEM)],
        out_specs=pl.BlockSpec(memory_space=pltpu.VMEM),
    )(rel_bias)


def _inproj_kernel(x_ref, mod_ref, gpre_ref, w_ref, cw_ref, cb_ref, lng_ref, lnb_ref, wpw2_ref,
                   qg_ref, wuq_ref, wqi_ref, kvg_ref,
                   yc_ref, q_ref, qi_ref, kv_ref, kvt_ref, ki_ref, wt_ref, sga_ref,
                   hbuf, abuf, ybuf, *, tm, d_model):
    j = pl.program_id(1)

    x = x_ref[0]
    ms = jnp.mean(x * x, axis=-1, keepdims=True)
    xn = (x * lax.rsqrt(ms + EPS)) * gpre_ref[...]
    shift = mod_ref[0, :, 0:d_model]
    scale = mod_ref[0, :, d_model:2 * d_model]
    hbuf[...] = (xn * (1.0 + scale) + shift).astype(BF16)

    u_val = _dot(hbuf[...], w_ref[:, OFF_UVAL:OFF_UVAL + D_CONV])
    u_gate = _dot(hbuf[...], w_ref[:, OFF_UGATE:OFF_UGATE + D_CONV])

    @pl.when(j == 0)
    def _():
        abuf[0:HALO, :] = jnp.zeros((HALO, D_CONV), F32)

    abuf[HALO:HALO + tm, :] = u_val * jax.nn.sigmoid(u_gate)

    first = HALO - (CONV_WIDTH - 1)
    for r0 in range(0, tm, CONV_ROWS):
        acc = jnp.broadcast_to(cb_ref[...], (CONV_ROWS, D_CONV))
        for k in range(CONV_WIDTH):
            acc = acc + cw_ref[k:k + 1, :] * abuf[r0 + first + k:r0 + first + k + CONV_ROWS, :]
        mu = jnp.mean(acc, axis=-1, keepdims=True)
        cen = acc - mu
        var = jnp.mean(cen * cen, axis=-1, keepdims=True)
        yn = cen * lax.rsqrt(var + EPS) * lng_ref[...] + lnb_ref[...]
        ybuf[r0:r0 + CONV_ROWS, :] = _silu(yn).astype(BF16)

    abuf[0:HALO, :] = abuf[tm:tm + HALO, :]

    g_conv = _dot(hbuf[...], w_ref[:, OFF_GCONV:OFF_GCONV + D_CONV])
    yc_ref[0] = (_dot(ybuf[...], wpw2_ref[...]) * _silu(g_conv)).astype(BF16)

    c_q = _dot(hbuf[...], w_ref[:, OFF_CQ:OFF_CQ + Q_LORA])
    cq = (c_q * lax.rsqrt(jnp.mean(c_q * c_q, axis=-1, keepdims=True) + EPS)) * qg_ref[...]
    cq = cq.astype(BF16)
    q_ref[0] = (_dot(cq, wuq_ref[...]) * (KV_LORA ** -0.5)).astype(BF16)
    qi_ref[0] = _dot(cq, wqi_ref[...]).astype(BF16)

    c_kv = _dot(hbuf[...], w_ref[:, OFF_CKV:OFF_CKV + KV_LORA])
    kvn = (c_kv * lax.rsqrt(jnp.mean(c_kv * c_kv, axis=-1, keepdims=True) + EPS)) * kvg_ref[...]
    kv_ref[0] = kvn.astype(BF16)
    kvt_ref[0] = jnp.transpose(kvn).astype(BF16)

    kw = _dot(hbuf[...], w_ref[:, OFF_KW:OFF_KW + 128])
    ki_ref[0] = kw[:, 0:D_IDX].astype(BF16)
    kwt = jnp.transpose(kw)
    wt_ref[0] = kwt[D_IDX:D_IDX + N_IDX_HEADS, :] * (N_IDX_HEADS ** -0.5 * D_IDX ** -0.5)

    g_attn = _dot(hbuf[...], w_ref[:, OFF_GATTN:OFF_GATTN + D_ATTN])
    sga_ref[0] = _silu(g_attn)


def _in_projection(x, mod_l, g_pre, w_packed, conv_w, conv_b, ln_g, ln_b, w_pw2, q_g, w_uq, w_qi,
                   kv_g, *, tm):
    bsz, s, d = x.shape
    row = lambda v: v.reshape(1, -1)
    full = lambda a: pl.BlockSpec(a.shape, lambda b, j: (0,) * a.ndim)
    args = (x, mod_l, row(g_pre), w_packed, conv_w, row(conv_b), row(ln_g), row(ln_b), w_pw2,
            row(q_g), w_uq, w_qi, row(kv_g))
    in_specs = [
        pl.BlockSpec((1, tm, d), lambda b, j: (b, j, 0)),
        pl.BlockSpec((1, 1, 3 * d), lambda b, j: (b, 0, 0)),
    ] + [full(a) for a in args[2:]]
    blk = lambda w: pl.BlockSpec((1, tm, w), lambda b, j: (b, j, 0))
    out_shape = (
        jax.ShapeDtypeStruct((bsz, s, D_CONV), BF16),
        jax.ShapeDtypeStruct((bsz, s, N_HEADS * KV_LORA), BF16),
        jax.ShapeDtypeStruct((bsz, s, N_IDX_HEADS * D_IDX), BF16),
        jax.ShapeDtypeStruct((bsz, s, KV_LORA), BF16),
        jax.ShapeDtypeStruct((bsz, KV_LORA, s), BF16),
        jax.ShapeDtypeStruct((bsz, s, D_IDX), BF16),
        jax.ShapeDtypeStruct((bsz, N_IDX_HEADS, s), F32),
        jax.ShapeDtypeStruct((bsz, s, D_ATTN), F32),
    )
    out_specs = (
        blk(D_CONV), blk(N_HEADS * KV_LORA), blk(N_IDX_HEADS * D_IDX), blk(KV_LORA),
        pl.BlockSpec((1, KV_LORA, tm), lambda b, j: (b, 0, j)),
        blk(D_IDX),
        pl.BlockSpec((1, N_IDX_HEADS, tm), lambda b, j: (b, 0, j)),
        blk(D_ATTN),
    )
    return pl.pallas_call(
        functools.partial(_inproj_kernel, tm=tm, d_model=d),
        name="in_projection",
        out_shape=out_shape,
        grid=(bsz, s // tm),
        in_specs=in_specs,
        out_specs=out_specs,
        scratch_shapes=[
            pltpu.VMEM((tm, d), BF16),
            pltpu.VMEM((tm + HALO, D_CONV), F32),
            pltpu.VMEM((tm, D_CONV), BF16),
        ],
        compiler_params=pltpu.CompilerParams(
            dimension_semantics=("arbitrary", "arbitrary"), vmem_limit_bytes=VMEM_LIMIT),
    )(*args)


def _sortable(v):
    b = pltpu.bitcast(v, I32)
    return jnp.where(b < 0, INT_MIN - b, b)


def _attn_kernel(q_ref, qi_ref, wt_ref, kv_ref, kvt_ref, ki_ref, sga_ref, wuv_ref, btile_ref, rb_ref,
                 o_ref, keys_ref, qis_ref, tau_ref, m_ref, l_ref, acc_ref, *, t, k_top):
    j = pl.program_id(1)
    nchunk = j + 1
    npair = (j + 2) // 2
    groups = t // SUBLANES

    def chunk_start(c):
        return pl.multiple_of(c * t, t)

    for h in range(N_IDX_HEADS):
        qis_ref[h * t:(h + 1) * t, :] = qi_ref[0, :, h * D_IDX:(h + 1) * D_IDX]

    def score_matmul(c):
        kc = ki_ref[0, pl.ds(chunk_start(c), t), :]
        return _dot_nt(kc, qis_ref[...])

    def score_finish(s_all):
        acc = jnp.zeros((t, t), F32)
        for h in range(N_IDX_HEADS):
            acc = acc + wt_ref[0, h:h + 1, :] * jnp.maximum(s_all[:, h * t:(h + 1) * t], 0.0)
        return _sortable(acc)

    def score_body(c, s_all):
        s_next = score_matmul(c + 1)
        keys_ref[pl.ds(chunk_start(c), t), :] = score_finish(s_all)
        return s_next

    s_all = lax.fori_loop(0, j, score_body, score_matmul(0))
    sk = lax.broadcasted_iota(I32, (t, t), 0)
    tq = lax.broadcasted_iota(I32, (t, t), 1)
    keys_ref[pl.ds(chunk_start(j), t), :] = jnp.where(sk <= tq, score_finish(s_all), NEG_KEY)
    keys_ref[pl.ds(chunk_start(j + 1), t), :] = jnp.full((t, t), NEG_KEY, I32)

    tau_ref[...] = jnp.full((SUBLANES, t), NEG_KEY + 1, I32)

    def count(pred):
        def body(cc, cnt):
            start = pl.multiple_of(cc * (2 * t), 2 * t)
            kk = keys_ref[pl.ds(start, 2 * t), :].reshape(2 * groups, SUBLANES, t)
            return cnt + jnp.sum(jnp.where(pred(kk, cc * (2 * t)), 1, 0), axis=0)
        cnt = lax.fori_loop(0, npair, body, jnp.zeros((SUBLANES, t), I32))
        return jnp.sum(cnt, axis=0, keepdims=True)

    @pl.when(nchunk * t > k_top)
    def _():
        def bit_body(i, carry):
            prefix, n_ge = carry
            cand = prefix | jnp.left_shift(jnp.int32(1), 31 - i)
            cand_s = jnp.broadcast_to(cand ^ INT_MIN, (SUBLANES, t))
            cnt = count(lambda kk, base: kk >= cand_s[None])
            ok = cnt >= k_top
            return jnp.where(ok, cand, prefix), jnp.where(ok, cnt, n_ge)

        prefix, n_ge = lax.fori_loop(
            0, 32, bit_body, (jnp.zeros((1, t), I32), jnp.full((1, t), nchunk * t, I32)))
        tau = prefix ^ INT_MIN
        tau_ref[...] = jnp.broadcast_to(jnp.maximum(tau, NEG_KEY + 1), (SUBLANES, t))

        tie = jnp.logical_and(n_ge > k_top, tau > NEG_KEY)

        @pl.when(jnp.max(jnp.where(tie, 1, 0)) > 0)
        def _():
            tau_b = jnp.broadcast_to(tau, (SUBLANES, t))
            n_gt = count(lambda kk, base: kk > tau_b[None])
            want = k_top - n_gt
            row = lax.broadcasted_iota(I32, (2 * groups, SUBLANES, t), 0) * SUBLANES + \
                lax.broadcasted_iota(I32, (2 * groups, SUBLANES, t), 1)

            def pos_body(i, pos):
                cand = pos | jnp.left_shift(jnp.int32(1), 15 - i)
                cand_b = jnp.broadcast_to(cand, (SUBLANES, t))
                cnt = count(lambda kk, base: jnp.logical_and(kk == tau_b[None],
                                                            row + base < cand_b[None]))
                return jnp.where(cnt < want, cand, pos)

            pos = lax.fori_loop(0, 16, pos_body, jnp.zeros((1, t), I32))
            pos_b = jnp.broadcast_to(jnp.where(tie, pos, jnp.int32(2 ** 30)), (SUBLANES, t))

            def demote_body(cc, carry):
                start = pl.multiple_of(cc * (2 * t), 2 * t)
                kk = keys_ref[pl.ds(start, 2 * t), :].reshape(2 * groups, SUBLANES, t)
                drop = jnp.logical_and(kk == tau_b[None], row + cc * (2 * t) > pos_b[None])
                keys_ref[pl.ds(start, 2 * t), :] = jnp.where(drop, NEG_KEY, kk).reshape(2 * t, t)
                return carry

            lax.fori_loop(0, npair, demote_body, 0)

    m_ref[...] = jnp.full(m_ref.shape, -jnp.inf, F32)
    l_ref[...] = jnp.zeros(l_ref.shape, F32)
    acc_ref[...] = jnp.zeros(acc_ref.shape, F32)
    tau_sel = tau_ref[...]

    def logits_matmul(c):
        kvc = kv_ref[0, pl.ds(chunk_start(c), t), :]
        return tuple(_dot_nt(kvc, q_ref[0, :, h * KV_LORA:(h + 1) * KV_LORA])
                     for h in range(N_HEADS))

    def attend(c, logits, bias_of_head):
        kvtc = kvt_ref[0, :, pl.ds(chunk_start(c), t)]
        kk = keys_ref[pl.ds(chunk_start(c), t), :].reshape(groups, SUBLANES, t)
        sel = (kk >= tau_sel[None]).reshape(t, t)
        for h in range(N_HEADS):
            s = jnp.where(sel, logits[h] + bias_of_head(h), -jnp.inf)
            m_old = m_ref[h]
            m_new = jnp.maximum(m_old, jnp.max(s, axis=0, keepdims=True))
            m_safe = jnp.where(m_new == -jnp.inf, 0.0, m_new)
            alpha = jnp.exp(m_old - m_safe)
            p = jnp.exp(s - m_safe)
            m_ref[h] = m_new
            l_ref[h] = alpha * l_ref[h] + jnp.sum(p, axis=0, keepdims=True)
            acc_ref[h] = alpha * acc_ref[h] + _dot(kvtc, p.astype(BF16))

    def far_body(c, logits):
        nxt = logits_matmul(c + 1)
        attend(c, logits, lambda h: rb_ref[N_BUCKETS - 1, h])
        return nxt

    def near_body(c, logits):
        nxt = logits_matmul(c + 1)
        attend(c, logits, lambda h: btile_ref[1, h])
        return nxt

    logits = lax.fori_loop(0, jnp.maximum(j - 1, 0), far_body, logits_matmul(0))
    logits = lax.fori_loop(jnp.maximum(j - 1, 0), j, near_body, logits)
    attend(j, logits, lambda h: btile_ref[0, h])

    ys = []
    for h in range(N_HEADS):
        o_t = (acc_ref[h] * (1.0 / l_ref[h])).astype(BF16)
        ys.append(_dot_tn(o_t, wuv_ref[h]))
    y = jnp.concatenate(ys, axis=-1)
    o_ref[0] = (y * sga_ref[0]).astype(BF16)


def _sparse_attention(q, qi, wt, kv, kvt, ki, sga, w_uv, btiles, rel_bias, *, t):
    bsz, s, _ = q.shape
    k_top = min(TOPK_MAX, s // 4)
    blk = lambda w: pl.BlockSpec((1, t, w), lambda b, j: (b, j, 0))
    in_specs = [
        blk(N_HEADS * KV_LORA),
        blk(N_IDX_HEADS * D_IDX),
        pl.BlockSpec((1, N_IDX_HEADS, t), lambda b, j: (b, 0, j)),
        pl.BlockSpec((1, s, KV_LORA), lambda b, j: (b, 0, 0)),
        pl.BlockSpec((1, KV_LORA, s), lambda b, j: (b, 0, 0)),
        pl.BlockSpec((1, s, D_IDX), lambda b, j: (b, 0, 0)),
        blk(D_ATTN),
        pl.BlockSpec(w_uv.shape, lambda b, j: (0, 0, 0)),
        pl.BlockSpec(btiles.shape, lambda b, j: (0, 0, 0, 0)),
        pl.BlockSpec(memory_space=pltpu.SMEM),
    ]
    return pl.pallas_call(
        functools.partial(_attn_kernel, t=t, k_top=k_top),
        name="sparse_attention",
        out_shape=jax.ShapeDtypeStruct((bsz, s, D_ATTN), BF16),
        grid=(bsz, s // t),
        in_specs=in_specs,
        out_specs=blk(D_ATTN),
        scratch_shapes=[
            pltpu.VMEM((s + t, t), I32),
            pltpu.VMEM((N_IDX_HEADS * t, D_IDX), BF16),
            pltpu.VMEM((SUBLANES, t), I32),
            pltpu.VMEM((N_HEADS, 1, t), F32),
            pltpu.VMEM((N_HEADS, 1, t), F32),
            pltpu.VMEM((N_HEADS, KV_LORA, t), F32),
        ],
        compiler_params=pltpu.CompilerParams(
            dimension_semantics=("arbitrary", "arbitrary"), vmem_limit_bytes=VMEM_LIMIT),
    )(q, qi, wt, kv, kvt, ki, sga, w_uv, btiles, rel_bias)


def _outproj_kernel(x_ref, yc_ref, ya_ref, wo_ref, mod_ref, gpost_ref, o_ref, *, d_model):
    y = _dot(jnp.concatenate([yc_ref[0], ya_ref[0]], axis=-1), wo_ref[...])
    yn = (y * lax.rsqrt(jnp.mean(y * y, axis=-1, keepdims=True) + EPS)) * gpost_ref[...]
    gate = mod_ref[0, :, 2 * d_model:3 * d_model]
    o_ref[0] = x_ref[0] + gate * yn


def _out_projection(x, yc, ya, w_out, mod_l, g_post, *, tm):
    bsz, s, d = x.shape
    return pl.pallas_call(
        functools.partial(_outproj_kernel, d_model=d),
        name="out_projection",
        out_shape=jax.ShapeDtypeStruct((bsz, s, d), F32),
        grid=(bsz, s // tm),
        in_specs=[
            pl.BlockSpec((1, tm, d), lambda b, j: (b, j, 0)),
            pl.BlockSpec((1, tm, D_CONV), lambda b, j: (b, j, 0)),
            pl.BlockSpec((1, tm, D_ATTN), lambda b, j: (b, j, 0)),
            pl.BlockSpec(w_out.shape, lambda b, j: (0, 0)),
            pl.BlockSpec((1, 1, 3 * d), lambda b, j: (b, 0, 0)),
            pl.BlockSpec((1, d), lambda b, j: (0, 0)),
        ],
        out_specs=pl.BlockSpec((1, tm, d), lambda b, j: (b, j, 0)),
        compiler_params=pltpu.CompilerParams(
            dimension_semantics=("arbitrary", "arbitrary"), vmem_limit_bytes=VMEM_LIMIT),
    )(x, yc, ya, w_out, mod_l, g_post.reshape(1, d))


def _pack_w_in(w_in_l):
    d = w_in_l.shape[0]
    split = OFF_KW + D_IDX + N_IDX_HEADS
    return jnp.concatenate(
        [w_in_l[:, :split], jnp.zeros((d, KW_PAD), w_in_l.dtype), w_in_l[:, split:]],
        axis=-1).astype(BF16)


def kernel(x, c, w_ada, b_ada, g_pre, w_in, conv_w, conv_b, conv_ln_g, conv_ln_b, w_pw2, q_norm_g,
           w_uq, w_qidx, kv_norm_g, w_uv, rel_bias, w_out, g_post):
    depth = w_ada.shape[0]
    bsz, s, d = x.shape
    tm = min(512, s)
    t = min(ATTN_BLOCK, s)
    assert s % tm == 0 and s % t == 0 and tm % CONV_ROWS == 0 and d % LANES == 0

    mod = _modulation(c, w_ada, b_ada)
    btiles = _bias_tiles(rel_bias, t)
    for l in range(depth):
        mod_l = mod[l].reshape(bsz, 1, 3 * d)
        yc, q, qi, kv, kvt, ki, wt, sga = _in_projection(
            x, mod_l, g_pre[l], _pack_w_in(w_in[l]), conv_w[l], conv_b[l], conv_ln_g[l],
            conv_ln_b[l], w_pw2[l].astype(BF16), q_norm_g[l], w_uq[l].astype(BF16),
            w_qidx[l].astype(BF16), kv_norm_g[l], tm=tm)
        ya = _sparse_attention(q, qi, wt, kv, kvt, ki, sga, w_uv[l].astype(BF16), btiles, rel_bias,
                               t=t)
        x = _out_projection(x, yc, ya, w_out[l].astype(BF16), mod_l, g_post[l], tm=tm)
    return x
```

```python
import functools
import math

import jax
import jax.numpy as jnp
from jax import lax
from jax.experimental import pallas as pl
from jax.experimental.pallas import tpu as pltpu

F32 = jnp.float32
BF16 = jnp.bfloat16
I32 = jnp.int32

D_CONV = 512
CONV_WIDTH = 31
N_HEADS = 8
D_HEAD_OUT = 64
D_ATTN = N_HEADS * D_HEAD_OUT
Q_LORA = 256
KV_LORA = 128
N_IDX_HEADS = 8
D_IDX = 64
TOPK_MAX = 256
N_BUCKETS = 32
MAX_DISTANCE = 128
EPS = 1e-6

KW_PAD = 128 - D_IDX - N_IDX_HEADS
OFF_UVAL = 0
OFF_UGATE = OFF_UVAL + D_CONV
OFF_GCONV = OFF_UGATE + D_CONV
OFF_CQ = OFF_GCONV + D_CONV
OFF_CKV = OFF_CQ + Q_LORA
OFF_KW = OFF_CKV + KV_LORA
OFF_GATTN = OFF_KW + 128
D_IN_PACKED = OFF_GATTN + D_ATTN

HALO = 32
CONV_ROWS = 64
ATTN_BLOCK = 256
LANES = 128
SUBLANES = 8
VMEM_LIMIT = 48 * 1024 * 1024

INT_MIN = -(2 ** 31)
NEG_KEY = -0x7F800000
LOG2E = math.log2(math.e)


def _dot(a, b):
    return jnp.dot(a, b, preferred_element_type=F32)


def _dot_nt(a, b):
    return lax.dot_general(a, b, (((1,), (1,)), ((), ())), preferred_element_type=F32)


def _dot_tn(a, b):
    return lax.dot_general(a, b, (((0,), (0,)), ((), ())), preferred_element_type=F32)


def _silu(v):
    return v * jax.nn.sigmoid(v)


def _mod_kernel(c_ref, w_ref, b_ref, o_ref):
    c = c_ref[...]
    o_ref[0] = _dot(_silu(c).astype(BF16), w_ref[0].astype(BF16)) + b_ref[0]


def _modulation(c, w_ada, b_ada):
    depth, d, d3 = w_ada.shape
    bsz = c.shape[0]
    tn = 1024
    return pl.pallas_call(
        _mod_kernel,
        name="adaln_mod",
        out_shape=jax.ShapeDtypeStruct((depth, bsz, d3), F32),
        grid=(depth, d3 // tn),
        in_specs=[
            pl.BlockSpec((bsz, d), lambda l, n: (0, 0)),
            pl.BlockSpec((1, d, tn), lambda l, n: (l, 0, n)),
            pl.BlockSpec((1, 1, tn), lambda l, n: (l, 0, n)),
        ],
        out_specs=pl.BlockSpec((1, bsz, tn), lambda l, n: (l, 0, n)),
        compiler_params=pltpu.CompilerParams(
            dimension_semantics=("arbitrary", "arbitrary"), vmem_limit_bytes=VMEM_LIMIT),
    )(c, w_ada, b_ada.reshape(depth, 1, d3))


def _t5_bucket(n):
    max_exact = N_BUCKETS // 2
    n = jnp.maximum(n, 0)
    nf = jnp.maximum(n, 1).astype(F32)
    large = max_exact + (jnp.log(nf / max_exact) / math.log(MAX_DISTANCE / max_exact)
                         * (N_BUCKETS - max_exact)).astype(I32)
    large = jnp.minimum(large, N_BUCKETS - 1)
    return jnp.where(n < max_exact, n, large)


def _bias_kernel(rb_ref, o_ref, *, t):
    sk = lax.broadcasted_iota(I32, (t, t), 0)
    tq = lax.broadcasted_iota(I32, (t, t), 1)
    for kind in range(2):
        bucket = _t5_bucket(tq - sk + kind * t)
        for h in range(N_HEADS):
            val = jnp.zeros((t, t), F32)
            for b in range(N_BUCKETS):
                val = jnp.where(bucket == b, rb_ref[b, h], val)
            o_ref[kind, h] = (val - rb_ref[N_BUCKETS - 1, h]) * LOG2E


def _bias_tiles(rel_bias, t):
    return pl.pallas_call(
        functools.partial(_bias_kernel, t=t),
        name="t5_bias_tiles",
        out_shape=jax.ShapeDtypeStruct((2, N_HEADS, t, t), F32),
        in_specs=[pl.BlockSpec(memory_space=pltpu.SMEM)],
        out_specs=pl.BlockSpec(memory_space=pltpu.VMEM),
    )(rel_bias)


def _inproj_kernel(x_ref, mod_ref, gpre_ref, w_ref, cw_ref, cb_ref, lng_ref, lnb_ref, wpw2_ref,
                   qg_ref, wuq_ref, wqi_ref, kvg_ref,
                   yc_ref, q_ref, qi_ref, kv_ref, kvt_ref, ki_ref, wt_ref, sga_ref,
                   hbuf, abuf, ybuf, *, tm, d_model):
    j = pl.program_id(1)

    x = x_ref[0]
    ms = jnp.mean(x * x, axis=-1, keepdims=True)
    xn = (x * lax.rsqrt(ms + EPS)) * gpre_ref[...]
    shift = mod_ref[0, :, 0:d_model]
    scale = mod_ref[0, :, d_model:2 * d_model]
    hbuf[...] = (xn * (1.0 + scale) + shift).astype(BF16)

    u_val = _dot(hbuf[...], w_ref[:, OFF_UVAL:OFF_UVAL + D_CONV])
    u_gate = _dot(hbuf[...], w_ref[:, OFF_UGATE:OFF_UGATE + D_CONV])

    @pl.when(j == 0)
    def _():
        abuf[0:HALO, :] = jnp.zeros((HALO, D_CONV), F32)

    abuf[HALO:HALO + tm, :] = u_val * jax.nn.sigmoid(u_gate)

    first = HALO - (CONV_WIDTH - 1)
    for r0 in range(0, tm, CONV_ROWS):
        acc = jnp.broadcast_to(cb_ref[...], (CONV_ROWS, D_CONV))
        for k in range(CONV_WIDTH):
            acc = acc + cw_ref[k:k + 1, :] * abuf[r0 + first + k:r0 + first + k + CONV_ROWS, :]
        mu = jnp.mean(acc, axis=-1, keepdims=True)
        cen = acc - mu
        var = jnp.mean(cen * cen, axis=-1, keepdims=True)
        yn = cen * lax.rsqrt(var + EPS) * lng_ref[...] + lnb_ref[...]
        ybuf[r0:r0 + CONV_ROWS, :] = _silu(yn).astype(BF16)

    abuf[0:HALO, :] = abuf[tm:tm + HALO, :]

    g_conv = _dot(hbuf[...], w_ref[:, OFF_GCONV:OFF_GCONV + D_CONV])
    yc_ref[0] = (_dot(ybuf[...], wpw2_ref[...]) * _silu(g_conv)).astype(BF16)

    c_q = _dot(hbuf[...], w_ref[:, OFF_CQ:OFF_CQ + Q_LORA])
    cq = (c_q * lax.rsqrt(jnp.mean(c_q * c_q, axis=-1, keepdims=True) + EPS)) * qg_ref[...]
    cq = cq.astype(BF16)
    q_ref[0] = (_dot(cq, wuq_ref[...]) * (KV_LORA ** -0.5 * LOG2E)).astype(BF16)
    qi_ref[0] = _dot(cq, wqi_ref[...]).astype(BF16)

    c_kv = _dot(hbuf[...], w_ref[:, OFF_CKV:OFF_CKV + KV_LORA])
    kvn = (c_kv * lax.rsqrt(jnp.mean(c_kv * c_kv, axis=-1, keepdims=True) + EPS)) * kvg_ref[...]
    kv_ref[0] = kvn.astype(BF16)
    kvt_ref[0] = jnp.transpose(kvn).astype(BF16)

    kw = _dot(hbuf[...], w_ref[:, OFF_KW:OFF_KW + 128])
    ki_ref[0] = kw[:, 0:D_IDX].astype(BF16)
    kwt = jnp.transpose(kw)
    wt_ref[0] = kwt[D_IDX:D_IDX + N_IDX_HEADS, :] * (N_IDX_HEADS ** -0.5 * D_IDX ** -0.5)

    g_attn = _dot(hbuf[...], w_ref[:, OFF_GATTN:OFF_GATTN + D_ATTN])
    sga_ref[0] = _silu(g_attn)


def _in_projection(x, mod_l, g_pre, w_packed, conv_w, conv_b, ln_g, ln_b, w_pw2, q_g, w_uq, w_qi,
                   kv_g, *, tm):
    bsz, s, d = x.shape
    row = lambda v: v.reshape(1, -1)
    full = lambda a: pl.BlockSpec(a.shape, lambda b, j: (0,) * a.ndim)
    args = (x, mod_l, row(g_pre), w_packed, conv_w, row(conv_b), row(ln_g), row(ln_b), w_pw2,
            row(q_g), w_uq, w_qi, row(kv_g))
    in_specs = [
        pl.BlockSpec((1, tm, d), lambda b, j: (b, j, 0)),
        pl.BlockSpec((1, 1, 3 * d), lambda b, j: (b, 0, 0)),
    ] + [full(a) for a in args[2:]]
    blk = lambda w: pl.BlockSpec((1, tm, w), lambda b, j: (b, j, 0))
    out_shape = (
        jax.ShapeDtypeStruct((bsz, s, D_CONV), BF16),
        jax.ShapeDtypeStruct((bsz, s, N_HEADS * KV_LORA), BF16),
        jax.ShapeDtypeStruct((bsz, s, N_IDX_HEADS * D_IDX), BF16),
        jax.ShapeDtypeStruct((bsz, s, KV_LORA), BF16),
        jax.ShapeDtypeStruct((bsz, KV_LORA, s), BF16),
        jax.ShapeDtypeStruct((bsz, s, D_IDX), BF16),
        jax.ShapeDtypeStruct((bsz, N_IDX_HEADS, s), F32),
        jax.ShapeDtypeStruct((bsz, s, D_ATTN), F32),
    )
    out_specs = (
        blk(D_CONV), blk(N_HEADS * KV_LORA), blk(N_IDX_HEADS * D_IDX), blk(KV_LORA),
        pl.BlockSpec((1, KV_LORA, tm), lambda b, j: (b, 0, j)),
        blk(D_IDX),
        pl.BlockSpec((1, N_IDX_HEADS, tm), lambda b, j: (b, 0, j)),
        blk(D_ATTN),
    )
    return pl.pallas_call(
        functools.partial(_inproj_kernel, tm=tm, d_model=d),
        name="in_projection",
        out_shape=out_shape,
        grid=(bsz, s // tm),
        in_specs=in_specs,
        out_specs=out_specs,
        scratch_shapes=[
            pltpu.VMEM((tm, d), BF16),
            pltpu.VMEM((tm + HALO, D_CONV), F32),
            pltpu.VMEM((tm, D_CONV), BF16),
        ],
        compiler_params=pltpu.CompilerParams(
            dimension_semantics=("arbitrary", "arbitrary"), vmem_limit_bytes=VMEM_LIMIT),
    )(*args)


def _sortable(v):
    b = pltpu.bitcast(v, I32)
    return jnp.where(b < 0, INT_MIN - b, b)


def _attn_kernel(q_ref, qi_ref, wt_ref, kv_ref, kvt_ref, ki_ref, sga_ref, wuv_ref, btile_ref,
                 o_ref, keys_ref, qis_ref, tau_ref, m_ref, l_ref, acc_ref, sa0_ref, sa1_ref,
                 s2a_ref, s2b_ref, *, t, k_top):
    sa_refs = (sa0_ref, sa1_ref)
    s2_refs = (s2a_ref, s2b_ref)
    j = pl.program_id(1)
    nchunk = j + 1
    npair = (j + 2) // 2
    groups = t // SUBLANES

    def chunk_start(c):
        return pl.multiple_of(c * t, t)

    for h in range(N_IDX_HEADS):
        qis_ref[h * t:(h + 1) * t, :] = qi_ref[0, :, h * D_IDX:(h + 1) * D_IDX]

    def by_parity(c, fn):
        for parity, (cur, nxt) in enumerate(((0, 1), (1, 0))):
            pl.when((c & 1) == parity)(functools.partial(fn, cur, nxt))

    def score_matmul(c, dst):
        kc = ki_ref[0, pl.ds(chunk_start(c), t), :]
        dst[...] = _dot_nt(kc, qis_ref[...])

    def score_finish(src):
        acc = jnp.zeros((t, t), F32)
        for h in range(N_IDX_HEADS):
            acc = acc + wt_ref[0, h:h + 1, :] * jnp.maximum(src[:, h * t:(h + 1) * t], 0.0)
        return _sortable(acc)

    def score_body(c, carry):
        def step(cur, nxt):
            score_matmul(c + 1, sa_refs[nxt])
            keys_ref[pl.ds(chunk_start(c), t), :] = score_finish(sa_refs[cur])
        by_parity(c, step)
        return carry

    score_matmul(0, sa_refs[0])
    lax.fori_loop(0, j, score_body, 0)
    sk = lax.broadcasted_iota(I32, (t, t), 0)
    tq = lax.broadcasted_iota(I32, (t, t), 1)

    def score_diag(cur, nxt):
        keys_ref[pl.ds(chunk_start(j), t), :] = jnp.where(sk <= tq, score_finish(sa_refs[cur]),
                                                          NEG_KEY)
    by_parity(j, score_diag)
    keys_ref[pl.ds(chunk_start(j + 1), t), :] = jnp.full((t, t), NEG_KEY, I32)

    tau_ref[...] = jnp.full((SUBLANES, t), NEG_KEY + 1, I32)

    def count(pred):
        def body(cc, cnt):
            start = pl.multiple_of(cc * (2 * t), 2 * t)
            kk = keys_ref[pl.ds(start, 2 * t), :].reshape(2 * groups, SUBLANES, t)
            return cnt + jnp.sum(jnp.where(pred(kk, cc * (2 * t)), 1, 0), axis=0)
        cnt = lax.fori_loop(0, npair, body, jnp.zeros((SUBLANES, t), I32))
        return jnp.sum(cnt, axis=0, keepdims=True)

    @pl.when(nchunk * t > k_top)
    def _():
        def bit_body(i, carry):
            prefix, n_ge = carry
            cand = prefix | jnp.left_shift(jnp.int32(1), 31 - i)
            cand_s = jnp.broadcast_to(cand ^ INT_MIN, (SUBLANES, t))
            cnt = count(lambda kk, base: kk >= cand_s[None])
            ok = cnt >= k_top
            return jnp.where(ok, cand, prefix), jnp.where(ok, cnt, n_ge)

        prefix, n_ge = lax.fori_loop(
            0, 32, bit_body, (jnp.zeros((1, t), I32), jnp.full((1, t), nchunk * t, I32)))
        tau = prefix ^ INT_MIN
        tau_ref[...] = jnp.broadcast_to(jnp.maximum(tau, NEG_KEY + 1), (SUBLANES, t))

        tie = jnp.logical_and(n_ge > k_top, tau > NEG_KEY)

        @pl.when(jnp.max(jnp.where(tie, 1, 0)) > 0)
        def _():
            tau_b = jnp.broadcast_to(tau, (SUBLANES, t))
            n_gt = count(lambda kk, base: kk > tau_b[None])
            want = k_top - n_gt
            row = lax.broadcasted_iota(I32, (2 * groups, SUBLANES, t), 0) * SUBLANES + \
                lax.broadcasted_iota(I32, (2 * groups, SUBLANES, t), 1)

            def pos_body(i, pos):
                cand = pos | jnp.left_shift(jnp.int32(1), 15 - i)
                cand_b = jnp.broadcast_to(cand, (SUBLANES, t))
                cnt = count(lambda kk, base: jnp.logical_and(kk == tau_b[None],
                                                            row + base < cand_b[None]))
                return jnp.where(cnt < want, cand, pos)

            pos = lax.fori_loop(0, 16, pos_body, jnp.zeros((1, t), I32))
            pos_b = jnp.broadcast_to(jnp.where(tie, pos, jnp.int32(2 ** 30)), (SUBLANES, t))

            def demote_body(cc, carry):
                start = pl.multiple_of(cc * (2 * t), 2 * t)
                kk = keys_ref[pl.ds(start, 2 * t), :].reshape(2 * groups, SUBLANES, t)
                drop = jnp.logical_and(kk == tau_b[None], row + cc * (2 * t) > pos_b[None])
                keys_ref[pl.ds(start, 2 * t), :] = jnp.where(drop, NEG_KEY, kk).reshape(2 * t, t)
                return carry

            lax.fori_loop(0, npair, demote_body, 0)

    m_ref[...] = jnp.full(m_ref.shape, -jnp.inf, F32)
    l_ref[...] = jnp.zeros(l_ref.shape, F32)
    acc_ref[...] = jnp.zeros(acc_ref.shape, F32)
    tau_sel = tau_ref[...]

    def logits_matmul(c, dst):
        kvc = kv_ref[0, pl.ds(chunk_start(c), t), :]
        for h in range(N_HEADS):
            dst[h] = _dot_nt(kvc, q_ref[0, :, h * KV_LORA:(h + 1) * KV_LORA])

    def attend(c, src, bias_tile):
        kvtc = kvt_ref[0, :, pl.ds(chunk_start(c), t)]
        kk = keys_ref[pl.ds(chunk_start(c), t), :].reshape(groups, SUBLANES, t)
        sel = (kk >= tau_sel[None]).reshape(t, t)
        for h in range(N_HEADS):
            s = src[h] if bias_tile is None else src[h] + btile_ref[bias_tile, h]
            s = jnp.where(sel, s, -jnp.inf)
            m_old = m_ref[h]
            m_new = jnp.maximum(m_old, jnp.max(s, axis=0, keepdims=True))
            m_safe = jnp.where(m_new == -jnp.inf, 0.0, m_new)
            alpha = jnp.exp2(m_old - m_safe)
            p = jnp.exp2(s - m_safe)
            m_ref[h] = m_new
            l_ref[h] = alpha * l_ref[h] + jnp.sum(p, axis=0, keepdims=True)
            acc_ref[h] = alpha * acc_ref[h] + _dot(kvtc, p.astype(BF16))

    def attend_loop(lo, hi, bias_tile):
        def body(c, carry):
            def step(cur, nxt):
                logits_matmul(c + 1, s2_refs[nxt])
                attend(c, s2_refs[cur], bias_tile)
            by_parity(c, step)
            return carry
        lax.fori_loop(lo, hi, body, 0)

    logits_matmul(0, s2_refs[0])
    attend_loop(0, jnp.maximum(j - 1, 0), None)
    attend_loop(jnp.maximum(j - 1, 0), j, 1)
    by_parity(j, lambda cur, nxt: attend(j, s2_refs[cur], 0))

    ys = []
    for h in range(N_HEADS):
        o_t = (acc_ref[h] * (1.0 / l_ref[h])).astype(BF16)
        ys.append(_dot_tn(o_t, wuv_ref[h]))
    y = jnp.concatenate(ys, axis=-1)
    o_ref[0] = (y * sga_ref[0]).astype(BF16)


def _sparse_attention(q, qi, wt, kv, kvt, ki, sga, w_uv, btiles, *, t):
    bsz, s, _ = q.shape
    k_top = min(TOPK_MAX, s // 4)
    blk = lambda w: pl.BlockSpec((1, t, w), lambda b, j: (b, j, 0))
    in_specs = [
        blk(N_HEADS * KV_LORA),
        blk(N_IDX_HEADS * D_IDX),
        pl.BlockSpec((1, N_IDX_HEADS, t), lambda b, j: (b, 0, j)),
        pl.BlockSpec((1, s, KV_LORA), lambda b, j: (b, 0, 0)),
        pl.BlockSpec((1, KV_LORA, s), lambda b, j: (b, 0, 0)),
        pl.BlockSpec((1, s, D_IDX), lambda b, j: (b, 0, 0)),
        blk(D_ATTN),
        pl.BlockSpec(w_uv.shape, lambda b, j: (0, 0, 0)),
        pl.BlockSpec(btiles.shape, lambda b, j: (0, 0, 0, 0)),
    ]
    return pl.pallas_call(
        functools.partial(_attn_kernel, t=t, k_top=k_top),
        name="sparse_attention",
        out_shape=jax.ShapeDtypeStruct((bsz, s, D_ATTN), BF16),
        grid=(bsz, s // t),
        in_specs=in_specs,
        out_specs=blk(D_ATTN),
        scratch_shapes=[
            pltpu.VMEM((s + t, t), I32),
            pltpu.VMEM((N_IDX_HEADS * t, D_IDX), BF16),
            pltpu.VMEM((SUBLANES, t), I32),
            pltpu.VMEM((N_HEADS, 1, t), F32),
            pltpu.VMEM((N_HEADS, 1, t), F32),
            pltpu.VMEM((N_HEADS, KV_LORA, t), F32),
            pltpu.VMEM((t, N_IDX_HEADS * t), F32),
            pltpu.VMEM((t, N_IDX_HEADS * t), F32),
            pltpu.VMEM((N_HEADS, t, t), F32),
            pltpu.VMEM((N_HEADS, t, t), F32),
        ],
        compiler_params=pltpu.CompilerParams(
            dimension_semantics=("arbitrary", "arbitrary"), vmem_limit_bytes=VMEM_LIMIT),
    )(q, qi, wt, kv, kvt, ki, sga, w_uv, btiles)


def _outproj_kernel(x_ref, yc_ref, ya_ref, wo_ref, mod_ref, gpost_ref, o_ref, *, d_model):
    y = _dot(jnp.concatenate([yc_ref[0], ya_ref[0]], axis=-1), wo_ref[...])
    yn = (y * lax.rsqrt(jnp.mean(y * y, axis=-1, keepdims=True) + EPS)) * gpost_ref[...]
    gate = mod_ref[0, :, 2 * d_model:3 * d_model]
    o_ref[0] = x_ref[0] + gate * yn


def _out_projection(x, yc, ya, w_out, mod_l, g_post, *, tm):
    bsz, s, d = x.shape
    return pl.pallas_call(
        functools.partial(_outproj_kernel, d_model=d),
        name="out_projection",
        out_shape=jax.ShapeDtypeStruct((bsz, s, d), F32),
        grid=(bsz, s // tm),
        in_specs=[
            pl.BlockSpec((1, tm, d), lambda b, j: (b, j, 0)),
            pl.BlockSpec((1, tm, D_CONV), lambda b, j: (b, j, 0)),
            pl.BlockSpec((1, tm, D_ATTN), lambda b, j: (b, j, 0)),
            pl.BlockSpec(w_out.shape, lambda b, j: (0, 0)),
            pl.BlockSpec((1, 1, 3 * d), lambda b, j: (b, 0, 0)),
            pl.BlockSpec((1, d), lambda b, j: (0, 0)),
        ],
        out_specs=pl.BlockSpec((1, tm, d), lambda b, j: (b, j, 0)),
        compiler_params=pltpu.CompilerParams(
            dimension_semantics=("arbitrary", "arbitrary"), vmem_limit_bytes=VMEM_LIMIT),
    )(x, yc, ya, w_out, mod_l, g_post.reshape(1, d))


def _pack_w_in(w_in_l):
    d = w_in_l.shape[0]
    split = OFF_KW + D_IDX + N_IDX_HEADS
    return jnp.concatenate(
        [w_in_l[:, :split], jnp.zeros((d, KW_PAD), w_in_l.dtype), w_in_l[:, split:]],
        axis=-1).astype(BF16)


def kernel(x, c, w_ada, b_ada, g_pre, w_in, conv_w, conv_b, conv_ln_g, conv_ln_b, w_pw2, q_norm_g,
           w_uq, w_qidx, kv_norm_g, w_uv, rel_bias, w_out, g_post):
    depth = w_ada.shape[0]
    bsz, s, d = x.shape
    tm = min(512, s)
    t = min(ATTN_BLOCK, s)
    assert s % tm == 0 and s % t == 0 and tm % CONV_ROWS == 0 and d % LANES == 0

    mod = _modulation(c, w_ada, b_ada)
    btiles = _bias_tiles(rel_bias, t)
    for l in range(depth):
        mod_l = mod[l].reshape(bsz, 1, 3 * d)
        yc, q, qi, kv, kvt, ki, wt, sga = _in_projection(
            x, mod_l, g_pre[l], _pack_w_in(w_in[l]), conv_w[l], conv_b[l], conv_ln_g[l],
            conv_ln_b[l], w_pw2[l].astype(BF16), q_norm_g[l], w_uq[l].astype(BF16),
            w_qidx[l].astype(BF16), kv_norm_g[l], tm=tm)
        ya = _sparse_attention(q, qi, wt, kv, kvt, ki, sga, w_uv[l].astype(BF16), btiles, t=t)
        x = _out_projection(x, yc, ya, w_out[l].astype(BF16), mod_l, g_post[l], tm=tm)
    return x
```

```python
import functools
import math

import jax
import jax.numpy as jnp
from jax import lax
from jax.experimental import pallas as pl
from jax.experimental.pallas import tpu as pltpu

F32 = jnp.float32
BF16 = jnp.bfloat16
I32 = jnp.int32
I16 = jnp.int16

D_CONV = 512
CONV_WIDTH = 31
N_HEADS = 8
D_HEAD_OUT = 64
D_ATTN = N_HEADS * D_HEAD_OUT
Q_LORA = 256
KV_LORA = 128
N_IDX_HEADS = 8
D_IDX = 64
TOPK_MAX = 256
N_BUCKETS = 32
MAX_DISTANCE = 128
EPS = 1e-6

KW_PAD = 128 - D_IDX - N_IDX_HEADS
OFF_UVAL = 0
OFF_UGATE = OFF_UVAL + D_CONV
OFF_GCONV = OFF_UGATE + D_CONV
OFF_CQ = OFF_GCONV + D_CONV
OFF_CKV = OFF_CQ + Q_LORA
OFF_KW = OFF_CKV + KV_LORA
OFF_GATTN = OFF_KW + 128
D_IN_PACKED = OFF_GATTN + D_ATTN

HALO = 32
CONV_ROWS = 64
ATTN_BLOCK = 256
LANES = 128
SUBLANES = 8
PACKED_ROWS = 2 * SUBLANES
VMEM_LIMIT = 48 * 1024 * 1024

INT_MIN = -(2 ** 31)
NEG_KEY = -0x7F800000
LOG2E = math.log2(math.e)


def _dot(a, b):
    return jnp.dot(a, b, preferred_element_type=F32)


def _dot_nt(a, b):
    return lax.dot_general(a, b, (((1,), (1,)), ((), ())), preferred_element_type=F32)


def _dot_tn(a, b):
    return lax.dot_general(a, b, (((0,), (0,)), ((), ())), preferred_element_type=F32)


def _silu(v):
    return v * jax.nn.sigmoid(v)


def _mod_kernel(c_ref, w_ref, b_ref, o_ref):
    c = c_ref[...]
    o_ref[0] = _dot(_silu(c).astype(BF16), w_ref[0].astype(BF16)) + b_ref[0]


def _modulation(c, w_ada, b_ada):
    depth, d, d3 = w_ada.shape
    bsz = c.shape[0]
    tn = 1024
    return pl.pallas_call(
        _mod_kernel,
        name="adaln_mod",
        out_shape=jax.ShapeDtypeStruct((depth, bsz, d3), F32),
        grid=(depth, d3 // tn),
        in_specs=[
            pl.BlockSpec((bsz, d), lambda l, n: (0, 0)),
            pl.BlockSpec((1, d, tn), lambda l, n: (l, 0, n)),
            pl.BlockSpec((1, 1, tn), lambda l, n: (l, 0, n)),
        ],
        out_specs=pl.BlockSpec((1, bsz, tn), lambda l, n: (l, 0, n)),
        compiler_params=pltpu.CompilerParams(
            dimension_semantics=("arbitrary", "arbitrary"), vmem_limit_bytes=VMEM_LIMIT),
    )(c, w_ada, b_ada.reshape(depth, 1, d3))


def _t5_bucket(n):
    max_exact = N_BUCKETS // 2
    n = jnp.maximum(n, 0)
    nf = jnp.maximum(n, 1).astype(F32)
    large = max_exact + (jnp.log(nf / max_exact) / math.log(MAX_DISTANCE / max_exact)
                         * (N_BUCKETS - max_exact)).astype(I32)
    large = jnp.minimum(large, N_BUCKETS - 1)
    return jnp.where(n < max_exact, n, large)


def _bias_kernel(rb_ref, o_ref, *, t):
    sk = lax.broadcasted_iota(I32, (t, t), 0)
    tq = lax.broadcasted_iota(I32, (t, t), 1)
    for kind in range(2):
        bucket = _t5_bucket(tq - sk + kind * t)
        for h in range(N_HEADS):
            val = jnp.zeros((t, t), F32)
            for b in range(N_BUCKETS):
                val = jnp.where(bucket == b, rb_ref[b, h], val)
            o_ref[kind, h] = (val - rb_ref[N_BUCKETS - 1, h]) * LOG2E


def _bias_tiles(rel_bias, t):
    return pl.pallas_call(
        functools.partial(_bias_kernel, t=t),
        name="t5_bias_tiles",
        out_shape=jax.ShapeDtypeStruct((2, N_HEADS, t, t), F32),
        in_specs=[pl.BlockSpec(memory_space=pltpu.SMEM)],
        out_specs=pl.BlockSpec(memory_space=pltpu.VMEM),
    )(rel_bias)


def _inproj_kernel(x_ref, mod_ref, gpre_ref, w_ref, cw_ref, cb_ref, lng_ref, lnb_ref, wpw2_ref,
                   qg_ref, wuq_ref, wqi_ref, kvg_ref,
                   yc_ref, q_ref, qi_ref, kv_ref, kvt_ref, ki_ref, wt_ref, sga_ref,
                   hbuf, abuf, ybuf, *, tm, d_model):
    j = pl.program_id(1)

    x = x_ref[0]
    ms = jnp.mean(x * x, axis=-1, keepdims=True)
    xn = (x * lax.rsqrt(ms + EPS)) * gpre_ref[...]
    shift = mod_ref[0, :, 0:d_model]
    scale = mod_ref[0, :, d_model:2 * d_model]
    hbuf[...] = (xn * (1.0 + scale) + shift).astype(BF16)

    u_val = _dot(hbuf[...], w_ref[:, OFF_UVAL:OFF_UVAL + D_CONV])
    u_gate = _dot(hbuf[...], w_ref[:, OFF_UGATE:OFF_UGATE + D_CONV])

    @pl.when(j == 0)
    def _():
        abuf[0:HALO, :] = jnp.zeros((HALO, D_CONV), F32)

    abuf[HALO:HALO + tm, :] = u_val * jax.nn.sigmoid(u_gate)

    first = HALO - (CONV_WIDTH - 1)
    for r0 in range(0, tm, CONV_ROWS):
        acc = jnp.broadcast_to(cb_ref[...], (CONV_ROWS, D_CONV))
        for k in range(CONV_WIDTH):
            acc = acc + cw_ref[k:k + 1, :] * abuf[r0 + first + k:r0 + first + k + CONV_ROWS, :]
        mu = jnp.mean(acc, axis=-1, keepdims=True)
        cen = acc - mu
        var = jnp.mean(cen * cen, axis=-1, keepdims=True)
        yn = cen * lax.rsqrt(var + EPS) * lng_ref[...] + lnb_ref[...]
        ybuf[r0:r0 + CONV_ROWS, :] = _silu(yn).astype(BF16)

    abuf[0:HALO, :] = abuf[tm:tm + HALO, :]

    g_conv = _dot(hbuf[...], w_ref[:, OFF_GCONV:OFF_GCONV + D_CONV])
    yc_ref[0] = (_dot(ybuf[...], wpw2_ref[...]) * _silu(g_conv)).astype(BF16)

    c_q = _dot(hbuf[...], w_ref[:, OFF_CQ:OFF_CQ + Q_LORA])
    cq = (c_q * lax.rsqrt(jnp.mean(c_q * c_q, axis=-1, keepdims=True) + EPS)) * qg_ref[...]
    cq = cq.astype(BF16)
    q_ref[0] = (_dot(cq, wuq_ref[...]) * (KV_LORA ** -0.5 * LOG2E)).astype(BF16)
    qi_ref[0] = _dot(cq, wqi_ref[...]).astype(BF16)

    c_kv = _dot(hbuf[...], w_ref[:, OFF_CKV:OFF_CKV + KV_LORA])
    kvn = (c_kv * lax.rsqrt(jnp.mean(c_kv * c_kv, axis=-1, keepdims=True) + EPS)) * kvg_ref[...]
    kv_ref[0] = kvn.astype(BF16)
    kvt_ref[0] = jnp.transpose(kvn).astype(BF16)

    kw = _dot(hbuf[...], w_ref[:, OFF_KW:OFF_KW + 128])
    ki_ref[0] = kw[:, 0:D_IDX].astype(BF16)
    kwt = jnp.transpose(kw)
    wt_ref[0] = kwt[D_IDX:D_IDX + N_IDX_HEADS, :] * (N_IDX_HEADS ** -0.5 * D_IDX ** -0.5)

    g_attn = _dot(hbuf[...], w_ref[:, OFF_GATTN:OFF_GATTN + D_ATTN])
    sga_ref[0] = _silu(g_attn)


def _in_projection(x, mod_l, g_pre, w_packed, conv_w, conv_b, ln_g, ln_b, w_pw2, q_g, w_uq, w_qi,
                   kv_g, *, tm):
    bsz, s, d = x.shape
    row = lambda v: v.reshape(1, -1)
    full = lambda a: pl.BlockSpec(a.shape, lambda b, j: (0,) * a.ndim)
    args = (x, mod_l, row(g_pre), w_packed, conv_w, row(conv_b), row(ln_g), row(ln_b), w_pw2,
            row(q_g), w_uq, w_qi, row(kv_g))
    in_specs = [
        pl.BlockSpec((1, tm, d), lambda b, j: (b, j, 0)),
        pl.BlockSpec((1, 1, 3 * d), lambda b, j: (b, 0, 0)),
    ] + [full(a) for a in args[2:]]
    blk = lambda w: pl.BlockSpec((1, tm, w), lambda b, j: (b, j, 0))
    out_shape = (
        jax.ShapeDtypeStruct((bsz, s, D_CONV), BF16),
        jax.ShapeDtypeStruct((bsz, s, N_HEADS * KV_LORA), BF16),
        jax.ShapeDtypeStruct((bsz, s, N_IDX_HEADS * D_IDX), BF16),
        jax.ShapeDtypeStruct((bsz, s, KV_LORA), BF16),
        jax.ShapeDtypeStruct((bsz, KV_LORA, s), BF16),
        jax.ShapeDtypeStruct((bsz, s, D_IDX), BF16),
        jax.ShapeDtypeStruct((bsz, N_IDX_HEADS, s), F32),
        jax.ShapeDtypeStruct((bsz, s, D_ATTN), F32),
    )
    out_specs = (
        blk(D_CONV), blk(N_HEADS * KV_LORA), blk(N_IDX_HEADS * D_IDX), blk(KV_LORA),
        pl.BlockSpec((1, KV_LORA, tm), lambda b, j: (b, 0, j)),
        blk(D_IDX),
        pl.BlockSpec((1, N_IDX_HEADS, tm), lambda b, j: (b, 0, j)),
        blk(D_ATTN),
    )
    return pl.pallas_call(
        functools.partial(_inproj_kernel, tm=tm, d_model=d),
        name="in_projection",
        out_shape=out_shape,
        grid=(bsz, s // tm),
        in_specs=in_specs,
        out_specs=out_specs,
        scratch_shapes=[
            pltpu.VMEM((tm, d), BF16),
            pltpu.VMEM((tm + HALO, D_CONV), F32),
            pltpu.VMEM((tm, D_CONV), BF16),
        ],
        compiler_params=pltpu.CompilerParams(
            dimension_semantics=("arbitrary", "arbitrary"), vmem_limit_bytes=VMEM_LIMIT),
    )(*args)


def _sortable(v):
    b = pltpu.bitcast(v, I32)
    return jnp.where(b < 0, INT_MIN - b, b)


def _attn_kernel(q_ref, qi_ref, wt_ref, kv_ref, kvt_ref, ki_ref, sga_ref, wuv_ref, btile_ref,
                 o_ref, keys_ref, qis_ref, tau_ref, m_ref, l_ref, acc_ref, sa0_ref, sa1_ref,
                 s2a_ref, s2b_ref, hi_ref, lo_ref, *, t, k_top):
    sa_refs = (sa0_ref, sa1_ref)
    s2_refs = (s2a_ref, s2b_ref)
    j = pl.program_id(1)
    nchunk = j + 1
    npair = (j + 2) // 2
    groups = t // SUBLANES

    def chunk_start(c):
        return pl.multiple_of(c * t, t)

    for h in range(N_IDX_HEADS):
        qis_ref[h * t:(h + 1) * t, :] = qi_ref[0, :, h * D_IDX:(h + 1) * D_IDX]

    def by_parity(c, fn):
        for parity, (cur, nxt) in enumerate(((0, 1), (1, 0))):
            pl.when((c & 1) == parity)(functools.partial(fn, cur, nxt))

    def score_matmul(c, dst):
        kc = ki_ref[0, pl.ds(chunk_start(c), t), :]
        dst[...] = _dot_nt(kc, qis_ref[...])

    def score_finish(src):
        acc = jnp.zeros((t, t), F32)
        for h in range(N_IDX_HEADS):
            acc = acc + wt_ref[0, h:h + 1, :] * jnp.maximum(src[:, h * t:(h + 1) * t], 0.0)
        return _sortable(acc)

    def store_keys(c, key):
        keys_ref[pl.ds(chunk_start(c), t), :] = key
        hi_ref[pl.ds(chunk_start(c), t), :] = (key >> 16).astype(I16)
        lo_ref[pl.ds(chunk_start(c), t), :] = ((key & 0xFFFF) - 0x8000).astype(I16)

    def score_body(c, carry):
        def step(cur, nxt):
            score_matmul(c + 1, sa_refs[nxt])
            store_keys(c, score_finish(sa_refs[cur]))
        by_parity(c, step)
        return carry

    score_matmul(0, sa_refs[0])
    lax.fori_loop(0, j, score_body, 0)
    sk = lax.broadcasted_iota(I32, (t, t), 0)
    tq = lax.broadcasted_iota(I32, (t, t), 1)

    by_parity(j, lambda cur, nxt: store_keys(
        j, jnp.where(sk <= tq, score_finish(sa_refs[cur]), NEG_KEY)))
    store_keys(j + 1, jnp.full((t, t), NEG_KEY, I32))

    tau_ref[...] = jnp.full((SUBLANES, t), NEG_KEY + 1, I32)

    def count16(plane_ref, pred):
        def body(cc, cnt):
            start = pl.multiple_of(cc * (2 * t), 2 * t)
            v = plane_ref[pl.ds(start, 2 * t), :].reshape(2 * t // PACKED_ROWS, PACKED_ROWS, t)
            ones = jnp.where(pred(v), jnp.ones((), BF16), jnp.zeros((), BF16))
            parts = [ones[i] for i in range(2 * t // PACKED_ROWS)]
            while len(parts) > 1:
                parts = [parts[i] + parts[i + 1] for i in range(0, len(parts), 2)]
            return cnt + parts[0]
        cnt = lax.fori_loop(0, npair, body, jnp.zeros((PACKED_ROWS, t), BF16))
        return jnp.sum(cnt.astype(F32), axis=0, keepdims=True).astype(I32)

    def bisect16(plane_ref, want, n_all):
        def bit_body(i, carry):
            prefix, n_ge = carry
            cand = prefix | jnp.left_shift(jnp.int32(1), 15 - i)
            cand_b = jnp.broadcast_to((cand - 0x8000).astype(I16), (PACKED_ROWS, t))
            cnt = count16(plane_ref, lambda v: v >= cand_b[None])
            ok = cnt >= want
            return jnp.where(ok, cand, prefix), jnp.where(ok, cnt, n_ge)
        return lax.fori_loop(0, 16, bit_body, (jnp.zeros((1, t), I32), n_all))

    def count(pred):
        def body(cc, cnt):
            start = pl.multiple_of(cc * (2 * t), 2 * t)
            kk = keys_ref[pl.ds(start, 2 * t), :].reshape(2 * groups, SUBLANES, t)
            return cnt + jnp.sum(jnp.where(pred(kk, cc * (2 * t)), 1, 0), axis=0)
        cnt = lax.fori_loop(0, npair, body, jnp.zeros((SUBLANES, t), I32))
        return jnp.sum(cnt, axis=0, keepdims=True)

    @pl.when(nchunk * t > k_top)
    def _():
        hi_u, n_ge_hi = bisect16(hi_ref, k_top, jnp.full((1, t), nchunk * t, I32))
        hi_b = jnp.broadcast_to((hi_u - 0x8000).astype(I16), (PACKED_ROWS, t))
        n_gt_hi = count16(hi_ref, lambda v: v > hi_b[None])

        def mask_low(cc, carry):
            start = pl.multiple_of(cc * (2 * t), 2 * t)
            hv = hi_ref[pl.ds(start, 2 * t), :].reshape(2 * t // PACKED_ROWS, PACKED_ROWS, t)
            lv = lo_ref[pl.ds(start, 2 * t), :].reshape(2 * t // PACKED_ROWS, PACKED_ROWS, t)
            lo_ref[pl.ds(start, 2 * t), :] = jnp.where(
                hv == hi_b[None], lv, jnp.full((), -0x8000, I16)).reshape(2 * t, t)
            return carry

        lax.fori_loop(0, npair, mask_low, 0)
        lo_u, n_ge_lo = bisect16(lo_ref, k_top - n_gt_hi, n_ge_hi - n_gt_hi)
        tau = (hi_u - 0x8000) * 65536 + lo_u
        n_ge = n_gt_hi + n_ge_lo
        tau_ref[...] = jnp.broadcast_to(jnp.maximum(tau, NEG_KEY + 1), (SUBLANES, t))

        tie = jnp.logical_and(n_ge > k_top, tau > NEG_KEY)

        @pl.when(jnp.max(jnp.where(tie, 1, 0)) > 0)
        def _():
            tau_b = jnp.broadcast_to(tau, (SUBLANES, t))
            n_gt = count(lambda kk, base: kk > tau_b[None])
            want = k_top - n_gt
            row = lax.broadcasted_iota(I32, (2 * groups, SUBLANES, t), 0) * SUBLANES + \
                lax.broadcasted_iota(I32, (2 * groups, SUBLANES, t), 1)

            def pos_body(i, pos):
                cand = pos | jnp.left_shift(jnp.int32(1), 15 - i)
                cand_b = jnp.broadcast_to(cand, (SUBLANES, t))
                cnt = count(lambda kk, base: jnp.logical_and(kk == tau_b[None],
                                                            row + base < cand_b[None]))
                return jnp.where(cnt < want, cand, pos)

            pos = lax.fori_loop(0, 16, pos_body, jnp.zeros((1, t), I32))
            pos_b = jnp.broadcast_to(jnp.where(tie, pos, jnp.int32(2 ** 30)), (SUBLANES, t))

            def demote_body(cc, carry):
                start = pl.multiple_of(cc * (2 * t), 2 * t)
                kk = keys_ref[pl.ds(start, 2 * t), :].reshape(2 * groups, SUBLANES, t)
                drop = jnp.logical_and(kk == tau_b[None], row + cc * (2 * t) > pos_b[None])
                keys_ref[pl.ds(start, 2 * t), :] = jnp.where(drop, NEG_KEY, kk).reshape(2 * t, t)
                return carry

            lax.fori_loop(0, npair, demote_body, 0)

    m_ref[...] = jnp.full(m_ref.shape, -jnp.inf, F32)
    l_ref[...] = jnp.zeros(l_ref.shape, F32)
    acc_ref[...] = jnp.zeros(acc_ref.shape, F32)
    tau_sel = tau_ref[...]

    def logits_matmul(c, dst):
        kvc = kv_ref[0, pl.ds(chunk_start(c), t), :]
        for h in range(N_HEADS):
            dst[h] = _dot_nt(kvc, q_ref[0, :, h * KV_LORA:(h + 1) * KV_LORA])

    def attend(c, src, bias_tile):
        kvtc = kvt_ref[0, :, pl.ds(chunk_start(c), t)]
        kk = keys_ref[pl.ds(chunk_start(c), t), :].reshape(groups, SUBLANES, t)
        sel = (kk >= tau_sel[None]).reshape(t, t)
        for h in range(N_HEADS):
            s = src[h] if bias_tile is None else src[h] + btile_ref[bias_tile, h]
            s = jnp.where(sel, s, -jnp.inf)
            m_old = m_ref[h]
            m_new = jnp.maximum(m_old, jnp.max(s, axis=0, keepdims=True))
            m_safe = jnp.where(m_new == -jnp.inf, 0.0, m_new)
            alpha = jnp.exp2(m_old - m_safe)
            p = jnp.exp2(s - m_safe)
            m_ref[h] = m_new
            l_ref[h] = alpha * l_ref[h] + jnp.sum(p, axis=0, keepdims=True)
            acc_ref[h] = alpha * acc_ref[h] + _dot(kvtc, p.astype(BF16))

    def attend_loop(lo, hi, bias_tile):
        def body(c, carry):
            def step(cur, nxt):
                logits_matmul(c + 1, s2_refs[nxt])
                attend(c, s2_refs[cur], bias_tile)
            by_parity(c, step)
            return carry
        lax.fori_loop(lo, hi, body, 0)

    logits_matmul(0, s2_refs[0])
    attend_loop(0, jnp.maximum(j - 1, 0), None)
    attend_loop(jnp.maximum(j - 1, 0), j, 1)
    by_parity(j, lambda cur, nxt: attend(j, s2_refs[cur], 0))

    ys = []
    for h in range(N_HEADS):
        o_t = (acc_ref[h] * (1.0 / l_ref[h])).astype(BF16)
        ys.append(_dot_tn(o_t, wuv_ref[h]))
    y = jnp.concatenate(ys, axis=-1)
    o_ref[0] = (y * sga_ref[0]).astype(BF16)


def _sparse_attention(q, qi, wt, kv, kvt, ki, sga, w_uv, btiles, *, t):
    bsz, s, _ = q.shape
    k_top = min(TOPK_MAX, s // 4)
    assert (s + t) // PACKED_ROWS <= 256, "per-position key counts must stay exact in bf16"
    blk = lambda w: pl.BlockSpec((1, t, w), lambda b, j: (b, j, 0))
    in_specs = [
        blk(N_HEADS * KV_LORA),
        blk(N_IDX_HEADS * D_IDX),
        pl.BlockSpec((1, N_IDX_HEADS, t), lambda b, j: (b, 0, j)),
        pl.BlockSpec((1, s, KV_LORA), lambda b, j: (b, 0, 0)),
        pl.BlockSpec((1, KV_LORA, s), lambda b, j: (b, 0, 0)),
        pl.BlockSpec((1, s, D_IDX), lambda b, j: (b, 0, 0)),
        blk(D_ATTN),
        pl.BlockSpec(w_uv.shape, lambda b, j: (0, 0, 0)),
        pl.BlockSpec(btiles.shape, lambda b, j: (0, 0, 0, 0)),
    ]
    return pl.pallas_call(
        functools.partial(_attn_kernel, t=t, k_top=k_top),
        name="sparse_attention",
        out_shape=jax.ShapeDtypeStruct((bsz, s, D_ATTN), BF16),
        grid=(bsz, s // t),
        in_specs=in_specs,
        out_specs=blk(D_ATTN),
        scratch_shapes=[
            pltpu.VMEM((s + t, t), I32),
            pltpu.VMEM((N_IDX_HEADS * t, D_IDX), BF16),
            pltpu.VMEM((SUBLANES, t), I32),
            pltpu.VMEM((N_HEADS, 1, t), F32),
            pltpu.VMEM((N_HEADS, 1, t), F32),
            pltpu.VMEM((N_HEADS, KV_LORA, t), F32),
            pltpu.VMEM((t, N_IDX_HEADS * t), F32),
            pltpu.VMEM((t, N_IDX_HEADS * t), F32),
            pltpu.VMEM((N_HEADS, t, t), F32),
            pltpu.VMEM((N_HEADS, t, t), F32),
            pltpu.VMEM((s + t, t), I16),
            pltpu.VMEM((s + t, t), I16),
        ],
        compiler_params=pltpu.CompilerParams(
            dimension_semantics=("arbitrary", "arbitrary"), vmem_limit_bytes=VMEM_LIMIT),
    )(q, qi, wt, kv, kvt, ki, sga, w_uv, btiles)


def _outproj_kernel(x_ref, yc_ref, ya_ref, wo_ref, mod_ref, gpost_ref, o_ref, *, d_model):
    y = _dot(jnp.concatenate([yc_ref[0], ya_ref[0]], axis=-1), wo_ref[...])
    yn = (y * lax.rsqrt(jnp.mean(y * y, axis=-1, keepdims=True) + EPS)) * gpost_ref[...]
    gate = mod_ref[0, :, 2 * d_model:3 * d_model]
    o_ref[0] = x_ref[0] + gate * yn


def _out_projection(x, yc, ya, w_out, mod_l, g_post, *, tm):
    bsz, s, d = x.shape
    return pl.pallas_call(
        functools.partial(_outproj_kernel, d_model=d),
        name="out_projection",
        out_shape=jax.ShapeDtypeStruct((bsz, s, d), F32),
        grid=(bsz, s // tm),
        in_specs=[
            pl.BlockSpec((1, tm, d), lambda b, j: (b, j, 0)),
            pl.BlockSpec((1, tm, D_CONV), lambda b, j: (b, j, 0)),
            pl.BlockSpec((1, tm, D_ATTN), lambda b, j: (b, j, 0)),
            pl.BlockSpec(w_out.shape, lambda b, j: (0, 0)),
            pl.BlockSpec((1, 1, 3 * d), lambda b, j: (b, 0, 0)),
            pl.BlockSpec((1, d), lambda b, j: (0, 0)),
        ],
        out_specs=pl.BlockSpec((1, tm, d), lambda b, j: (b, j, 0)),
        compiler_params=pltpu.CompilerParams(
            dimension_semantics=("arbitrary", "arbitrary"), vmem_limit_bytes=VMEM_LIMIT),
    )(x, yc, ya, w_out, mod_l, g_post.reshape(1, d))


def _pack_w_in(w_in_l):
    d = w_in_l.shape[0]
    split = OFF_KW + D_IDX + N_IDX_HEADS
    return jnp.concatenate(
        [w_in_l[:, :split], jnp.zeros((d, KW_PAD), w_in_l.dtype), w_in_l[:, split:]],
        axis=-1).astype(BF16)


def kernel(x, c, w_ada, b_ada, g_pre, w_in, conv_w, conv_b, conv_ln_g, conv_ln_b, w_pw2, q_norm_g,
           w_uq, w_qidx, kv_norm_g, w_uv, rel_bias, w_out, g_post):
    depth = w_ada.shape[0]
    bsz, s, d = x.shape
    tm = min(512, s)
    t = min(ATTN_BLOCK, s)
    assert s % tm == 0 and s % t == 0 and tm % CONV_ROWS == 0 and d % LANES == 0

    mod = _modulation(c, w_ada, b_ada)
    btiles = _bias_tiles(rel_bias, t)
    for l in range(depth):
        mod_l = mod[l].reshape(bsz, 1, 3 * d)
        yc, q, qi, kv, kvt, ki, wt, sga = _in_projection(
            x, mod_l, g_pre[l], _pack_w_in(w_in[l]), conv_w[l], conv_b[l], conv_ln_g[l],
            conv_ln_b[l], w_pw2[l].astype(BF16), q_norm_g[l], w_uq[l].astype(BF16),
            w_qidx[l].astype(BF16), kv_norm_g[l], tm=tm)
        ya = _sparse_attention(q, qi, wt, kv, kvt, ki, sga, w_uv[l].astype(BF16), btiles, t=t)
        x = _out_projection(x, yc, ya, w_out[l].astype(BF16), mod_l, g_post[l], tm=tm)
    return x
```

```python
import functools
import math

import jax
import jax.numpy as jnp
from jax import lax
from jax.experimental import pallas as pl
from jax.experimental.pallas import tpu as pltpu

F32 = jnp.float32
BF16 = jnp.bfloat16
I32 = jnp.int32
I16 = jnp.int16

D_CONV = 512
CONV_WIDTH = 31
N_HEADS = 8
D_HEAD_OUT = 64
D_ATTN = N_HEADS * D_HEAD_OUT
Q_LORA = 256
KV_LORA = 128
N_IDX_HEADS = 8
D_IDX = 64
TOPK_MAX = 256
N_BUCKETS = 32
MAX_DISTANCE = 128
EPS = 1e-6

KW_PAD = 128 - D_IDX - N_IDX_HEADS
OFF_UVAL = 0
OFF_UGATE = OFF_UVAL + D_CONV
OFF_GCONV = OFF_UGATE + D_CONV
OFF_CQ = OFF_GCONV + D_CONV
OFF_CKV = OFF_CQ + Q_LORA
OFF_KW = OFF_CKV + KV_LORA
OFF_GATTN = OFF_KW + 128
D_IN_PACKED = OFF_GATTN + D_ATTN

HALO = 32
CONV_ROWS = 64
ATTN_BLOCK = 256
LANES = 128
SUBLANES = 8
PACKED_ROWS = 2 * SUBLANES
VMEM_LIMIT = 48 * 1024 * 1024

INT_MIN = -(2 ** 31)
NEG_KEY = -0x7F800000
LOG2E = math.log2(math.e)


def _dot(a, b):
    return jnp.dot(a, b, preferred_element_type=F32)


def _dot_nt(a, b):
    return lax.dot_general(a, b, (((1,), (1,)), ((), ())), preferred_element_type=F32)


def _dot_tn(a, b):
    return lax.dot_general(a, b, (((0,), (0,)), ((), ())), preferred_element_type=F32)


def _silu(v):
    return v * jax.nn.sigmoid(v)


def _mod_kernel(c_ref, w_ref, b_ref, o_ref):
    c = c_ref[...]
    o_ref[0] = _dot(_silu(c).astype(BF16), w_ref[0].astype(BF16)) + b_ref[0]


def _modulation(c, w_ada, b_ada):
    depth, d, d3 = w_ada.shape
    bsz = c.shape[0]
    tn = 1024
    return pl.pallas_call(
        _mod_kernel,
        name="adaln_mod",
        out_shape=jax.ShapeDtypeStruct((depth, bsz, d3), F32),
        grid=(depth, d3 // tn),
        in_specs=[
            pl.BlockSpec((bsz, d), lambda l, n: (0, 0)),
            pl.BlockSpec((1, d, tn), lambda l, n: (l, 0, n)),
            pl.BlockSpec((1, 1, tn), lambda l, n: (l, 0, n)),
        ],
        out_specs=pl.BlockSpec((1, bsz, tn), lambda l, n: (l, 0, n)),
        compiler_params=pltpu.CompilerParams(
            dimension_semantics=("arbitrary", "arbitrary"), vmem_limit_bytes=VMEM_LIMIT),
    )(c, w_ada, b_ada.reshape(depth, 1, d3))


def _t5_bucket(n):
    max_exact = N_BUCKETS // 2
    n = jnp.maximum(n, 0)
    nf = jnp.maximum(n, 1).astype(F32)
    large = max_exact + (jnp.log(nf / max_exact) / math.log(MAX_DISTANCE / max_exact)
                         * (N_BUCKETS - max_exact)).astype(I32)
    large = jnp.minimum(large, N_BUCKETS - 1)
    return jnp.where(n < max_exact, n, large)


def _bias_kernel(rb_ref, o_ref, *, t):
    sk = lax.broadcasted_iota(I32, (t, t), 0)
    tq = lax.broadcasted_iota(I32, (t, t), 1)
    for kind in range(2):
        bucket = _t5_bucket(tq - sk + kind * t)
        for h in range(N_HEADS):
            val = jnp.zeros((t, t), F32)
            for b in range(N_BUCKETS):
                val = jnp.where(bucket == b, rb_ref[b, h], val)
            o_ref[kind, h] = (val - rb_ref[N_BUCKETS - 1, h]) * LOG2E


def _bias_tiles(rel_bias, t):
    return pl.pallas_call(
        functools.partial(_bias_kernel, t=t),
        name="t5_bias_tiles",
        out_shape=jax.ShapeDtypeStruct((2, N_HEADS, t, t), F32),
        in_specs=[pl.BlockSpec(memory_space=pltpu.SMEM)],
        out_specs=pl.BlockSpec(memory_space=pltpu.VMEM),
    )(rel_bias)


def _inproj_kernel(x_ref, mod_ref, gpre_ref, w_ref, cw_ref, cb_ref, lng_ref, lnb_ref, wpw2_ref,
                   qg_ref, wuq_ref, wqi_ref, kvg_ref,
                   yc_ref, q_ref, qi_ref, kv_ref, kvt_ref, ki_ref, wt_ref, sga_ref,
                   hbuf, abuf, ybuf, shbuf, *, tm, d_model):
    j = pl.program_id(1)

    x = x_ref[0]
    ms = jnp.mean(x * x, axis=-1, keepdims=True)
    xn = (x * lax.rsqrt(ms + EPS)) * gpre_ref[...]
    shift = mod_ref[0, :, 0:d_model]
    scale = mod_ref[0, :, d_model:2 * d_model]
    hbuf[...] = (xn * (1.0 + scale) + shift).astype(BF16)

    u_val = _dot(hbuf[...], w_ref[:, OFF_UVAL:OFF_UVAL + D_CONV])
    u_gate = _dot(hbuf[...], w_ref[:, OFF_UGATE:OFF_UGATE + D_CONV])

    @pl.when(j == 0)
    def _():
        abuf[0:HALO, :] = jnp.zeros((HALO, D_CONV), F32)

    abuf[HALO:HALO + tm, :] = u_val * jax.nn.sigmoid(u_gate)

    span = tm + HALO - SUBLANES
    for r in range(1, SUBLANES):
        shbuf[r - 1, 0:span, :] = abuf[r:r + span, :]

    first = HALO - (CONV_WIDTH - 1)
    for r0 in range(0, tm, CONV_ROWS):
        acc = jnp.broadcast_to(cb_ref[...], (CONV_ROWS, D_CONV))
        for k in range(CONV_WIDTH):
            q8, r = divmod(first + k, SUBLANES)
            rows = slice(r0 + q8 * SUBLANES, r0 + q8 * SUBLANES + CONV_ROWS)
            tap = abuf[rows, :] if r == 0 else shbuf[r - 1, rows, :]
            acc = acc + cw_ref[k:k + 1, :] * tap
        mu = jnp.mean(acc, axis=-1, keepdims=True)
        cen = acc - mu
        var = jnp.mean(cen * cen, axis=-1, keepdims=True)
        yn = cen * lax.rsqrt(var + EPS) * lng_ref[...] + lnb_ref[...]
        ybuf[r0:r0 + CONV_ROWS, :] = _silu(yn).astype(BF16)

    abuf[0:HALO, :] = abuf[tm:tm + HALO, :]

    g_conv = _dot(hbuf[...], w_ref[:, OFF_GCONV:OFF_GCONV + D_CONV])
    yc_ref[0] = (_dot(ybuf[...], wpw2_ref[...]) * _silu(g_conv)).astype(BF16)

    c_q = _dot(hbuf[...], w_ref[:, OFF_CQ:OFF_CQ + Q_LORA])
    cq = (c_q * lax.rsqrt(jnp.mean(c_q * c_q, axis=-1, keepdims=True) + EPS)) * qg_ref[...]
    cq = cq.astype(BF16)
    q_ref[0] = (_dot(cq, wuq_ref[...]) * (KV_LORA ** -0.5 * LOG2E)).astype(BF16)
    qi_ref[0] = _dot(cq, wqi_ref[...]).astype(BF16)

    c_kv = _dot(hbuf[...], w_ref[:, OFF_CKV:OFF_CKV + KV_LORA])
    kvn = (c_kv * lax.rsqrt(jnp.mean(c_kv * c_kv, axis=-1, keepdims=True) + EPS)) * kvg_ref[...]
    kv_ref[0] = kvn.astype(BF16)
    kvt_ref[0] = jnp.transpose(kvn).astype(BF16)

    kw = _dot(hbuf[...], w_ref[:, OFF_KW:OFF_KW + 128])
    ki_ref[0] = kw[:, 0:D_IDX].astype(BF16)
    kwt = jnp.transpose(kw)
    wt_ref[0] = kwt[D_IDX:D_IDX + N_IDX_HEADS, :] * (N_IDX_HEADS ** -0.5 * D_IDX ** -0.5)

    g_attn = _dot(hbuf[...], w_ref[:, OFF_GATTN:OFF_GATTN + D_ATTN])
    sga_ref[0] = _silu(g_attn)


def _in_projection(x, mod_l, g_pre, w_packed, conv_w, conv_b, ln_g, ln_b, w_pw2, q_g, w_uq, w_qi,
                   kv_g, *, tm):
    bsz, s, d = x.shape
    row = lambda v: v.reshape(1, -1)
    full = lambda a: pl.BlockSpec(a.shape, lambda b, j: (0,) * a.ndim)
    args = (x, mod_l, row(g_pre), w_packed, conv_w, row(conv_b), row(ln_g), row(ln_b), w_pw2,
            row(q_g), w_uq, w_qi, row(kv_g))
    in_specs = [
        pl.BlockSpec((1, tm, d), lambda b, j: (b, j, 0)),
        pl.BlockSpec((1, 1, 3 * d), lambda b, j: (b, 0, 0)),
    ] + [full(a) for a in args[2:]]
    blk = lambda w: pl.BlockSpec((1, tm, w), lambda b, j: (b, j, 0))
    out_shape = (
        jax.ShapeDtypeStruct((bsz, s, D_CONV), BF16),
        jax.ShapeDtypeStruct((bsz, s, N_HEADS * KV_LORA), BF16),
        jax.ShapeDtypeStruct((bsz, s, N_IDX_HEADS * D_IDX), BF16),
        jax.ShapeDtypeStruct((bsz, s, KV_LORA), BF16),
        jax.ShapeDtypeStruct((bsz, KV_LORA, s), BF16),
        jax.ShapeDtypeStruct((bsz, s, D_IDX), BF16),
        jax.ShapeDtypeStruct((bsz, N_IDX_HEADS, s), F32),
        jax.ShapeDtypeStruct((bsz, s, D_ATTN), F32),
    )
    out_specs = (
        blk(D_CONV), blk(N_HEADS * KV_LORA), blk(N_IDX_HEADS * D_IDX), blk(KV_LORA),
        pl.BlockSpec((1, KV_LORA, tm), lambda b, j: (b, 0, j)),
        blk(D_IDX),
        pl.BlockSpec((1, N_IDX_HEADS, tm), lambda b, j: (b, 0, j)),
        blk(D_ATTN),
    )
    return pl.pallas_call(
        functools.partial(_inproj_kernel, tm=tm, d_model=d),
        name="in_projection",
        out_shape=out_shape,
        grid=(bsz, s // tm),
        in_specs=in_specs,
        out_specs=out_specs,
        scratch_shapes=[
            pltpu.VMEM((tm, d), BF16),
            pltpu.VMEM((tm + HALO, D_CONV), F32),
            pltpu.VMEM((tm, D_CONV), BF16),
            pltpu.VMEM((SUBLANES - 1, tm + HALO, D_CONV), F32),
        ],
        compiler_params=pltpu.CompilerParams(
            dimension_semantics=("arbitrary", "arbitrary"), vmem_limit_bytes=VMEM_LIMIT),
    )(*args)


def _sortable(v):
    b = pltpu.bitcast(v, I32)
    return jnp.where(b < 0, INT_MIN - b, b)


def _attn_kernel(q_ref, qi_ref, wt_ref, kv_ref, kvt_ref, ki_ref, sga_ref, wuv_ref, btile_ref,
                 o_ref, keys_ref, qis_ref, tau_ref, m_ref, l_ref, acc_ref, sa0_ref, sa1_ref,
                 s2a_ref, s2b_ref, hi_ref, lo_ref, *, t, k_top):
    sa_refs = (sa0_ref, sa1_ref)
    s2_refs = (s2a_ref, s2b_ref)
    j = pl.program_id(1)
    nchunk = j + 1
    npair = (j + 2) // 2
    groups = t // SUBLANES

    def chunk_start(c):
        return pl.multiple_of(c * t, t)

    for h in range(N_IDX_HEADS):
        qis_ref[h * t:(h + 1) * t, :] = qi_ref[0, :, h * D_IDX:(h + 1) * D_IDX]

    def by_parity(c, fn):
        for parity, (cur, nxt) in enumerate(((0, 1), (1, 0))):
            pl.when((c & 1) == parity)(functools.partial(fn, cur, nxt))

    def score_matmul(c, dst):
        kc = ki_ref[0, pl.ds(chunk_start(c), t), :]
        dst[...] = _dot_nt(kc, qis_ref[...])

    def score_finish(src):
        acc = jnp.zeros((t, t), F32)
        for h in range(N_IDX_HEADS):
            acc = acc + wt_ref[0, h:h + 1, :] * jnp.maximum(src[:, h * t:(h + 1) * t], 0.0)
        return _sortable(acc)

    def store_keys(c, key):
        keys_ref[pl.ds(chunk_start(c), t), :] = key
        hi_ref[pl.ds(chunk_start(c), t), :] = (key >> 16).astype(I16)
        lo_ref[pl.ds(chunk_start(c), t), :] = ((key & 0xFFFF) - 0x8000).astype(I16)

    def score_body(c, carry):
        def step(cur, nxt):
            score_matmul(c + 1, sa_refs[nxt])
            store_keys(c, score_finish(sa_refs[cur]))
        by_parity(c, step)
        return carry

    score_matmul(0, sa_refs[0])
    lax.fori_loop(0, j, score_body, 0)
    sk = lax.broadcasted_iota(I32, (t, t), 0)
    tq = lax.broadcasted_iota(I32, (t, t), 1)

    by_parity(j, lambda cur, nxt: store_keys(
        j, jnp.where(sk <= tq, score_finish(sa_refs[cur]), NEG_KEY)))
    store_keys(j + 1, jnp.full((t, t), NEG_KEY, I32))

    tau_ref[...] = jnp.full((SUBLANES, t), NEG_KEY + 1, I32)

    def count16(plane_ref, pred):
        def body(cc, cnt):
            start = pl.multiple_of(cc * (2 * t), 2 * t)
            v = plane_ref[pl.ds(start, 2 * t), :].reshape(2 * t // PACKED_ROWS, PACKED_ROWS, t)
            ones = jnp.where(pred(v), jnp.ones((), BF16), jnp.zeros((), BF16))
            parts = [ones[i] for i in range(2 * t // PACKED_ROWS)]
            while len(parts) > 1:
                parts = [parts[i] + parts[i + 1] for i in range(0, len(parts), 2)]
            return cnt + parts[0]
        cnt = lax.fori_loop(0, npair, body, jnp.zeros((PACKED_ROWS, t), BF16))
        return jnp.sum(cnt.astype(F32), axis=0, keepdims=True).astype(I32)

    def bisect16(plane_ref, want, n_all):
        def bit_body(i, carry):
            prefix, n_ge = carry
            cand = prefix | jnp.left_shift(jnp.int32(1), 15 - i)
            cand_b = jnp.broadcast_to((cand - 0x8000).astype(I16), (PACKED_ROWS, t))
            cnt = count16(plane_ref, lambda v: v >= cand_b[None])
            ok = cnt >= want
            return jnp.where(ok, cand, prefix), jnp.where(ok, cnt, n_ge)
        return lax.fori_loop(0, 16, bit_body, (jnp.zeros((1, t), I32), n_all))

    def count(pred):
        def body(cc, cnt):
            start = pl.multiple_of(cc * (2 * t), 2 * t)
            kk = keys_ref[pl.ds(start, 2 * t), :].reshape(2 * groups, SUBLANES, t)
            return cnt + jnp.sum(jnp.where(pred(kk, cc * (2 * t)), 1, 0), axis=0)
        cnt = lax.fori_loop(0, npair, body, jnp.zeros((SUBLANES, t), I32))
        return jnp.sum(cnt, axis=0, keepdims=True)

    @pl.when(nchunk * t > k_top)
    def _():
        hi_u, n_ge_hi = bisect16(hi_ref, k_top, jnp.full((1, t), nchunk * t, I32))
        hi_b = jnp.broadcast_to((hi_u - 0x8000).astype(I16), (PACKED_ROWS, t))
        n_gt_hi = count16(hi_ref, lambda v: v > hi_b[None])

        def mask_low(cc, carry):
            start = pl.multiple_of(cc * (2 * t), 2 * t)
            hv = hi_ref[pl.ds(start, 2 * t), :].reshape(2 * t // PACKED_ROWS, PACKED_ROWS, t)
            lv = lo_ref[pl.ds(start, 2 * t), :].reshape(2 * t // PACKED_ROWS, PACKED_ROWS, t)
            lo_ref[pl.ds(start, 2 * t), :] = jnp.where(
                hv == hi_b[None], lv, jnp.full((), -0x8000, I16)).reshape(2 * t, t)
            return carry

        lax.fori_loop(0, npair, mask_low, 0)
        lo_u, n_ge_lo = bisect16(lo_ref, k_top - n_gt_hi, n_ge_hi - n_gt_hi)
        tau = (hi_u - 0x8000) * 65536 + lo_u
        n_ge = n_gt_hi + n_ge_lo
        tau_ref[...] = jnp.broadcast_to(jnp.maximum(tau, NEG_KEY + 1), (SUBLANES, t))

        tie = jnp.logical_and(n_ge > k_top, tau > NEG_KEY)

        @pl.when(jnp.max(jnp.where(tie, 1, 0)) > 0)
        def _():
            tau_b = jnp.broadcast_to(tau, (SUBLANES, t))
            n_gt = count(lambda kk, base: kk > tau_b[None])
            want = k_top - n_gt
            row = lax.broadcasted_iota(I32, (2 * groups, SUBLANES, t), 0) * SUBLANES + \
                lax.broadcasted_iota(I32, (2 * groups, SUBLANES, t), 1)

            def pos_body(i, pos):
                cand = pos | jnp.left_shift(jnp.int32(1), 15 - i)
                cand_b = jnp.broadcast_to(cand, (SUBLANES, t))
                cnt = count(lambda kk, base: jnp.logical_and(kk == tau_b[None],
                                                            row + base < cand_b[None]))
                return jnp.where(cnt < want, cand, pos)

            pos = lax.fori_loop(0, 16, pos_body, jnp.zeros((1, t), I32))
            pos_b = jnp.broadcast_to(jnp.where(tie, pos, jnp.int32(2 ** 30)), (SUBLANES, t))

            def demote_body(cc, carry):
                start = pl.multiple_of(cc * (2 * t), 2 * t)
                kk = keys_ref[pl.ds(start, 2 * t), :].reshape(2 * groups, SUBLANES, t)
                drop = jnp.logical_and(kk == tau_b[None], row + cc * (2 * t) > pos_b[None])
                keys_ref[pl.ds(start, 2 * t), :] = jnp.where(drop, NEG_KEY, kk).reshape(2 * t, t)
                return carry

            lax.fori_loop(0, npair, demote_body, 0)

    m_ref[...] = jnp.full(m_ref.shape, -jnp.inf, F32)
    l_ref[...] = jnp.zeros(l_ref.shape, F32)
    acc_ref[...] = jnp.zeros(acc_ref.shape, F32)
    tau_sel = tau_ref[...]

    def logits_matmul(c, dst):
        kvc = kv_ref[0, pl.ds(chunk_start(c), t), :]
        for h in range(N_HEADS):
            dst[h] = _dot_nt(kvc, q_ref[0, :, h * KV_LORA:(h + 1) * KV_LORA])

    def attend(c, src, bias_tile):
        kvtc = kvt_ref[0, :, pl.ds(chunk_start(c), t)]
        kk = keys_ref[pl.ds(chunk_start(c), t), :].reshape(groups, SUBLANES, t)
        sel = (kk >= tau_sel[None]).reshape(t, t)
        for h in range(N_HEADS):
            s = src[h] if bias_tile is None else src[h] + btile_ref[bias_tile, h]
            s = jnp.where(sel, s, -jnp.inf)
            m_old = m_ref[h]
            m_new = jnp.maximum(m_old, jnp.max(s, axis=0, keepdims=True))
            m_safe = jnp.where(m_new == -jnp.inf, 0.0, m_new)
            alpha = jnp.exp2(m_old - m_safe)
            p = jnp.exp2(s - m_safe)
            m_ref[h] = m_new
            l_ref[h] = alpha * l_ref[h] + jnp.sum(p, axis=0, keepdims=True)
            acc_ref[h] = alpha * acc_ref[h] + _dot(kvtc, p.astype(BF16))

    def attend_loop(lo, hi, bias_tile):
        def body(c, carry):
            def step(cur, nxt):
                logits_matmul(c + 1, s2_refs[nxt])
                attend(c, s2_refs[cur], bias_tile)
            by_parity(c, step)
            return carry
        lax.fori_loop(lo, hi, body, 0)

    logits_matmul(0, s2_refs[0])
    attend_loop(0, jnp.maximum(j - 1, 0), None)
    attend_loop(jnp.maximum(j - 1, 0), j, 1)
    by_parity(j, lambda cur, nxt: attend(j, s2_refs[cur], 0))

    ys = []
    for h in range(N_HEADS):
        o_t = (acc_ref[h] * (1.0 / l_ref[h])).astype(BF16)
        ys.append(_dot_tn(o_t, wuv_ref[h]))
    y = jnp.concatenate(ys, axis=-1)
    o_ref[0] = (y * sga_ref[0]).astype(BF16)


def _sparse_attention(q, qi, wt, kv, kvt, ki, sga, w_uv, btiles, *, t):
    bsz, s, _ = q.shape
    k_top = min(TOPK_MAX, s // 4)
    assert (s + t) // PACKED_ROWS <= 256, "per-position key counts must stay exact in bf16"
    blk = lambda w: pl.BlockSpec((1, t, w), lambda b, j: (b, j, 0))
    in_specs = [
        blk(N_HEADS * KV_LORA),
        blk(N_IDX_HEADS * D_IDX),
        pl.BlockSpec((1, N_IDX_HEADS, t), lambda b, j: (b, 0, j)),
        pl.BlockSpec((1, s, KV_LORA), lambda b, j: (b, 0, 0)),
        pl.BlockSpec((1, KV_LORA, s), lambda b, j: (b, 0, 0)),
        pl.BlockSpec((1, s, D_IDX), lambda b, j: (b, 0, 0)),
        blk(D_ATTN),
        pl.BlockSpec(w_uv.shape, lambda b, j: (0, 0, 0)),
        pl.BlockSpec(btiles.shape, lambda b, j: (0, 0, 0, 0)),
    ]
    return pl.pallas_call(
        functools.partial(_attn_kernel, t=t, k_top=k_top),
        name="sparse_attention",
        out_shape=jax.ShapeDtypeStruct((bsz, s, D_ATTN), BF16),
        grid=(bsz, s // t),
        in_specs=in_specs,
        out_specs=blk(D_ATTN),
        scratch_shapes=[
            pltpu.VMEM((s + t, t), I32),
            pltpu.VMEM((N_IDX_HEADS * t, D_IDX), BF16),
            pltpu.VMEM((SUBLANES, t), I32),
            pltpu.VMEM((N_HEADS, 1, t), F32),
            pltpu.VMEM((N_HEADS, 1, t), F32),
            pltpu.VMEM((N_HEADS, KV_LORA, t), F32),
            pltpu.VMEM((t, N_IDX_HEADS * t), F32),
            pltpu.VMEM((t, N_IDX_HEADS * t), F32),
            pltpu.VMEM((N_HEADS, t, t), F32),
            pltpu.VMEM((N_HEADS, t, t), F32),
            pltpu.VMEM((s + t, t), I16),
            pltpu.VMEM((s + t, t), I16),
        ],
        compiler_params=pltpu.CompilerParams(
            dimension_semantics=("arbitrary", "arbitrary"), vmem_limit_bytes=VMEM_LIMIT),
    )(q, qi, wt, kv, kvt, ki, sga, w_uv, btiles)


def _outproj_kernel(x_ref, yc_ref, ya_ref, wo_ref, mod_ref, gpost_ref, o_ref, *, d_model):
    y = _dot(jnp.concatenate([yc_ref[0], ya_ref[0]], axis=-1), wo_ref[...])
    yn = (y * lax.rsqrt(jnp.mean(y * y, axis=-1, keepdims=True) + EPS)) * gpost_ref[...]
    gate = mod_ref[0, :, 2 * d_model:3 * d_model]
    o_ref[0] = x_ref[0] + gate * yn


def _out_projection(x, yc, ya, w_out, mod_l, g_post, *, tm):
    bsz, s, d = x.shape
    return pl.pallas_call(
        functools.partial(_outproj_kernel, d_model=d),
        name="out_projection",
        out_shape=jax.ShapeDtypeStruct((bsz, s, d), F32),
        grid=(bsz, s // tm),
        in_specs=[
            pl.BlockSpec((1, tm, d), lambda b, j: (b, j, 0)),
            pl.BlockSpec((1, tm, D_CONV), lambda b, j: (b, j, 0)),
            pl.BlockSpec((1, tm, D_ATTN), lambda b, j: (b, j, 0)),
            pl.BlockSpec(w_out.shape, lambda b, j: (0, 0)),
            pl.BlockSpec((1, 1, 3 * d), lambda b, j: (b, 0, 0)),
            pl.BlockSpec((1, d), lambda b, j: (0, 0)),
        ],
        out_specs=pl.BlockSpec((1, tm, d), lambda b, j: (b, j, 0)),
        compiler_params=pltpu.CompilerParams(
            dimension_semantics=("arbitrary", "arbitrary"), vmem_limit_bytes=VMEM_LIMIT),
    )(x, yc, ya, w_out, mod_l, g_post.reshape(1, d))


def _pack_w_in(w_in_l):
    d = w_in_l.shape[0]
    split = OFF_KW + D_IDX + N_IDX_HEADS
    return jnp.concatenate(
        [w_in_l[:, :split], jnp.zeros((d, KW_PAD), w_in_l.dtype), w_in_l[:, split:]],
        axis=-1).astype(BF16)


def kernel(x, c, w_ada, b_ada, g_pre, w_in, conv_w, conv_b, conv_ln_g, conv_ln_b, w_pw2, q_norm_g,
           w_uq, w_qidx, kv_norm_g, w_uv, rel_bias, w_out, g_post):
    depth = w_ada.shape[0]
    bsz, s, d = x.shape
    tm = min(512, s)
    t = min(ATTN_BLOCK, s)
    assert s % tm == 0 and s % t == 0 and tm % CONV_ROWS == 0 and d % LANES == 0

    mod = _modulation(c, w_ada, b_ada)
    btiles = _bias_tiles(rel_bias, t)
    for l in range(depth):
        mod_l = mod[l].reshape(bsz, 1, 3 * d)
        yc, q, qi, kv, kvt, ki, wt, sga = _in_projection(
            x, mod_l, g_pre[l], _pack_w_in(w_in[l]), conv_w[l], conv_b[l], conv_ln_g[l],
            conv_ln_b[l], w_pw2[l].astype(BF16), q_norm_g[l], w_uq[l].astype(BF16),
            w_qidx[l].astype(BF16), kv_norm_g[l], tm=tm)
        ya = _sparse_attention(q, qi, wt, kv, kvt, ki, sga, w_uv[l].astype(BF16), btiles, t=t)
        x = _out_projection(x, yc, ya, w_out[l].astype(BF16), mod_l, g_post[l], tm=tm)
    return x
```

```python
import functools
import math

import jax
import jax.numpy as jnp
from jax import lax
from jax.experimental import pallas as pl
from jax.experimental.pallas import tpu as pltpu

F32 = jnp.float32
BF16 = jnp.bfloat16
I32 = jnp.int32
I16 = jnp.int16

D_CONV = 512
CONV_WIDTH = 31
N_HEADS = 8
D_HEAD_OUT = 64
D_ATTN = N_HEADS * D_HEAD_OUT
Q_LORA = 256
KV_LORA = 128
N_IDX_HEADS = 8
D_IDX = 64
TOPK_MAX = 256
N_BUCKETS = 32
MAX_DISTANCE = 128
EPS = 1e-6

KW_PAD = 128 - D_IDX - N_IDX_HEADS
OFF_UVAL = 0
OFF_UGATE = OFF_UVAL + D_CONV
OFF_GCONV = OFF_UGATE + D_CONV
OFF_CQ = OFF_GCONV + D_CONV
OFF_CKV = OFF_CQ + Q_LORA
OFF_KW = OFF_CKV + KV_LORA
OFF_GATTN = OFF_KW + 128
D_IN_PACKED = OFF_GATTN + D_ATTN

HALO = 32
CONV_ROWS = 64
ATTN_BLOCK = 256
LANES = 128
SUBLANES = 8
PACKED_ROWS = 2 * SUBLANES
VMEM_LIMIT = 48 * 1024 * 1024

INT_MIN = -(2 ** 31)
NEG_KEY = -0x7F800000
LOG2E = math.log2(math.e)


def _dot(a, b):
    return jnp.dot(a, b, preferred_element_type=F32)


def _dot_nt(a, b):
    return lax.dot_general(a, b, (((1,), (1,)), ((), ())), preferred_element_type=F32)


def _dot_tn(a, b):
    return lax.dot_general(a, b, (((0,), (0,)), ((), ())), preferred_element_type=F32)


def _silu(v):
    return v * jax.nn.sigmoid(v)


def _mod_kernel(c_ref, w_ref, b_ref, o_ref):
    c = c_ref[...]
    o_ref[0] = _dot(_silu(c).astype(BF16), w_ref[0].astype(BF16)) + b_ref[0]


def _modulation(c, w_ada, b_ada):
    depth, d, d3 = w_ada.shape
    bsz = c.shape[0]
    tn = 1024
    return pl.pallas_call(
        _mod_kernel,
        name="adaln_mod",
        out_shape=jax.ShapeDtypeStruct((depth, bsz, d3), F32),
        grid=(depth, d3 // tn),
        in_specs=[
            pl.BlockSpec((bsz, d), lambda l, n: (0, 0)),
            pl.BlockSpec((1, d, tn), lambda l, n: (l, 0, n)),
            pl.BlockSpec((1, 1, tn), lambda l, n: (l, 0, n)),
        ],
        out_specs=pl.BlockSpec((1, bsz, tn), lambda l, n: (l, 0, n)),
        compiler_params=pltpu.CompilerParams(
            dimension_semantics=("arbitrary", "arbitrary"), vmem_limit_bytes=VMEM_LIMIT),
    )(c, w_ada, b_ada.reshape(depth, 1, d3))


def _t5_bucket(n):
    max_exact = N_BUCKETS // 2
    n = jnp.maximum(n, 0)
    nf = jnp.maximum(n, 1).astype(F32)
    large = max_exact + (jnp.log(nf / max_exact) / math.log(MAX_DISTANCE / max_exact)
                         * (N_BUCKETS - max_exact)).astype(I32)
    large = jnp.minimum(large, N_BUCKETS - 1)
    return jnp.where(n < max_exact, n, large)


def _bias_kernel(rb_ref, o_ref, *, t):
    sk = lax.broadcasted_iota(I32, (t, t), 0)
    tq = lax.broadcasted_iota(I32, (t, t), 1)
    for kind in range(2):
        bucket = _t5_bucket(tq - sk + kind * t)
        for h in range(N_HEADS):
            val = jnp.zeros((t, t), F32)
            for b in range(N_BUCKETS):
                val = jnp.where(bucket == b, rb_ref[b, h], val)
            o_ref[kind, h] = (val - rb_ref[N_BUCKETS - 1, h]) * LOG2E


def _bias_tiles(rel_bias, t):
    return pl.pallas_call(
        functools.partial(_bias_kernel, t=t),
        name="t5_bias_tiles",
        out_shape=jax.ShapeDtypeStruct((2, N_HEADS, t, t), F32),
        in_specs=[pl.BlockSpec(memory_space=pltpu.SMEM)],
        out_specs=pl.BlockSpec(memory_space=pltpu.VMEM),
    )(rel_bias)


def _inproj_kernel(x_ref, mod_ref, gpre_ref, w_ref, cw_ref, cb_ref, lng_ref, lnb_ref, wpw2_ref,
                   qg_ref, wuq_ref, wqi_ref, kvg_ref,
                   yc_ref, q_ref, qi_ref, kv_ref, kvt_ref, ki_ref, wt_ref, sga_ref,
                   hbuf, abuf, ybuf, shbuf, gcbuf, *, tm, d_model):
    j = pl.program_id(1)

    x = x_ref[0]
    ms = jnp.mean(x * x, axis=-1, keepdims=True)
    xn = (x * lax.rsqrt(ms + EPS)) * gpre_ref[...]
    shift = mod_ref[0, :, 0:d_model]
    scale = mod_ref[0, :, d_model:2 * d_model]
    hbuf[...] = (xn * (1.0 + scale) + shift).astype(BF16)

    u_val = _dot(hbuf[...], w_ref[:, OFF_UVAL:OFF_UVAL + D_CONV])
    u_gate = _dot(hbuf[...], w_ref[:, OFF_UGATE:OFF_UGATE + D_CONV])

    @pl.when(j == 0)
    def _():
        abuf[0:HALO, :] = jnp.zeros((HALO, D_CONV), F32)

    abuf[HALO:HALO + tm, :] = u_val * jax.nn.sigmoid(u_gate)

    span = tm + HALO - SUBLANES
    for r in range(1, SUBLANES):
        shbuf[r - 1, 0:span, :] = abuf[r:r + span, :]

    held = {}

    def proj_gconv():
        gcbuf[...] = _silu(_dot(hbuf[...], w_ref[:, OFF_GCONV:OFF_GCONV + D_CONV]))

    def proj_cq():
        c_q = _dot(hbuf[...], w_ref[:, OFF_CQ:OFF_CQ + Q_LORA])
        cq = (c_q * lax.rsqrt(jnp.mean(c_q * c_q, axis=-1, keepdims=True) + EPS)) * qg_ref[...]
        held["cq"] = cq.astype(BF16)

    def proj_q():
        q_ref[0] = (_dot(held["cq"], wuq_ref[...]) * (KV_LORA ** -0.5 * LOG2E)).astype(BF16)

    def proj_qi():
        qi_ref[0] = _dot(held["cq"], wqi_ref[...]).astype(BF16)

    def proj_kv():
        c_kv = _dot(hbuf[...], w_ref[:, OFF_CKV:OFF_CKV + KV_LORA])
        kvn = (c_kv * lax.rsqrt(jnp.mean(c_kv * c_kv, axis=-1, keepdims=True) + EPS)) * kvg_ref[...]
        kv_ref[0] = kvn.astype(BF16)
        kvt_ref[0] = jnp.transpose(kvn).astype(BF16)

    def proj_kw():
        kw = _dot(hbuf[...], w_ref[:, OFF_KW:OFF_KW + 128])
        ki_ref[0] = kw[:, 0:D_IDX].astype(BF16)
        kwt = jnp.transpose(kw)
        wt_ref[0] = kwt[D_IDX:D_IDX + N_IDX_HEADS, :] * (N_IDX_HEADS ** -0.5 * D_IDX ** -0.5)

    def proj_gattn():
        sga_ref[0] = _silu(_dot(hbuf[...], w_ref[:, OFF_GATTN:OFF_GATTN + D_ATTN]))

    steps = [proj_gconv, proj_cq, proj_q, proj_qi, proj_kv, proj_kw, proj_gattn]

    first = HALO - (CONV_WIDTH - 1)
    for i, r0 in enumerate(range(0, tm, CONV_ROWS)):
        if i < len(steps):
            steps[i]()
        acc = jnp.broadcast_to(cb_ref[...], (CONV_ROWS, D_CONV))
        for k in range(CONV_WIDTH):
            q8, r = divmod(first + k, SUBLANES)
            rows = slice(r0 + q8 * SUBLANES, r0 + q8 * SUBLANES + CONV_ROWS)
            tap = abuf[rows, :] if r == 0 else shbuf[r - 1, rows, :]
            acc = acc + cw_ref[k:k + 1, :] * tap
        mu = jnp.mean(acc, axis=-1, keepdims=True)
        cen = acc - mu
        var = jnp.mean(cen * cen, axis=-1, keepdims=True)
        yn = cen * lax.rsqrt(var + EPS) * lng_ref[...] + lnb_ref[...]
        ybuf[r0:r0 + CONV_ROWS, :] = _silu(yn).astype(BF16)

    for step in steps[tm // CONV_ROWS:]:
        step()

    abuf[0:HALO, :] = abuf[tm:tm + HALO, :]
    yc_ref[0] = (_dot(ybuf[...], wpw2_ref[...]) * gcbuf[...]).astype(BF16)


def _in_projection(x, mod_l, g_pre, w_packed, conv_w, conv_b, ln_g, ln_b, w_pw2, q_g, w_uq, w_qi,
                   kv_g, *, tm):
    bsz, s, d = x.shape
    row = lambda v: v.reshape(1, -1)
    full = lambda a: pl.BlockSpec(a.shape, lambda b, j: (0,) * a.ndim)
    args = (x, mod_l, row(g_pre), w_packed, conv_w, row(conv_b), row(ln_g), row(ln_b), w_pw2,
            row(q_g), w_uq, w_qi, row(kv_g))
    in_specs = [
        pl.BlockSpec((1, tm, d), lambda b, j: (b, j, 0)),
        pl.BlockSpec((1, 1, 3 * d), lambda b, j: (b, 0, 0)),
    ] + [full(a) for a in args[2:]]
    blk = lambda w: pl.BlockSpec((1, tm, w), lambda b, j: (b, j, 0))
    out_shape = (
        jax.ShapeDtypeStruct((bsz, s, D_CONV), BF16),
        jax.ShapeDtypeStruct((bsz, s, N_HEADS * KV_LORA), BF16),
        jax.ShapeDtypeStruct((bsz, s, N_IDX_HEADS * D_IDX), BF16),
        jax.ShapeDtypeStruct((bsz, s, KV_LORA), BF16),
        jax.ShapeDtypeStruct((bsz, KV_LORA, s), BF16),
        jax.ShapeDtypeStruct((bsz, s, D_IDX), BF16),
        jax.ShapeDtypeStruct((bsz, N_IDX_HEADS, s), F32),
        jax.ShapeDtypeStruct((bsz, s, D_ATTN), F32),
    )
    out_specs = (
        blk(D_CONV), blk(N_HEADS * KV_LORA), blk(N_IDX_HEADS * D_IDX), blk(KV_LORA),
        pl.BlockSpec((1, KV_LORA, tm), lambda b, j: (b, 0, j)),
        blk(D_IDX),
        pl.BlockSpec((1, N_IDX_HEADS, tm), lambda b, j: (b, 0, j)),
        blk(D_ATTN),
    )
    return pl.pallas_call(
        functools.partial(_inproj_kernel, tm=tm, d_model=d),
        name="in_projection",
        out_shape=out_shape,
        grid=(bsz, s // tm),
        in_specs=in_specs,
        out_specs=out_specs,
        scratch_shapes=[
            pltpu.VMEM((tm, d), BF16),
            pltpu.VMEM((tm + HALO, D_CONV), F32),
            pltpu.VMEM((tm, D_CONV), BF16),
            pltpu.VMEM((SUBLANES - 1, tm + HALO, D_CONV), F32),
            pltpu.VMEM((tm, D_CONV), F32),
        ],
        compiler_params=pltpu.CompilerParams(
            dimension_semantics=("arbitrary", "arbitrary"), vmem_limit_bytes=VMEM_LIMIT),
    )(*args)


def _sortable(v):
    b = pltpu.bitcast(v, I32)
    return jnp.where(b < 0, INT_MIN - b, b)


def _attn_kernel(q_ref, qi_ref, wt_ref, kv_ref, kvt_ref, ki_ref, sga_ref, wuv_ref, btile_ref,
                 o_ref, keys_ref, qis_ref, tau_ref, m_ref, mask_ref, acc_ref, sa0_ref, sa1_ref,
                 s2a_ref, s2b_ref, hi_ref, lo_ref, *, t, k_top):
    sa_refs = (sa0_ref, sa1_ref)
    s2_refs = (s2a_ref, s2b_ref)
    j = pl.program_id(1)
    nchunk = j + 1
    npair = (j + 2) // 2
    groups = t // SUBLANES

    def chunk_start(c):
        return pl.multiple_of(c * t, t)

    for h in range(N_IDX_HEADS):
        qis_ref[h * t:(h + 1) * t, :] = qi_ref[0, :, h * D_IDX:(h + 1) * D_IDX]

    def by_parity(c, fn):
        for parity, (cur, nxt) in enumerate(((0, 1), (1, 0))):
            pl.when((c & 1) == parity)(functools.partial(fn, cur, nxt))

    def chunk_loop(n, step):
        def pair_body(cc, carry):
            step(2 * cc, 0, 1)
            step(2 * cc + 1, 1, 0)
            return carry
        lax.fori_loop(0, n // 2, pair_body, 0)
        pl.when(n % 2 == 1)(lambda: step(n - 1, 0, 1))

    def score_matmul(c, dst):
        kc = ki_ref[0, pl.ds(chunk_start(c), t), :]
        dst[...] = _dot_nt(kc, qis_ref[...])

    def score_finish(src):
        acc = jnp.zeros((t, t), F32)
        for h in range(N_IDX_HEADS):
            acc = acc + wt_ref[0, h:h + 1, :] * jnp.maximum(src[:, h * t:(h + 1) * t], 0.0)
        return _sortable(acc)

    def store_keys(c, key):
        keys_ref[pl.ds(chunk_start(c), t), :] = key
        hi_ref[pl.ds(chunk_start(c), t), :] = (key >> 16).astype(I16)
        lo_ref[pl.ds(chunk_start(c), t), :] = ((key & 0xFFFF) - 0x8000).astype(I16)

    def score_step(c, cur, nxt):
        score_matmul(c + 1, sa_refs[nxt])
        store_keys(c, score_finish(sa_refs[cur]))

    score_matmul(0, sa_refs[0])
    chunk_loop(j, score_step)
    sk = lax.broadcasted_iota(I32, (t, t), 0)
    tq = lax.broadcasted_iota(I32, (t, t), 1)

    by_parity(j, lambda cur, nxt: store_keys(
        j, jnp.where(sk <= tq, score_finish(sa_refs[cur]), NEG_KEY)))
    store_keys(j + 1, jnp.full((t, t), NEG_KEY, I32))

    tau_ref[...] = jnp.full((SUBLANES, t), NEG_KEY + 1, I32)

    def count16(plane_ref, pred):
        def body(cc, cnt):
            start = pl.multiple_of(cc * (2 * t), 2 * t)
            v = plane_ref[pl.ds(start, 2 * t), :].reshape(2 * t // PACKED_ROWS, PACKED_ROWS, t)
            ones = jnp.where(pred(v), jnp.ones((), BF16), jnp.zeros((), BF16))
            parts = [ones[i] for i in range(2 * t // PACKED_ROWS)]
            while len(parts) > 1:
                parts = [parts[i] + parts[i + 1] for i in range(0, len(parts), 2)]
            return cnt + parts[0]
        cnt = lax.fori_loop(0, npair, body, jnp.zeros((PACKED_ROWS, t), BF16))
        return jnp.sum(cnt.astype(F32), axis=0, keepdims=True).astype(I32)

    def bisect16(plane_ref, want, n_all):
        def bit_body(i, carry):
            prefix, n_ge = carry
            cand = prefix | jnp.left_shift(jnp.int32(1), 15 - i)
            cand_b = jnp.broadcast_to((cand - 0x8000).astype(I16), (PACKED_ROWS, t))
            cnt = count16(plane_ref, lambda v: v >= cand_b[None])
            ok = cnt >= want
            return jnp.where(ok, cand, prefix), jnp.where(ok, cnt, n_ge)
        return lax.fori_loop(0, 16, bit_body, (jnp.zeros((1, t), I32), n_all))

    def count(pred):
        def body(cc, cnt):
            start = pl.multiple_of(cc * (2 * t), 2 * t)
            kk = keys_ref[pl.ds(start, 2 * t), :].reshape(2 * groups, SUBLANES, t)
            return cnt + jnp.sum(jnp.where(pred(kk, cc * (2 * t)), 1, 0), axis=0)
        cnt = lax.fori_loop(0, npair, body, jnp.zeros((SUBLANES, t), I32))
        return jnp.sum(cnt, axis=0, keepdims=True)

    @pl.when(nchunk * t > k_top)
    def _():
        hi_u, n_ge_hi = bisect16(hi_ref, k_top, jnp.full((1, t), nchunk * t, I32))
        hi_b = jnp.broadcast_to((hi_u - 0x8000).astype(I16), (PACKED_ROWS, t))
        n_gt_hi = count16(hi_ref, lambda v: v > hi_b[None])

        def mask_low(cc, carry):
            start = pl.multiple_of(cc * (2 * t), 2 * t)
            hv = hi_ref[pl.ds(start, 2 * t), :].reshape(2 * t // PACKED_ROWS, PACKED_ROWS, t)
            lv = lo_ref[pl.ds(start, 2 * t), :].reshape(2 * t // PACKED_ROWS, PACKED_ROWS, t)
            lo_ref[pl.ds(start, 2 * t), :] = jnp.where(
                hv == hi_b[None], lv, jnp.full((), -0x8000, I16)).reshape(2 * t, t)
            return carry

        lax.fori_loop(0, npair, mask_low, 0)
        lo_u, n_ge_lo = bisect16(lo_ref, k_top - n_gt_hi, n_ge_hi - n_gt_hi)
        tau = (hi_u - 0x8000) * 65536 + lo_u
        n_ge = n_gt_hi + n_ge_lo
        tau_ref[...] = jnp.broadcast_to(jnp.maximum(tau, NEG_KEY + 1), (SUBLANES, t))

        tie = jnp.logical_and(n_ge > k_top, tau > NEG_KEY)

        @pl.when(jnp.max(jnp.where(tie, 1, 0)) > 0)
        def _():
            tau_b = jnp.broadcast_to(tau, (SUBLANES, t))
            n_gt = count(lambda kk, base: kk > tau_b[None])
            want = k_top - n_gt
            row = lax.broadcasted_iota(I32, (2 * groups, SUBLANES, t), 0) * SUBLANES + \
                lax.broadcasted_iota(I32, (2 * groups, SUBLANES, t), 1)

            def pos_body(i, pos):
                cand = pos | jnp.left_shift(jnp.int32(1), 15 - i)
                cand_b = jnp.broadcast_to(cand, (SUBLANES, t))
                cnt = count(lambda kk, base: jnp.logical_and(kk == tau_b[None],
                                                            row + base < cand_b[None]))
                return jnp.where(cnt < want, cand, pos)

            pos = lax.fori_loop(0, 16, pos_body, jnp.zeros((1, t), I32))
            pos_b = jnp.broadcast_to(jnp.where(tie, pos, jnp.int32(2 ** 30)), (SUBLANES, t))

            def demote_body(cc, carry):
                start = pl.multiple_of(cc * (2 * t), 2 * t)
                kk = keys_ref[pl.ds(start, 2 * t), :].reshape(2 * groups, SUBLANES, t)
                drop = jnp.logical_and(kk == tau_b[None], row + cc * (2 * t) > pos_b[None])
                keys_ref[pl.ds(start, 2 * t), :] = jnp.where(drop, NEG_KEY, kk).reshape(2 * t, t)
                return carry

            lax.fori_loop(0, npair, demote_body, 0)

    m_ref[...] = jnp.full(m_ref.shape, -jnp.inf, F32)
    acc_ref[...] = jnp.zeros(acc_ref.shape, F32)
    tau_sel = tau_ref[...]
    ones_rows = jnp.ones((PACKED_ROWS, t), BF16)

    def logits_matmul(c, dst):
        kvc = kv_ref[0, pl.ds(chunk_start(c), t), :]
        for h in range(N_HEADS):
            dst[h] = _dot_nt(kvc, q_ref[0, :, h * KV_LORA:(h + 1) * KV_LORA])

    def attend(c, src, bias_tile):
        kvtc = jnp.concatenate([kvt_ref[0, :, pl.ds(chunk_start(c), t)], ones_rows], axis=0)
        kk = keys_ref[pl.ds(chunk_start(c), t), :].reshape(groups, SUBLANES, t)
        mask_ref[...] = jnp.where(kk >= tau_sel[None], 0.0, -jnp.inf).reshape(t, t)
        for h in range(N_HEADS):
            s = src[h] + mask_ref[...]
            if bias_tile is not None:
                s = s + btile_ref[bias_tile, h]
            m_old = m_ref[h]
            m_new = jnp.maximum(m_old, jnp.max(s, axis=0, keepdims=True))
            m_safe = jnp.where(m_new == -jnp.inf, 0.0, m_new)
            alpha = jnp.exp2(m_old - m_safe)
            p = jnp.exp2(s - m_safe)
            m_ref[h] = m_new
            acc_ref[h] = alpha * acc_ref[h] + _dot(kvtc, p.astype(BF16))

    def attend_step(c, cur, nxt, bias_tile):
        logits_matmul(c + 1, s2_refs[nxt])
        attend(c, s2_refs[cur], bias_tile)

    logits_matmul(0, s2_refs[0])
    chunk_loop(jnp.maximum(j - 1, 0), functools.partial(attend_step, bias_tile=None))
    pl.when(j >= 1)(lambda: by_parity(
        j - 1, lambda cur, nxt: attend_step(j - 1, cur, nxt, bias_tile=1)))
    by_parity(j, lambda cur, nxt: attend(j, s2_refs[cur], 0))

    ys = []
    for h in range(N_HEADS):
        denom = acc_ref[h, KV_LORA:KV_LORA + 1, :]
        o_t = (acc_ref[h, 0:KV_LORA, :] * (1.0 / denom)).astype(BF16)
        ys.append(_dot_tn(o_t, wuv_ref[h]))
    y = jnp.concatenate(ys, axis=-1)
    o_ref[0] = (y * sga_ref[0]).astype(BF16)


def _sparse_attention(q, qi, wt, kv, kvt, ki, sga, w_uv, btiles, *, t):
    bsz, s, _ = q.shape
    k_top = min(TOPK_MAX, s // 4)
    assert (s + t) // PACKED_ROWS <= 256, "per-position key counts must stay exact in bf16"
    blk = lambda w: pl.BlockSpec((1, t, w), lambda b, j: (b, j, 0))
    in_specs = [
        blk(N_HEADS * KV_LORA),
        blk(N_IDX_HEADS * D_IDX),
        pl.BlockSpec((1, N_IDX_HEADS, t), lambda b, j: (b, 0, j)),
        pl.BlockSpec((1, s, KV_LORA), lambda b, j: (b, 0, 0)),
        pl.BlockSpec((1, KV_LORA, s), lambda b, j: (b, 0, 0)),
        pl.BlockSpec((1, s, D_IDX), lambda b, j: (b, 0, 0)),
        blk(D_ATTN),
        pl.BlockSpec(w_uv.shape, lambda b, j: (0, 0, 0)),
        pl.BlockSpec(btiles.shape, lambda b, j: (0, 0, 0, 0)),
    ]
    return pl.pallas_call(
        functools.partial(_attn_kernel, t=t, k_top=k_top),
        name="sparse_attention",
        out_shape=jax.ShapeDtypeStruct((bsz, s, D_ATTN), BF16),
        grid=(bsz, s // t),
        in_specs=in_specs,
        out_specs=blk(D_ATTN),
        scratch_shapes=[
            pltpu.VMEM((s + t, t), I32),
            pltpu.VMEM((N_IDX_HEADS * t, D_IDX), BF16),
            pltpu.VMEM((SUBLANES, t), I32),
            pltpu.VMEM((N_HEADS, 1, t), F32),
            pltpu.VMEM((t, t), F32),
            pltpu.VMEM((N_HEADS, KV_LORA + PACKED_ROWS, t), F32),
            pltpu.VMEM((t, N_IDX_HEADS * t), F32),
            pltpu.VMEM((t, N_IDX_HEADS * t), F32),
            pltpu.VMEM((N_HEADS, t, t), F32),
            pltpu.VMEM((N_HEADS, t, t), F32),
            pltpu.VMEM((s + t, t), I16),
            pltpu.VMEM((s + t, t), I16),
        ],
        compiler_params=pltpu.CompilerParams(
            dimension_semantics=("arbitrary", "arbitrary"), vmem_limit_bytes=VMEM_LIMIT),
    )(q, qi, wt, kv, kvt, ki, sga, w_uv, btiles)


def _outproj_kernel(x_ref, yc_ref, ya_ref, wo_ref, mod_ref, gpost_ref, o_ref, *, d_model):
    y = _dot(jnp.concatenate([yc_ref[0], ya_ref[0]], axis=-1), wo_ref[...])
    yn = (y * lax.rsqrt(jnp.mean(y * y, axis=-1, keepdims=True) + EPS)) * gpost_ref[...]
    gate = mod_ref[0, :, 2 * d_model:3 * d_model]
    o_ref[0] = x_ref[0] + gate * yn


def _out_projection(x, yc, ya, w_out, mod_l, g_post, *, tm):
    bsz, s, d = x.shape
    return pl.pallas_call(
        functools.partial(_outproj_kernel, d_model=d),
        name="out_projection",
        out_shape=jax.ShapeDtypeStruct((bsz, s, d), F32),
        grid=(bsz, s // tm),
        in_specs=[
            pl.BlockSpec((1, tm, d), lambda b, j: (b, j, 0)),
            pl.BlockSpec((1, tm, D_CONV), lambda b, j: (b, j, 0)),
            pl.BlockSpec((1, tm, D_ATTN), lambda b, j: (b, j, 0)),
            pl.BlockSpec(w_out.shape, lambda b, j: (0, 0)),
            pl.BlockSpec((1, 1, 3 * d), lambda b, j: (b, 0, 0)),
            pl.BlockSpec((1, d), lambda b, j: (0, 0)),
        ],
        out_specs=pl.BlockSpec((1, tm, d), lambda b, j: (b, j, 0)),
        compiler_params=pltpu.CompilerParams(
            dimension_semantics=("arbitrary", "arbitrary"), vmem_limit_bytes=VMEM_LIMIT),
    )(x, yc, ya, w_out, mod_l, g_post.reshape(1, d))


def _pack_w_in(w_in_l):
    d = w_in_l.shape[0]
    split = OFF_KW + D_IDX + N_IDX_HEADS
    return jnp.concatenate(
        [w_in_l[:, :split], jnp.zeros((d, KW_PAD), w_in_l.dtype), w_in_l[:, split:]],
        axis=-1).astype(BF16)


def kernel(x, c, w_ada, b_ada, g_pre, w_in, conv_w, conv_b, conv_ln_g, conv_ln_b, w_pw2, q_norm_g,
           w_uq, w_qidx, kv_norm_g, w_uv, rel_bias, w_out, g_post):
    depth = w_ada.shape[0]
    bsz, s, d = x.shape
    tm = min(512, s)
    t = min(ATTN_BLOCK, s)
    assert s % tm == 0 and s % t == 0 and tm % CONV_ROWS == 0 and d % LANES == 0

    mod = _modulation(c, w_ada, b_ada)
    btiles = _bias_tiles(rel_bias, t)
    for l in range(depth):
        mod_l = mod[l].reshape(bsz, 1, 3 * d)
        yc, q, qi, kv, kvt, ki, wt, sga = _in_projection(
            x, mod_l, g_pre[l], _pack_w_in(w_in[l]), conv_w[l], conv_b[l], conv_ln_g[l],
            conv_ln_b[l], w_pw2[l].astype(BF16), q_norm_g[l], w_uq[l].astype(BF16),
            w_qidx[l].astype(BF16), kv_norm_g[l], tm=tm)
        ya = _sparse_attention(q, qi, wt, kv, kvt, ki, sga, w_uv[l].astype(BF16), btiles, t=t)
        x = _out_projection(x, yc, ya, w_out[l].astype(BF16), mod_l, g_post[l], tm=tm)
    return x
```

```python
import functools
import math

import jax
import jax.numpy as jnp
from jax import lax
from jax.experimental import pallas as pl
from jax.experimental.pallas import tpu as pltpu

F32 = jnp.float32
BF16 = jnp.bfloat16
I32 = jnp.int32
I16 = jnp.int16

D_CONV = 512
CONV_WIDTH = 31
N_HEADS = 8
D_HEAD_OUT = 64
D_ATTN = N_HEADS * D_HEAD_OUT
Q_LORA = 256
KV_LORA = 128
N_IDX_HEADS = 8
D_IDX = 64
TOPK_MAX = 256
N_BUCKETS = 32
MAX_DISTANCE = 128
EPS = 1e-6

KW_PAD = 128 - D_IDX - N_IDX_HEADS
OFF_UVAL = 0
OFF_UGATE = OFF_UVAL + D_CONV
OFF_GCONV = OFF_UGATE + D_CONV
OFF_CQ = OFF_GCONV + D_CONV
OFF_CKV = OFF_CQ + Q_LORA
OFF_KW = OFF_CKV + KV_LORA
OFF_GATTN = OFF_KW + 128
D_IN_PACKED = OFF_GATTN + D_ATTN

HALO = 32
CONV_ROWS = 64
ATTN_BLOCK = 256
LANES = 128
SUBLANES = 8
PACKED_ROWS = 2 * SUBLANES
VMEM_LIMIT = 48 * 1024 * 1024

INT_MIN = -(2 ** 31)
NEG_KEY = -0x7F800000
LOG2E = math.log2(math.e)


def _dot(a, b):
    return jnp.dot(a, b, preferred_element_type=F32)


def _dot_nt(a, b):
    return lax.dot_general(a, b, (((1,), (1,)), ((), ())), preferred_element_type=F32)


def _dot_tn(a, b):
    return lax.dot_general(a, b, (((0,), (0,)), ((), ())), preferred_element_type=F32)


def _silu(v):
    return v * jax.nn.sigmoid(v)


def _mod_kernel(c_ref, w_ref, b_ref, o_ref):
    c = c_ref[...]
    o_ref[0] = _dot(_silu(c).astype(BF16), w_ref[0].astype(BF16)) + b_ref[0]


def _modulation(c, w_ada, b_ada):
    depth, d, d3 = w_ada.shape
    bsz = c.shape[0]
    tn = 1024
    return pl.pallas_call(
        _mod_kernel,
        name="adaln_mod",
        out_shape=jax.ShapeDtypeStruct((depth, bsz, d3), F32),
        grid=(depth, d3 // tn),
        in_specs=[
            pl.BlockSpec((bsz, d), lambda l, n: (0, 0)),
            pl.BlockSpec((1, d, tn), lambda l, n: (l, 0, n)),
            pl.BlockSpec((1, 1, tn), lambda l, n: (l, 0, n)),
        ],
        out_specs=pl.BlockSpec((1, bsz, tn), lambda l, n: (l, 0, n)),
        compiler_params=pltpu.CompilerParams(
            dimension_semantics=("arbitrary", "arbitrary"), vmem_limit_bytes=VMEM_LIMIT),
    )(c, w_ada, b_ada.reshape(depth, 1, d3))


def _t5_bucket(n):
    max_exact = N_BUCKETS // 2
    n = jnp.maximum(n, 0)
    nf = jnp.maximum(n, 1).astype(F32)
    large = max_exact + (jnp.log(nf / max_exact) / math.log(MAX_DISTANCE / max_exact)
                         * (N_BUCKETS - max_exact)).astype(I32)
    large = jnp.minimum(large, N_BUCKETS - 1)
    return jnp.where(n < max_exact, n, large)


def _bias_kernel(rb_ref, o_ref, *, t):
    sk = lax.broadcasted_iota(I32, (t, t), 0)
    tq = lax.broadcasted_iota(I32, (t, t), 1)
    for kind in range(2):
        bucket = _t5_bucket(tq - sk + kind * t)
        for h in range(N_HEADS):
            val = jnp.zeros((t, t), F32)
            for b in range(N_BUCKETS):
                val = jnp.where(bucket == b, rb_ref[b, h], val)
            o_ref[kind, h] = (val - rb_ref[N_BUCKETS - 1, h]) * LOG2E


def _bias_tiles(rel_bias, t):
    return pl.pallas_call(
        functools.partial(_bias_kernel, t=t),
        name="t5_bias_tiles",
        out_shape=jax.ShapeDtypeStruct((2, N_HEADS, t, t), F32),
        in_specs=[pl.BlockSpec(memory_space=pltpu.SMEM)],
        out_specs=pl.BlockSpec(memory_space=pltpu.VMEM),
    )(rel_bias)


def _inproj_kernel(x_ref, mod_ref, gpre_ref, w_ref, cw_ref, cb_ref, lng_ref, lnb_ref, wpw2_ref,
                   qg_ref, wuq_ref, wqi_ref, kvg_ref,
                   yc_ref, q_ref, qi_ref, kv_ref, kvt_ref, ki_ref, wt_ref, sga_ref,
                   hbuf, abuf, ybuf, shbuf, gcbuf, *, tm, d_model):
    j = pl.program_id(1)

    x = x_ref[0]
    ms = jnp.mean(x * x, axis=-1, keepdims=True)
    xn = (x * lax.rsqrt(ms + EPS)) * gpre_ref[...]
    shift = mod_ref[0, :, 0:d_model]
    scale = mod_ref[0, :, d_model:2 * d_model]
    hbuf[...] = (xn * (1.0 + scale) + shift).astype(BF16)

    u_val = _dot(hbuf[...], w_ref[:, OFF_UVAL:OFF_UVAL + D_CONV])
    u_gate = _dot(hbuf[...], w_ref[:, OFF_UGATE:OFF_UGATE + D_CONV])

    @pl.when(j == 0)
    def _():
        abuf[0:HALO, :] = jnp.zeros((HALO, D_CONV), F32)

    abuf[HALO:HALO + tm, :] = u_val * jax.nn.sigmoid(u_gate)

    span = tm + HALO - SUBLANES
    for r in range(1, SUBLANES):
        shbuf[r - 1, 0:span, :] = abuf[r:r + span, :]

    held = {}

    def proj_gconv():
        gcbuf[...] = _silu(_dot(hbuf[...], w_ref[:, OFF_GCONV:OFF_GCONV + D_CONV]))

    def proj_cq():
        c_q = _dot(hbuf[...], w_ref[:, OFF_CQ:OFF_CQ + Q_LORA])
        cq = (c_q * lax.rsqrt(jnp.mean(c_q * c_q, axis=-1, keepdims=True) + EPS)) * qg_ref[...]
        held["cq"] = cq.astype(BF16)

    def proj_q():
        q_ref[0] = (_dot(held["cq"], wuq_ref[...]) * (KV_LORA ** -0.5 * LOG2E)).astype(BF16)

    def proj_qi():
        qi_ref[0] = _dot(held["cq"], wqi_ref[...]).astype(BF16)

    def proj_kv():
        c_kv = _dot(hbuf[...], w_ref[:, OFF_CKV:OFF_CKV + KV_LORA])
        kvn = (c_kv * lax.rsqrt(jnp.mean(c_kv * c_kv, axis=-1, keepdims=True) + EPS)) * kvg_ref[...]
        kv_ref[0] = kvn.astype(BF16)
        kvt_ref[0] = jnp.transpose(kvn).astype(BF16)

    def proj_kw():
        kw = _dot(hbuf[...], w_ref[:, OFF_KW:OFF_KW + 128])
        ki_ref[0] = kw[:, 0:D_IDX].astype(BF16)
        kwt = jnp.transpose(kw)
        wt_ref[0] = kwt[D_IDX:D_IDX + N_IDX_HEADS, :] * (N_IDX_HEADS ** -0.5 * D_IDX ** -0.5)

    def proj_gattn():
        sga_ref[0] = _silu(_dot(hbuf[...], w_ref[:, OFF_GATTN:OFF_GATTN + D_ATTN]))

    steps = [proj_gconv, proj_cq, proj_q, proj_qi, proj_kv, proj_kw, proj_gattn]

    first = HALO - (CONV_WIDTH - 1)
    for i, r0 in enumerate(range(0, tm, CONV_ROWS)):
        if i < len(steps):
            steps[i]()
        acc = jnp.broadcast_to(cb_ref[...], (CONV_ROWS, D_CONV))
        for k in range(CONV_WIDTH):
            q8, r = divmod(first + k, SUBLANES)
            rows = slice(r0 + q8 * SUBLANES, r0 + q8 * SUBLANES + CONV_ROWS)
            tap = abuf[rows, :] if r == 0 else shbuf[r - 1, rows, :]
            acc = acc + cw_ref[k:k + 1, :] * tap
        mu = jnp.mean(acc, axis=-1, keepdims=True)
        cen = acc - mu
        var = jnp.mean(cen * cen, axis=-1, keepdims=True)
        yn = cen * lax.rsqrt(var + EPS) * lng_ref[...] + lnb_ref[...]
        ybuf[r0:r0 + CONV_ROWS, :] = _silu(yn).astype(BF16)

    for step in steps[tm // CONV_ROWS:]:
        step()

    abuf[0:HALO, :] = abuf[tm:tm + HALO, :]
    yc_ref[0] = (_dot(ybuf[...], wpw2_ref[...]) * gcbuf[...]).astype(BF16)


def _in_projection(x, mod_l, g_pre, w_packed, conv_w, conv_b, ln_g, ln_b, w_pw2, q_g, w_uq, w_qi,
                   kv_g, *, tm):
    bsz, s, d = x.shape
    row = lambda v: v.reshape(1, -1)
    full = lambda a: pl.BlockSpec(a.shape, lambda b, j: (0,) * a.ndim)
    args = (x, mod_l, row(g_pre), w_packed, conv_w, row(conv_b), row(ln_g), row(ln_b), w_pw2,
            row(q_g), w_uq, w_qi, row(kv_g))
    in_specs = [
        pl.BlockSpec((1, tm, d), lambda b, j: (b, j, 0)),
        pl.BlockSpec((1, 1, 3 * d), lambda b, j: (b, 0, 0)),
    ] + [full(a) for a in args[2:]]
    blk = lambda w: pl.BlockSpec((1, tm, w), lambda b, j: (b, j, 0))
    out_shape = (
        jax.ShapeDtypeStruct((bsz, s, D_CONV), BF16),
        jax.ShapeDtypeStruct((bsz, s, N_HEADS * KV_LORA), BF16),
        jax.ShapeDtypeStruct((bsz, s, N_IDX_HEADS * D_IDX), BF16),
        jax.ShapeDtypeStruct((bsz, s, KV_LORA), BF16),
        jax.ShapeDtypeStruct((bsz, KV_LORA, s), BF16),
        jax.ShapeDtypeStruct((bsz, s, D_IDX), BF16),
        jax.ShapeDtypeStruct((bsz, N_IDX_HEADS, s), F32),
        jax.ShapeDtypeStruct((bsz, s, D_ATTN), F32),
    )
    out_specs = (
        blk(D_CONV), blk(N_HEADS * KV_LORA), blk(N_IDX_HEADS * D_IDX), blk(KV_LORA),
        pl.BlockSpec((1, KV_LORA, tm), lambda b, j: (b, 0, j)),
        blk(D_IDX),
        pl.BlockSpec((1, N_IDX_HEADS, tm), lambda b, j: (b, 0, j)),
        blk(D_ATTN),
    )
    return pl.pallas_call(
        functools.partial(_inproj_kernel, tm=tm, d_model=d),
        name="in_projection",
        out_shape=out_shape,
        grid=(bsz, s // tm),
        in_specs=in_specs,
        out_specs=out_specs,
        scratch_shapes=[
            pltpu.VMEM((tm, d), BF16),
            pltpu.VMEM((tm + HALO, D_CONV), F32),
            pltpu.VMEM((tm, D_CONV), BF16),
            pltpu.VMEM((SUBLANES - 1, tm + HALO, D_CONV), F32),
            pltpu.VMEM((tm, D_CONV), F32),
        ],
        compiler_params=pltpu.CompilerParams(
            dimension_semantics=("arbitrary", "arbitrary"), vmem_limit_bytes=VMEM_LIMIT),
    )(*args)


def _sortable(v):
    b = pltpu.bitcast(v, I32)
    return jnp.where(b < 0, INT_MIN - b, b)


def _attn_kernel(q_ref, qi_ref, wt_ref, kv_ref, kvt_ref, ki_ref, sga_ref, wuv_ref, btile_ref,
                 o_ref, keys_ref, qis_ref, tau_ref, m_ref, mask_ref, acc_ref, sa0_ref, sa1_ref,
                 s2a_ref, s2b_ref, hi_ref, lo_ref, *, t, k_top):
    sa_refs = (sa0_ref, sa1_ref)
    s2_refs = (s2a_ref, s2b_ref)
    j = pl.program_id(1)
    nchunk = j + 1
    npair = (j + 2) // 2
    groups = t // SUBLANES

    def chunk_start(c):
        return pl.multiple_of(c * t, t)

    for h in range(N_IDX_HEADS):
        qis_ref[h * t:(h + 1) * t, :] = qi_ref[0, :, h * D_IDX:(h + 1) * D_IDX]

    def by_parity(c, fn):
        for parity, (cur, nxt) in enumerate(((0, 1), (1, 0))):
            pl.when((c & 1) == parity)(functools.partial(fn, cur, nxt))

    def chunk_loop(n, step):
        def pair_body(cc, carry):
            step(2 * cc, 0, 1)
            step(2 * cc + 1, 1, 0)
            return carry
        lax.fori_loop(0, n // 2, pair_body, 0)
        pl.when(n % 2 == 1)(lambda: step(n - 1, 0, 1))

    def score_matmul(c, dst):
        kc = ki_ref[0, pl.ds(chunk_start(c), t), :]
        dst[...] = _dot_nt(kc, qis_ref[...])

    def score_finish(src):
        acc = jnp.zeros((t, t), F32)
        for h in range(N_IDX_HEADS):
            acc = acc + wt_ref[0, h:h + 1, :] * jnp.maximum(src[:, h * t:(h + 1) * t], 0.0)
        return _sortable(acc)

    def store_keys(c, key):
        keys_ref[pl.ds(chunk_start(c), t), :] = key
        hi_ref[pl.ds(chunk_start(c), t), :] = (key >> 16).astype(I16)
        lo_ref[pl.ds(chunk_start(c), t), :] = ((key & 0xFFFF) - 0x8000).astype(I16)

    def score_step(c, cur, nxt):
        score_matmul(c + 1, sa_refs[nxt])
        store_keys(c, score_finish(sa_refs[cur]))

    score_matmul(0, sa_refs[0])
    chunk_loop(j, score_step)
    sk = lax.broadcasted_iota(I32, (t, t), 0)
    tq = lax.broadcasted_iota(I32, (t, t), 1)

    by_parity(j, lambda cur, nxt: store_keys(
        j, jnp.where(sk <= tq, score_finish(sa_refs[cur]), NEG_KEY)))
    store_keys(j + 1, jnp.full((t, t), NEG_KEY, I32))

    tau_ref[...] = jnp.full((SUBLANES, t), NEG_KEY + 1, I32)

    def packed_rows(ref, cc):
        return ref[2 * t * cc:2 * t * (cc + 1), :].reshape(2 * t // PACKED_ROWS, PACKED_ROWS, t)

    def count16(plane_ref, pred, pairs):
        parts = []
        for cc in range(pairs):
            ones = jnp.where(pred(packed_rows(plane_ref, cc)), jnp.ones((), BF16),
                             jnp.zeros((), BF16))
            parts += [ones[i] for i in range(2 * t // PACKED_ROWS)]
        while len(parts) > 1:
            parts = [sum(parts[i:i + 2][1:], parts[i]) for i in range(0, len(parts), 2)]
        return jnp.sum(parts[0].astype(F32), axis=0, keepdims=True).astype(I32)

    def bisect16(plane_ref, want, n_all, pairs):
        def bit_body(i, carry):
            prefix, n_ge = carry
            cand = prefix | jnp.left_shift(jnp.int32(1), 15 - i)
            cand_b = jnp.broadcast_to((cand - 0x8000).astype(I16), (PACKED_ROWS, t))
            cnt = count16(plane_ref, lambda v: v >= cand_b[None], pairs)
            ok = cnt >= want
            return jnp.where(ok, cand, prefix), jnp.where(ok, cnt, n_ge)
        return lax.fori_loop(0, 16, bit_body, (jnp.zeros((1, t), I32), n_all))

    def count(pred):
        def body(cc, cnt):
            start = pl.multiple_of(cc * (2 * t), 2 * t)
            kk = keys_ref[pl.ds(start, 2 * t), :].reshape(2 * groups, SUBLANES, t)
            return cnt + jnp.sum(jnp.where(pred(kk, cc * (2 * t)), 1, 0), axis=0)
        cnt = lax.fori_loop(0, npair, body, jnp.zeros((SUBLANES, t), I32))
        return jnp.sum(cnt, axis=0, keepdims=True)

    def select_threshold(pairs):
        hi_u, n_ge_hi = bisect16(hi_ref, k_top, jnp.full((1, t), nchunk * t, I32), pairs)
        hi_b = jnp.broadcast_to((hi_u - 0x8000).astype(I16), (PACKED_ROWS, t))
        n_gt_hi = count16(hi_ref, lambda v: v > hi_b[None], pairs)
        for cc in range(pairs):
            lo_ref[2 * t * cc:2 * t * (cc + 1), :] = jnp.where(
                packed_rows(hi_ref, cc) == hi_b[None], packed_rows(lo_ref, cc),
                jnp.full((), -0x8000, I16)).reshape(2 * t, t)
        lo_u, n_ge_lo = bisect16(lo_ref, k_top - n_gt_hi, n_ge_hi - n_gt_hi, pairs)
        tau = (hi_u - 0x8000) * 65536 + lo_u
        n_ge = n_gt_hi + n_ge_lo
        tau_ref[...] = jnp.broadcast_to(jnp.maximum(tau, NEG_KEY + 1), (SUBLANES, t))

        tie = jnp.logical_and(n_ge > k_top, tau > NEG_KEY)

        @pl.when(jnp.max(jnp.where(tie, 1, 0)) > 0)
        def _():
            tau_b = jnp.broadcast_to(tau, (SUBLANES, t))
            n_gt = count(lambda kk, base: kk > tau_b[None])
            want = k_top - n_gt
            row = lax.broadcasted_iota(I32, (2 * groups, SUBLANES, t), 0) * SUBLANES + \
                lax.broadcasted_iota(I32, (2 * groups, SUBLANES, t), 1)

            def pos_body(i, pos):
                cand = pos | jnp.left_shift(jnp.int32(1), 15 - i)
                cand_b = jnp.broadcast_to(cand, (SUBLANES, t))
                cnt = count(lambda kk, base: jnp.logical_and(kk == tau_b[None],
                                                            row + base < cand_b[None]))
                return jnp.where(cnt < want, cand, pos)

            pos = lax.fori_loop(0, 16, pos_body, jnp.zeros((1, t), I32))
            pos_b = jnp.broadcast_to(jnp.where(tie, pos, jnp.int32(2 ** 30)), (SUBLANES, t))

            def demote_body(cc, carry):
                start = pl.multiple_of(cc * (2 * t), 2 * t)
                kk = keys_ref[pl.ds(start, 2 * t), :].reshape(2 * groups, SUBLANES, t)
                drop = jnp.logical_and(kk == tau_b[None], row + cc * (2 * t) > pos_b[None])
                keys_ref[pl.ds(start, 2 * t), :] = jnp.where(drop, NEG_KEY, kk).reshape(2 * t, t)
                return carry

            lax.fori_loop(0, npair, demote_body, 0)

    max_pairs = (keys_ref.shape[0] // t) // 2
    for pairs in range(1, max_pairs + 1):
        pl.when(jnp.logical_and(npair == pairs, nchunk * t > k_top))(
            functools.partial(select_threshold, pairs))

    m_ref[...] = jnp.full(m_ref.shape, -jnp.inf, F32)
    acc_ref[...] = jnp.zeros(acc_ref.shape, F32)
    tau_sel = tau_ref[...]
    ones_rows = jnp.ones((PACKED_ROWS, t), BF16)

    def logits_matmul(c, dst):
        kvc = kv_ref[0, pl.ds(chunk_start(c), t), :]
        for h in range(N_HEADS):
            dst[h] = _dot_nt(kvc, q_ref[0, :, h * KV_LORA:(h + 1) * KV_LORA])

    def attend(c, src, bias_tile):
        kvtc = jnp.concatenate([kvt_ref[0, :, pl.ds(chunk_start(c), t)], ones_rows], axis=0)
        kk = keys_ref[pl.ds(chunk_start(c), t), :].reshape(groups, SUBLANES, t)
        mask_ref[...] = jnp.where(kk >= tau_sel[None], 0.0, -jnp.inf).reshape(t, t)
        for h in range(N_HEADS):
            s = src[h] + mask_ref[...]
            if bias_tile is not None:
                s = s + btile_ref[bias_tile, h]
            m_old = m_ref[h]
            m_new = jnp.maximum(m_old, jnp.max(s, axis=0, keepdims=True))
            m_safe = jnp.where(m_new == -jnp.inf, 0.0, m_new)
            alpha = jnp.exp2(m_old - m_safe)
            p = jnp.exp2(s - m_safe)
            m_ref[h] = m_new
            acc_ref[h] = alpha * acc_ref[h] + _dot(kvtc, p.astype(BF16))

    def attend_step(c, cur, nxt, bias_tile):
        logits_matmul(c + 1, s2_refs[nxt])
        attend(c, s2_refs[cur], bias_tile)

    logits_matmul(0, s2_refs[0])
    chunk_loop(jnp.maximum(j - 1, 0), functools.partial(attend_step, bias_tile=None))
    pl.when(j >= 1)(lambda: by_parity(
        j - 1, lambda cur, nxt: attend_step(j - 1, cur, nxt, bias_tile=1)))
    by_parity(j, lambda cur, nxt: attend(j, s2_refs[cur], 0))

    ys = []
    for h in range(N_HEADS):
        denom = acc_ref[h, KV_LORA:KV_LORA + 1, :]
        o_t = (acc_ref[h, 0:KV_LORA, :] * (1.0 / denom)).astype(BF16)
        ys.append(_dot_tn(o_t, wuv_ref[h]))
    y = jnp.concatenate(ys, axis=-1)
    o_ref[0] = (y * sga_ref[0]).astype(BF16)


def _sparse_attention(q, qi, wt, kv, kvt, ki, sga, w_uv, btiles, *, t):
    bsz, s, _ = q.shape
    k_top = min(TOPK_MAX, s // 4)
    assert (s + t) // PACKED_ROWS <= 256, "per-position key counts must stay exact in bf16"
    blk = lambda w: pl.BlockSpec((1, t, w), lambda b, j: (b, j, 0))
    in_specs = [
        blk(N_HEADS * KV_LORA),
        blk(N_IDX_HEADS * D_IDX),
        pl.BlockSpec((1, N_IDX_HEADS, t), lambda b, j: (b, 0, j)),
        pl.BlockSpec((1, s, KV_LORA), lambda b, j: (b, 0, 0)),
        pl.BlockSpec((1, KV_LORA, s), lambda b, j: (b, 0, 0)),
        pl.BlockSpec((1, s, D_IDX), lambda b, j: (b, 0, 0)),
        blk(D_ATTN),
        pl.BlockSpec(w_uv.shape, lambda b, j: (0, 0, 0)),
        pl.BlockSpec(btiles.shape, lambda b, j: (0, 0, 0, 0)),
    ]
    return pl.pallas_call(
        functools.partial(_attn_kernel, t=t, k_top=k_top),
        name="sparse_attention",
        out_shape=jax.ShapeDtypeStruct((bsz, s, D_ATTN), BF16),
        grid=(bsz, s // t),
        in_specs=in_specs,
        out_specs=blk(D_ATTN),
        scratch_shapes=[
            pltpu.VMEM((s + t, t), I32),
            pltpu.VMEM((N_IDX_HEADS * t, D_IDX), BF16),
            pltpu.VMEM((SUBLANES, t), I32),
            pltpu.VMEM((N_HEADS, 1, t), F32),
            pltpu.VMEM((t, t), F32),
            pltpu.VMEM((N_HEADS, KV_LORA + PACKED_ROWS, t), F32),
            pltpu.VMEM((t, N_IDX_HEADS * t), F32),
            pltpu.VMEM((t, N_IDX_HEADS * t), F32),
            pltpu.VMEM((N_HEADS, t, t), F32),
            pltpu.VMEM((N_HEADS, t, t), F32),
            pltpu.VMEM((s + t, t), I16),
            pltpu.VMEM((s + t, t), I16),
        ],
        compiler_params=pltpu.CompilerParams(
            dimension_semantics=("arbitrary", "arbitrary"), vmem_limit_bytes=VMEM_LIMIT),
    )(q, qi, wt, kv, kvt, ki, sga, w_uv, btiles)


def _outproj_kernel(x_ref, yc_ref, ya_ref, wo_ref, mod_ref, gpost_ref, o_ref, *, d_model):
    y = _dot(jnp.concatenate([yc_ref[0], ya_ref[0]], axis=-1), wo_ref[...])
    yn = (y * lax.rsqrt(jnp.mean(y * y, axis=-1, keepdims=True) + EPS)) * gpost_ref[...]
    gate = mod_ref[0, :, 2 * d_model:3 * d_model]
    o_ref[0] = x_ref[0] + gate * yn


def _out_projection(x, yc, ya, w_out, mod_l, g_post, *, tm):
    bsz, s, d = x.shape
    return pl.pallas_call(
        functools.partial(_outproj_kernel, d_model=d),
        name="out_projection",
        out_shape=jax.ShapeDtypeStruct((bsz, s, d), F32),
        grid=(bsz, s // tm),
        in_specs=[
            pl.BlockSpec((1, tm, d), lambda b, j: (b, j, 0)),
            pl.BlockSpec((1, tm, D_CONV), lambda b, j: (b, j, 0)),
            pl.BlockSpec((1, tm, D_ATTN), lambda b, j: (b, j, 0)),
            pl.BlockSpec(w_out.shape, lambda b, j: (0, 0)),
            pl.BlockSpec((1, 1, 3 * d), lambda b, j: (b, 0, 0)),
            pl.BlockSpec((1, d), lambda b, j: (0, 0)),
        ],
        out_specs=pl.BlockSpec((1, tm, d), lambda b, j: (b, j, 0)),
        compiler_params=pltpu.CompilerParams(
            dimension_semantics=("arbitrary", "arbitrary"), vmem_limit_bytes=VMEM_LIMIT),
    )(x, yc, ya, w_out, mod_l, g_post.reshape(1, d))


def _pack_w_in(w_in_l):
    d = w_in_l.shape[0]
    split = OFF_KW + D_IDX + N_IDX_HEADS
    return jnp.concatenate(
        [w_in_l[:, :split], jnp.zeros((d, KW_PAD), w_in_l.dtype), w_in_l[:, split:]],
        axis=-1).astype(BF16)


def kernel(x, c, w_ada, b_ada, g_pre, w_in, conv_w, conv_b, conv_ln_g, conv_ln_b, w_pw2, q_norm_g,
           w_uq, w_qidx, kv_norm_g, w_uv, rel_bias, w_out, g_post):
    depth = w_ada.shape[0]
    bsz, s, d = x.shape
    tm = min(512, s)
    t = min(ATTN_BLOCK, s)
    assert s % tm == 0 and s % t == 0 and tm % CONV_ROWS == 0 and d % LANES == 0

    mod = _modulation(c, w_ada, b_ada)
    btiles = _bias_tiles(rel_bias, t)
    for l in range(depth):
        mod_l = mod[l].reshape(bsz, 1, 3 * d)
        yc, q, qi, kv, kvt, ki, wt, sga = _in_projection(
            x, mod_l, g_pre[l], _pack_w_in(w_in[l]), conv_w[l], conv_b[l], conv_ln_g[l],
            conv_ln_b[l], w_pw2[l].astype(BF16), q_norm_g[l], w_uq[l].astype(BF16),
            w_qidx[l].astype(BF16), kv_norm_g[l], tm=tm)
        ya = _sparse_attention(q, qi, wt, kv, kvt, ki, sga, w_uv[l].astype(BF16), btiles, t=t)
        x = _out_projection(x, yc, ya, w_out[l].astype(BF16), mod_l, g_post[l], tm=tm)
    return x
```

```python
import functools
import math

import jax
import jax.numpy as jnp
from jax import lax
from jax.experimental import pallas as pl
from jax.experimental.pallas import tpu as pltpu

F32 = jnp.float32
BF16 = jnp.bfloat16
I32 = jnp.int32
I16 = jnp.int16

D_CONV = 512
CONV_WIDTH = 31
N_HEADS = 8
D_HEAD_OUT = 64
D_ATTN = N_HEADS * D_HEAD_OUT
Q_LORA = 256
KV_LORA = 128
N_IDX_HEADS = 8
D_IDX = 64
TOPK_MAX = 256
N_BUCKETS = 32
MAX_DISTANCE = 128
EPS = 1e-6

KW_PAD = 128 - D_IDX - N_IDX_HEADS
OFF_UVAL = 0
OFF_UGATE = OFF_UVAL + D_CONV
OFF_GCONV = OFF_UGATE + D_CONV
OFF_CQ = OFF_GCONV + D_CONV
OFF_CKV = OFF_CQ + Q_LORA
OFF_KW = OFF_CKV + KV_LORA
OFF_GATTN = OFF_KW + 128
D_IN_PACKED = OFF_GATTN + D_ATTN

HALO = 32
CONV_ROWS = 64
INPROJ_SUB = 256
ATTN_BLOCK = 256
LANES = 128
SUBLANES = 8
PACKED_ROWS = 2 * SUBLANES
VMEM_LIMIT = 48 * 1024 * 1024

INT_MIN = -(2 ** 31)
NEG_KEY = -0x7F800000
LOG2E = math.log2(math.e)


def _dot(a, b):
    return jnp.dot(a, b, preferred_element_type=F32)


def _dot_nt(a, b):
    return lax.dot_general(a, b, (((1,), (1,)), ((), ())), preferred_element_type=F32)


def _dot_tn(a, b):
    return lax.dot_general(a, b, (((0,), (0,)), ((), ())), preferred_element_type=F32)


def _silu(v):
    return v * jax.nn.sigmoid(v)


def _mod_kernel(c_ref, w_ref, b_ref, o_ref):
    c = c_ref[...]
    o_ref[0] = _dot(_silu(c).astype(BF16), w_ref[0].astype(BF16)) + b_ref[0]


def _modulation(c, w_ada, b_ada):
    depth, d, d3 = w_ada.shape
    bsz = c.shape[0]
    tn = 1024
    return pl.pallas_call(
        _mod_kernel,
        name="adaln_mod",
        out_shape=jax.ShapeDtypeStruct((depth, bsz, d3), F32),
        grid=(depth, d3 // tn),
        in_specs=[
            pl.BlockSpec((bsz, d), lambda l, n: (0, 0)),
            pl.BlockSpec((1, d, tn), lambda l, n: (l, 0, n)),
            pl.BlockSpec((1, 1, tn), lambda l, n: (l, 0, n)),
        ],
        out_specs=pl.BlockSpec((1, bsz, tn), lambda l, n: (l, 0, n)),
        compiler_params=pltpu.CompilerParams(
            dimension_semantics=("arbitrary", "arbitrary"), vmem_limit_bytes=VMEM_LIMIT),
    )(c, w_ada, b_ada.reshape(depth, 1, d3))


def _t5_bucket(n):
    max_exact = N_BUCKETS // 2
    n = jnp.maximum(n, 0)
    nf = jnp.maximum(n, 1).astype(F32)
    large = max_exact + (jnp.log(nf / max_exact) / math.log(MAX_DISTANCE / max_exact)
                         * (N_BUCKETS - max_exact)).astype(I32)
    large = jnp.minimum(large, N_BUCKETS - 1)
    return jnp.where(n < max_exact, n, large)


def _bias_kernel(rb_ref, o_ref, *, t):
    sk = lax.broadcasted_iota(I32, (t, t), 0)
    tq = lax.broadcasted_iota(I32, (t, t), 1)
    for kind in range(2):
        bucket = _t5_bucket(tq - sk + kind * t)
        for h in range(N_HEADS):
            val = jnp.zeros((t, t), F32)
            for b in range(N_BUCKETS):
                val = jnp.where(bucket == b, rb_ref[b, h], val)
            o_ref[kind, h] = (val - rb_ref[N_BUCKETS - 1, h]) * LOG2E


def _bias_tiles(rel_bias, t):
    return pl.pallas_call(
        functools.partial(_bias_kernel, t=t),
        name="t5_bias_tiles",
        out_shape=jax.ShapeDtypeStruct((2, N_HEADS, t, t), F32),
        in_specs=[pl.BlockSpec(memory_space=pltpu.SMEM)],
        out_specs=pl.BlockSpec(memory_space=pltpu.VMEM),
    )(rel_bias)


def _inproj_kernel(x_ref, mod_ref, gpre_ref, w_ref, cw_ref, cb_ref, lng_ref, lnb_ref, wpw2_ref,
                   qg_ref, wuq_ref, wqi_ref, kvg_ref,
                   yc_ref, q_ref, qi_ref, kv_ref, kvt_ref, ki_ref, wt_ref, sga_ref,
                   hbuf, abuf, ybuf, shbuf, gcbuf, *, tm, d_model):
    j = pl.program_id(1)

    @pl.when(j == 0)
    def _():
        abuf[0:HALO, :] = jnp.zeros((HALO, D_CONV), F32)

    shift = mod_ref[0, :, 0:d_model]
    scale = mod_ref[0, :, d_model:2 * d_model]
    first = HALO - (CONV_WIDTH - 1)

    for base in range(0, tm, INPROJ_SUB):
        rows = slice(base, base + INPROJ_SUB)
        x = x_ref[0, rows, :]
        ms = jnp.mean(x * x, axis=-1, keepdims=True)
        xn = (x * lax.rsqrt(ms + EPS)) * gpre_ref[...]
        hbuf[rows, :] = (xn * (1.0 + scale) + shift).astype(BF16)
        h = hbuf[rows, :]

        u_val = _dot(h, w_ref[:, OFF_UVAL:OFF_UVAL + D_CONV])
        u_gate = _dot(h, w_ref[:, OFF_UGATE:OFF_UGATE + D_CONV])
        abuf[HALO + base:HALO + base + INPROJ_SUB, :] = u_val * jax.nn.sigmoid(u_gate)

        span = slice(base, base + INPROJ_SUB + HALO - SUBLANES)
        for r in range(1, SUBLANES):
            shbuf[r - 1, span, :] = abuf[base + r:base + r + INPROJ_SUB + HALO - SUBLANES, :]

        gcbuf[rows, :] = _silu(_dot(h, w_ref[:, OFF_GCONV:OFF_GCONV + D_CONV]))

        c_q = _dot(h, w_ref[:, OFF_CQ:OFF_CQ + Q_LORA])
        cq = (c_q * lax.rsqrt(jnp.mean(c_q * c_q, axis=-1, keepdims=True) + EPS)) * qg_ref[...]
        cq = cq.astype(BF16)
        q_ref[0, rows, :] = (_dot(cq, wuq_ref[...]) * (KV_LORA ** -0.5 * LOG2E)).astype(BF16)
        qi_ref[0, rows, :] = _dot(cq, wqi_ref[...]).astype(BF16)

        c_kv = _dot(h, w_ref[:, OFF_CKV:OFF_CKV + KV_LORA])
        kvn = (c_kv * lax.rsqrt(jnp.mean(c_kv * c_kv, axis=-1, keepdims=True) + EPS)) * kvg_ref[...]
        kv_ref[0, rows, :] = kvn.astype(BF16)
        kvt_ref[0, :, rows] = jnp.transpose(kvn).astype(BF16)

        kw = _dot(h, w_ref[:, OFF_KW:OFF_KW + 128])
        ki_ref[0, rows, :] = kw[:, 0:D_IDX].astype(BF16)
        kwt = jnp.transpose(kw)
        wt_ref[0, :, rows] = kwt[D_IDX:D_IDX + N_IDX_HEADS, :] * (N_IDX_HEADS ** -0.5 * D_IDX ** -0.5)

        sga_ref[0, rows, :] = _silu(_dot(h, w_ref[:, OFF_GATTN:OFF_GATTN + D_ATTN]))

    for base in range(0, tm, INPROJ_SUB):
        rows = slice(base, base + INPROJ_SUB)
        for r0 in range(base, base + INPROJ_SUB, CONV_ROWS):
            acc = jnp.broadcast_to(cb_ref[...], (CONV_ROWS, D_CONV))
            for k in range(CONV_WIDTH):
                q8, r = divmod(first + k, SUBLANES)
                taps = slice(r0 + q8 * SUBLANES, r0 + q8 * SUBLANES + CONV_ROWS)
                tap = abuf[taps, :] if r == 0 else shbuf[r - 1, taps, :]
                acc = acc + cw_ref[k:k + 1, :] * tap
            mu = jnp.mean(acc, axis=-1, keepdims=True)
            cen = acc - mu
            var = jnp.mean(cen * cen, axis=-1, keepdims=True)
            yn = cen * lax.rsqrt(var + EPS) * lng_ref[...] + lnb_ref[...]
            ybuf[r0:r0 + CONV_ROWS, :] = _silu(yn).astype(BF16)

        yc_ref[0, rows, :] = (_dot(ybuf[rows, :], wpw2_ref[...]) * gcbuf[rows, :]).astype(BF16)

    abuf[0:HALO, :] = abuf[tm:tm + HALO, :]


def _in_projection(x, mod_l, g_pre, w_packed, conv_w, conv_b, ln_g, ln_b, w_pw2, q_g, w_uq, w_qi,
                   kv_g, *, tm):
    bsz, s, d = x.shape
    row = lambda v: v.reshape(1, -1)
    full = lambda a: pl.BlockSpec(a.shape, lambda b, j: (0,) * a.ndim)
    args = (x, mod_l, row(g_pre), w_packed, conv_w, row(conv_b), row(ln_g), row(ln_b), w_pw2,
            row(q_g), w_uq, w_qi, row(kv_g))
    in_specs = [
        pl.BlockSpec((1, tm, d), lambda b, j: (b, j, 0)),
        pl.BlockSpec((1, 1, 3 * d), lambda b, j: (b, 0, 0)),
    ] + [full(a) for a in args[2:]]
    blk = lambda w: pl.BlockSpec((1, tm, w), lambda b, j: (b, j, 0))
    out_shape = (
        jax.ShapeDtypeStruct((bsz, s, D_CONV), BF16),
        jax.ShapeDtypeStruct((bsz, s, N_HEADS * KV_LORA), BF16),
        jax.ShapeDtypeStruct((bsz, s, N_IDX_HEADS * D_IDX), BF16),
        jax.ShapeDtypeStruct((bsz, s, KV_LORA), BF16),
        jax.ShapeDtypeStruct((bsz, KV_LORA, s), BF16),
        jax.ShapeDtypeStruct((bsz, s, D_IDX), BF16),
        jax.ShapeDtypeStruct((bsz, N_IDX_HEADS, s), F32),
        jax.ShapeDtypeStruct((bsz, s, D_ATTN), F32),
    )
    out_specs = (
        blk(D_CONV), blk(N_HEADS * KV_LORA), blk(N_IDX_HEADS * D_IDX), blk(KV_LORA),
        pl.BlockSpec((1, KV_LORA, tm), lambda b, j: (b, 0, j)),
        blk(D_IDX),
        pl.BlockSpec((1, N_IDX_HEADS, tm), lambda b, j: (b, 0, j)),
        blk(D_ATTN),
    )
    return pl.pallas_call(
        functools.partial(_inproj_kernel, tm=tm, d_model=d),
        name="in_projection",
        out_shape=out_shape,
        grid=(bsz, s // tm),
        in_specs=in_specs,
        out_specs=out_specs,
        scratch_shapes=[
            pltpu.VMEM((tm, d), BF16),
            pltpu.VMEM((tm + HALO, D_CONV), F32),
            pltpu.VMEM((tm, D_CONV), BF16),
            pltpu.VMEM((SUBLANES - 1, tm + HALO, D_CONV), F32),
            pltpu.VMEM((tm, D_CONV), F32),
        ],
        compiler_params=pltpu.CompilerParams(
            dimension_semantics=("arbitrary", "arbitrary"), vmem_limit_bytes=VMEM_LIMIT),
    )(*args)


def _sortable(v):
    b = pltpu.bitcast(v, I32)
    return jnp.where(b < 0, INT_MIN - b, b)


def _attn_kernel(q_ref, qi_ref, wt_ref, kv_ref, kvt_ref, ki_ref, sga_ref, wuv_ref, btile_ref,
                 o_ref, keys_ref, qis_ref, tau_ref, m_ref, mask_ref, acc_ref, sa0_ref, sa1_ref,
                 s2a_ref, s2b_ref, hi_ref, lo_ref, *, t, k_top):
    sa_refs = (sa0_ref, sa1_ref)
    s2_refs = (s2a_ref, s2b_ref)
    j = pl.program_id(1)
    nchunk = j + 1
    npair = (j + 2) // 2
    groups = t // SUBLANES

    def chunk_start(c):
        return pl.multiple_of(c * t, t)

    for h in range(N_IDX_HEADS):
        qis_ref[h * t:(h + 1) * t, :] = qi_ref[0, :, h * D_IDX:(h + 1) * D_IDX]

    def by_parity(c, fn):
        for parity, (cur, nxt) in enumerate(((0, 1), (1, 0))):
            pl.when((c & 1) == parity)(functools.partial(fn, cur, nxt))

    def chunk_loop(n, step):
        def pair_body(cc, carry):
            step(2 * cc, 0, 1)
            step(2 * cc + 1, 1, 0)
            return carry
        lax.fori_loop(0, n // 2, pair_body, 0)
        pl.when(n % 2 == 1)(lambda: step(n - 1, 0, 1))

    def score_matmul(c, dst):
        kc = ki_ref[0, pl.ds(chunk_start(c), t), :]
        dst[...] = _dot_nt(kc, qis_ref[...])

    def score_finish(src):
        acc = jnp.zeros((t, t), F32)
        for h in range(N_IDX_HEADS):
            acc = acc + wt_ref[0, h:h + 1, :] * jnp.maximum(src[:, h * t:(h + 1) * t], 0.0)
        return _sortable(acc)

    def store_keys(c, key):
        keys_ref[pl.ds(chunk_start(c), t), :] = key
        hi_ref[pl.ds(chunk_start(c), t), :] = (key >> 16).astype(I16)
        lo_ref[pl.ds(chunk_start(c), t), :] = ((key & 0xFFFF) - 0x8000).astype(I16)

    def score_step(c, cur, nxt):
        score_matmul(c + 1, sa_refs[nxt])
        store_keys(c, score_finish(sa_refs[cur]))

    score_matmul(0, sa_refs[0])
    chunk_loop(j, score_step)
    sk = lax.broadcasted_iota(I32, (t, t), 0)
    tq = lax.broadcasted_iota(I32, (t, t), 1)

    by_parity(j, lambda cur, nxt: store_keys(
        j, jnp.where(sk <= tq, score_finish(sa_refs[cur]), NEG_KEY)))
    store_keys(j + 1, jnp.full((t, t), NEG_KEY, I32))

    tau_ref[...] = jnp.full((SUBLANES, t), NEG_KEY + 1, I32)

    def packed_rows(ref, cc):
        return ref[2 * t * cc:2 * t * (cc + 1), :].reshape(2 * t // PACKED_ROWS, PACKED_ROWS, t)

    def count16(plane_ref, pred, pairs):
        parts = []
        for cc in range(pairs):
            ones = jnp.where(pred(packed_rows(plane_ref, cc)), jnp.ones((), BF16),
                             jnp.zeros((), BF16))
            parts += [ones[i] for i in range(2 * t // PACKED_ROWS)]
        while len(parts) > 1:
            parts = [sum(parts[i:i + 2][1:], parts[i]) for i in range(0, len(parts), 2)]
        return jnp.sum(parts[0].astype(F32), axis=0, keepdims=True).astype(I32)

    def bisect16(plane_ref, want, n_all, pairs):
        def bit_body(i, carry):
            prefix, n_ge = carry
            cand = prefix | jnp.left_shift(jnp.int32(1), 15 - i)
            cand_b = jnp.broadcast_to((cand - 0x8000).astype(I16), (PACKED_ROWS, t))
            cnt = count16(plane_ref, lambda v: v >= cand_b[None], pairs)
            ok = cnt >= want
            return jnp.where(ok, cand, prefix), jnp.where(ok, cnt, n_ge)
        return lax.fori_loop(0, 16, bit_body, (jnp.zeros((1, t), I32), n_all))

    def count(pred):
        def body(cc, cnt):
            start = pl.multiple_of(cc * (2 * t), 2 * t)
            kk = keys_ref[pl.ds(start, 2 * t), :].reshape(2 * groups, SUBLANES, t)
            return cnt + jnp.sum(jnp.where(pred(kk, cc * (2 * t)), 1, 0), axis=0)
        cnt = lax.fori_loop(0, npair, body, jnp.zeros((SUBLANES, t), I32))
        return jnp.sum(cnt, axis=0, keepdims=True)

    def select_threshold(pairs):
        hi_u, n_ge_hi = bisect16(hi_ref, k_top, jnp.full((1, t), nchunk * t, I32), pairs)
        hi_b = jnp.broadcast_to((hi_u - 0x8000).astype(I16), (PACKED_ROWS, t))
        n_gt_hi = count16(hi_ref, lambda v: v > hi_b[None], pairs)
        for cc in range(pairs):
            lo_ref[2 * t * cc:2 * t * (cc + 1), :] = jnp.where(
                packed_rows(hi_ref, cc) == hi_b[None], packed_rows(lo_ref, cc),
                jnp.full((), -0x8000, I16)).reshape(2 * t, t)
        lo_u, n_ge_lo = bisect16(lo_ref, k_top - n_gt_hi, n_ge_hi - n_gt_hi, pairs)
        tau = (hi_u - 0x8000) * 65536 + lo_u
        n_ge = n_gt_hi + n_ge_lo
        tau_ref[...] = jnp.broadcast_to(jnp.maximum(tau, NEG_KEY + 1), (SUBLANES, t))

        tie = jnp.logical_and(n_ge > k_top, tau > NEG_KEY)

        @pl.when(jnp.max(jnp.where(tie, 1, 0)) > 0)
        def _():
            tau_b = jnp.broadcast_to(tau, (SUBLANES, t))
            n_gt = count(lambda kk, base: kk > tau_b[None])
            want = k_top - n_gt
            row = lax.broadcasted_iota(I32, (2 * groups, SUBLANES, t), 0) * SUBLANES + \
                lax.broadcasted_iota(I32, (2 * groups, SUBLANES, t), 1)

            def pos_body(i, pos):
                cand = pos | jnp.left_shift(jnp.int32(1), 15 - i)
                cand_b = jnp.broadcast_to(cand, (SUBLANES, t))
                cnt = count(lambda kk, base: jnp.logical_and(kk == tau_b[None],
                                                            row + base < cand_b[None]))
                return jnp.where(cnt < want, cand, pos)

            pos = lax.fori_loop(0, 16, pos_body, jnp.zeros((1, t), I32))
            pos_b = jnp.broadcast_to(jnp.where(tie, pos, jnp.int32(2 ** 30)), (SUBLANES, t))

            def demote_body(cc, carry):
                start = pl.multiple_of(cc * (2 * t), 2 * t)
                kk = keys_ref[pl.ds(start, 2 * t), :].reshape(2 * groups, SUBLANES, t)
                drop = jnp.logical_and(kk == tau_b[None], row + cc * (2 * t) > pos_b[None])
                keys_ref[pl.ds(start, 2 * t), :] = jnp.where(drop, NEG_KEY, kk).reshape(2 * t, t)
                return carry

            lax.fori_loop(0, npair, demote_body, 0)

    max_pairs = (keys_ref.shape[0] // t) // 2
    for pairs in range(1, max_pairs + 1):
        pl.when(jnp.logical_and(npair == pairs, nchunk * t > k_top))(
            functools.partial(select_threshold, pairs))

    m_ref[...] = jnp.full(m_ref.shape, -jnp.inf, F32)
    acc_ref[...] = jnp.zeros(acc_ref.shape, F32)
    tau_sel = tau_ref[...]
    ones_rows = jnp.ones((PACKED_ROWS, t), BF16)

    def logits_matmul(c, dst):
        kvc = kv_ref[0, pl.ds(chunk_start(c), t), :]
        for h in range(N_HEADS):
            dst[h] = _dot_nt(kvc, q_ref[0, :, h * KV_LORA:(h + 1) * KV_LORA])

    def attend(c, src, bias_tile):
        kvtc = jnp.concatenate([kvt_ref[0, :, pl.ds(chunk_start(c), t)], ones_rows], axis=0)
        kk = keys_ref[pl.ds(chunk_start(c), t), :].reshape(groups, SUBLANES, t)
        mask_ref[...] = jnp.where(kk >= tau_sel[None], 0.0, -jnp.inf).reshape(t, t)
        for h in range(N_HEADS):
            s = src[h] + mask_ref[...]
            if bias_tile is not None:
                s = s + btile_ref[bias_tile, h]
            m_old = m_ref[h]
            m_new = jnp.maximum(m_old, jnp.max(s, axis=0, keepdims=True))
            m_safe = jnp.where(m_new == -jnp.inf, 0.0, m_new)
            alpha = jnp.exp2(m_old - m_safe)
            p = jnp.exp2(s - m_safe)
            m_ref[h] = m_new
            acc_ref[h] = alpha * acc_ref[h] + _dot(kvtc, p.astype(BF16))

    def attend_step(c, cur, nxt, bias_tile):
        logits_matmul(c + 1, s2_refs[nxt])
        attend(c, s2_refs[cur], bias_tile)

    logits_matmul(0, s2_refs[0])
    chunk_loop(jnp.maximum(j - 1, 0), functools.partial(attend_step, bias_tile=None))
    pl.when(j >= 1)(lambda: by_parity(
        j - 1, lambda cur, nxt: attend_step(j - 1, cur, nxt, bias_tile=1)))
    by_parity(j, lambda cur, nxt: attend(j, s2_refs[cur], 0))

    ys = []
    for h in range(N_HEADS):
        denom = acc_ref[h, KV_LORA:KV_LORA + 1, :]
        o_t = (acc_ref[h, 0:KV_LORA, :] * (1.0 / denom)).astype(BF16)
        ys.append(_dot_tn(o_t, wuv_ref[h]))
    y = jnp.concatenate(ys, axis=-1)
    o_ref[0] = (y * sga_ref[0]).astype(BF16)


def _sparse_attention(q, qi, wt, kv, kvt, ki, sga, w_uv, btiles, *, t):
    bsz, s, _ = q.shape
    k_top = min(TOPK_MAX, s // 4)
    assert (s + t) // PACKED_ROWS <= 256, "per-position key counts must stay exact in bf16"
    blk = lambda w: pl.BlockSpec((1, t, w), lambda b, j: (b, j, 0))
    in_specs = [
        blk(N_HEADS * KV_LORA),
        blk(N_IDX_HEADS * D_IDX),
        pl.BlockSpec((1, N_IDX_HEADS, t), lambda b, j: (b, 0, j)),
        pl.BlockSpec((1, s, KV_LORA), lambda b, j: (b, 0, 0)),
        pl.BlockSpec((1, KV_LORA, s), lambda b, j: (b, 0, 0)),
        pl.BlockSpec((1, s, D_IDX), lambda b, j: (b, 0, 0)),
        blk(D_ATTN),
        pl.BlockSpec(w_uv.shape, lambda b, j: (0, 0, 0)),
        pl.BlockSpec(btiles.shape, lambda b, j: (0, 0, 0, 0)),
    ]
    return pl.pallas_call(
        functools.partial(_attn_kernel, t=t, k_top=k_top),
        name="sparse_attention",
        out_shape=jax.ShapeDtypeStruct((bsz, s, D_ATTN), BF16),
        grid=(bsz, s // t),
        in_specs=in_specs,
        out_specs=blk(D_ATTN),
        scratch_shapes=[
            pltpu.VMEM((s + t, t), I32),
            pltpu.VMEM((N_IDX_HEADS * t, D_IDX), BF16),
            pltpu.VMEM((SUBLANES, t), I32),
            pltpu.VMEM((N_HEADS, 1, t), F32),
            pltpu.VMEM((t, t), F32),
            pltpu.VMEM((N_HEADS, KV_LORA + PACKED_ROWS, t), F32),
            pltpu.VMEM((t, N_IDX_HEADS * t), F32),
            pltpu.VMEM((t, N_IDX_HEADS * t), F32),
            pltpu.VMEM((N_HEADS, t, t), F32),
            pltpu.VMEM((N_HEADS, t, t), F32),
            pltpu.VMEM((s + t, t), I16),
            pltpu.VMEM((s + t, t), I16),
        ],
        compiler_params=pltpu.CompilerParams(
            dimension_semantics=("arbitrary", "arbitrary"), vmem_limit_bytes=VMEM_LIMIT),
    )(q, qi, wt, kv, kvt, ki, sga, w_uv, btiles)


def _outproj_kernel(x_ref, yc_ref, ya_ref, wo_ref, mod_ref, gpost_ref, o_ref, *, d_model):
    y = _dot(jnp.concatenate([yc_ref[0], ya_ref[0]], axis=-1), wo_ref[...])
    yn = (y * lax.rsqrt(jnp.mean(y * y, axis=-1, keepdims=True) + EPS)) * gpost_ref[...]
    gate = mod_ref[0, :, 2 * d_model:3 * d_model]
    o_ref[0] = x_ref[0] + gate * yn


def _out_projection(x, yc, ya, w_out, mod_l, g_post, *, tm):
    bsz, s, d = x.shape
    return pl.pallas_call(
        functools.partial(_outproj_kernel, d_model=d),
        name="out_projection",
        out_shape=jax.ShapeDtypeStruct((bsz, s, d), F32),
        grid=(bsz, s // tm),
        in_specs=[
            pl.BlockSpec((1, tm, d), lambda b, j: (b, j, 0)),
            pl.BlockSpec((1, tm, D_CONV), lambda b, j: (b, j, 0)),
            pl.BlockSpec((1, tm, D_ATTN), lambda b, j: (b, j, 0)),
            pl.BlockSpec(w_out.shape, lambda b, j: (0, 0)),
            pl.BlockSpec((1, 1, 3 * d), lambda b, j: (b, 0, 0)),
            pl.BlockSpec((1, d), lambda b, j: (0, 0)),
        ],
        out_specs=pl.BlockSpec((1, tm, d), lambda b, j: (b, j, 0)),
        compiler_params=pltpu.CompilerParams(
            dimension_semantics=("arbitrary", "arbitrary"), vmem_limit_bytes=VMEM_LIMIT),
    )(x, yc, ya, w_out, mod_l, g_post.reshape(1, d))


def _pack_w_in(w_in_l):
    d = w_in_l.shape[0]
    split = OFF_KW + D_IDX + N_IDX_HEADS
    return jnp.concatenate(
        [w_in_l[:, :split], jnp.zeros((d, KW_PAD), w_in_l.dtype), w_in_l[:, split:]],
        axis=-1).astype(BF16)


def kernel(x, c, w_ada, b_ada, g_pre, w_in, conv_w, conv_b, conv_ln_g, conv_ln_b, w_pw2, q_norm_g,
           w_uq, w_qidx, kv_norm_g, w_uv, rel_bias, w_out, g_post):
    depth = w_ada.shape[0]
    bsz, s, d = x.shape
    tm = min(512, s)
    t = min(ATTN_BLOCK, s)
    assert s % tm == 0 and s % t == 0 and d % LANES == 0
    assert tm % INPROJ_SUB == 0 and INPROJ_SUB % CONV_ROWS == 0

    mod = _modulation(c, w_ada, b_ada)
    btiles = _bias_tiles(rel_bias, t)
    for l in range(depth):
        mod_l = mod[l].reshape(bsz, 1, 3 * d)
        yc, q, qi, kv, kvt, ki, wt, sga = _in_projection(
            x, mod_l, g_pre[l], _pack_w_in(w_in[l]), conv_w[l], conv_b[l], conv_ln_g[l],
            conv_ln_b[l], w_pw2[l].astype(BF16), q_norm_g[l], w_uq[l].astype(BF16),
            w_qidx[l].astype(BF16), kv_norm_g[l], tm=tm)
        ya = _sparse_attention(q, qi, wt, kv, kvt, ki, sga, w_uv[l].astype(BF16), btiles, t=t)
        x = _out_projection(x, yc, ya, w_out[l].astype(BF16), mod_l, g_post[l], tm=tm)
    return x
```

```python
import functools
import math

import jax
import jax.numpy as jnp
from jax import lax
from jax.experimental import pallas as pl
from jax.experimental.pallas import tpu as pltpu

F32 = jnp.float32
BF16 = jnp.bfloat16
I32 = jnp.int32
I16 = jnp.int16

D_CONV = 512
CONV_WIDTH = 31
N_HEADS = 8
D_HEAD_OUT = 64
D_ATTN = N_HEADS * D_HEAD_OUT
Q_LORA = 256
KV_LORA = 128
N_IDX_HEADS = 8
D_IDX = 64
TOPK_MAX = 256
N_BUCKETS = 32
MAX_DISTANCE = 128
EPS = 1e-6

KW_PAD = 128 - D_IDX - N_IDX_HEADS
OFF_UVAL = 0
OFF_UGATE = OFF_UVAL + D_CONV
OFF_GCONV = OFF_UGATE + D_CONV
OFF_CQ = OFF_GCONV + D_CONV
OFF_CKV = OFF_CQ + Q_LORA
OFF_KW = OFF_CKV + KV_LORA
OFF_GATTN = OFF_KW + 128
D_IN_PACKED = OFF_GATTN + D_ATTN

HALO = 32
CONV_ROWS = 64
INPROJ_SUB = 256
ATTN_BLOCK = 256
PV_DELAY = 2
SCORE_ROWS = 32
LANES = 128
SUBLANES = 8
PACKED_ROWS = 2 * SUBLANES
VMEM_LIMIT = 48 * 1024 * 1024

INT_MIN = -(2 ** 31)
NEG_KEY = -0x7F800000
LOG2E = math.log2(math.e)


def _dot(a, b):
    return jnp.dot(a, b, preferred_element_type=F32)


def _dot_nt(a, b):
    return lax.dot_general(a, b, (((1,), (1,)), ((), ())), preferred_element_type=F32)


def _dot_tn(a, b):
    return lax.dot_general(a, b, (((0,), (0,)), ((), ())), preferred_element_type=F32)


def _silu(v):
    return v * jax.nn.sigmoid(v)


def _mod_kernel(c_ref, w_ref, b_ref, o_ref):
    c = c_ref[...]
    o_ref[0] = _dot(_silu(c).astype(BF16), w_ref[0].astype(BF16)) + b_ref[0]


def _modulation(c, w_ada, b_ada):
    depth, d, d3 = w_ada.shape
    bsz = c.shape[0]
    tn = 1024
    return pl.pallas_call(
        _mod_kernel,
        name="adaln_mod",
        out_shape=jax.ShapeDtypeStruct((depth, bsz, d3), F32),
        grid=(depth, d3 // tn),
        in_specs=[
            pl.BlockSpec((bsz, d), lambda l, n: (0, 0)),
            pl.BlockSpec((1, d, tn), lambda l, n: (l, 0, n)),
            pl.BlockSpec((1, 1, tn), lambda l, n: (l, 0, n)),
        ],
        out_specs=pl.BlockSpec((1, bsz, tn), lambda l, n: (l, 0, n)),
        compiler_params=pltpu.CompilerParams(
            dimension_semantics=("arbitrary", "arbitrary"), vmem_limit_bytes=VMEM_LIMIT),
    )(c, w_ada, b_ada.reshape(depth, 1, d3))


def _t5_bucket(n):
    max_exact = N_BUCKETS // 2
    n = jnp.maximum(n, 0)
    nf = jnp.maximum(n, 1).astype(F32)
    large = max_exact + (jnp.log(nf / max_exact) / math.log(MAX_DISTANCE / max_exact)
                         * (N_BUCKETS - max_exact)).astype(I32)
    large = jnp.minimum(large, N_BUCKETS - 1)
    return jnp.where(n < max_exact, n, large)


def _bias_kernel(rb_ref, o_ref, *, t):
    sk = lax.broadcasted_iota(I32, (t, t), 0)
    tq = lax.broadcasted_iota(I32, (t, t), 1)
    for kind in range(2):
        bucket = _t5_bucket(tq - sk + kind * t)
        for h in range(N_HEADS):
            val = jnp.zeros((t, t), F32)
            for b in range(N_BUCKETS):
                val = jnp.where(bucket == b, rb_ref[b, h], val)
            o_ref[kind, h] = (val - rb_ref[N_BUCKETS - 1, h]) * LOG2E


def _bias_tiles(rel_bias, t):
    return pl.pallas_call(
        functools.partial(_bias_kernel, t=t),
        name="t5_bias_tiles",
        out_shape=jax.ShapeDtypeStruct((2, N_HEADS, t, t), F32),
        in_specs=[pl.BlockSpec(memory_space=pltpu.SMEM)],
        out_specs=pl.BlockSpec(memory_space=pltpu.VMEM),
    )(rel_bias)


def _inproj_kernel(x_ref, mod_ref, gpre_ref, w_ref, cw_ref, cb_ref, lng_ref, lnb_ref, wpw2_ref,
                   qg_ref, wuq_ref, wqi_ref, kvg_ref,
                   yc_ref, q_ref, qi_ref, kv_ref, kvt_ref, ki_ref, wt_ref, sga_ref,
                   hbuf, abuf, ybuf, shbuf, gcbuf, *, tm, d_model):
    j = pl.program_id(1)

    @pl.when(j == 0)
    def _():
        abuf[0:HALO, :] = jnp.zeros((HALO, D_CONV), F32)

    shift = mod_ref[0, :, 0:d_model]
    scale = mod_ref[0, :, d_model:2 * d_model]
    first = HALO - (CONV_WIDTH - 1)

    for base in range(0, tm, INPROJ_SUB):
        rows = slice(base, base + INPROJ_SUB)
        x = x_ref[0, rows, :]
        ms = jnp.mean(x * x, axis=-1, keepdims=True)
        xn = (x * lax.rsqrt(ms + EPS)) * gpre_ref[...]
        hbuf[rows, :] = (xn * (1.0 + scale) + shift).astype(BF16)
        h = hbuf[rows, :]

        u_val = _dot(h, w_ref[:, OFF_UVAL:OFF_UVAL + D_CONV])
        u_gate = _dot(h, w_ref[:, OFF_UGATE:OFF_UGATE + D_CONV])
        abuf[HALO + base:HALO + base + INPROJ_SUB, :] = u_val * jax.nn.sigmoid(u_gate)

        span = slice(base, base + INPROJ_SUB + HALO - SUBLANES)
        for r in range(1, SUBLANES):
            shbuf[r - 1, span, :] = abuf[base + r:base + r + INPROJ_SUB + HALO - SUBLANES, :]

        gcbuf[rows, :] = _silu(_dot(h, w_ref[:, OFF_GCONV:OFF_GCONV + D_CONV]))

        c_q = _dot(h, w_ref[:, OFF_CQ:OFF_CQ + Q_LORA])
        cq = (c_q * lax.rsqrt(jnp.mean(c_q * c_q, axis=-1, keepdims=True) + EPS)) * qg_ref[...]
        cq = cq.astype(BF16)
        q_ref[0, rows, :] = (_dot(cq, wuq_ref[...]) * (KV_LORA ** -0.5 * LOG2E)).astype(BF16)
        qi_ref[0, rows, :] = _dot(cq, wqi_ref[...]).astype(BF16)

        c_kv = _dot(h, w_ref[:, OFF_CKV:OFF_CKV + KV_LORA])
        kvn = (c_kv * lax.rsqrt(jnp.mean(c_kv * c_kv, axis=-1, keepdims=True) + EPS)) * kvg_ref[...]
        kv_ref[0, rows, :] = kvn.astype(BF16)
        kvt_ref[0, :, rows] = jnp.transpose(kvn).astype(BF16)

        kw = _dot(h, w_ref[:, OFF_KW:OFF_KW + 128])
        ki_ref[0, rows, :] = kw[:, 0:D_IDX].astype(BF16)
        kwt = jnp.transpose(kw)
        wt_ref[0, :, rows] = kwt[D_IDX:D_IDX + N_IDX_HEADS, :] * (N_IDX_HEADS ** -0.5 * D_IDX ** -0.5)

        sga_ref[0, rows, :] = _silu(_dot(h, w_ref[:, OFF_GATTN:OFF_GATTN + D_ATTN]))

    for base in range(0, tm, INPROJ_SUB):
        rows = slice(base, base + INPROJ_SUB)
        for r0 in range(base, base + INPROJ_SUB, CONV_ROWS):
            acc = jnp.broadcast_to(cb_ref[...], (CONV_ROWS, D_CONV))
            for k in range(CONV_WIDTH):
                q8, r = divmod(first + k, SUBLANES)
                taps = slice(r0 + q8 * SUBLANES, r0 + q8 * SUBLANES + CONV_ROWS)
                tap = abuf[taps, :] if r == 0 else shbuf[r - 1, taps, :]
                acc = acc + cw_ref[k:k + 1, :] * tap
            mu = jnp.mean(acc, axis=-1, keepdims=True)
            cen = acc - mu
            var = jnp.mean(cen * cen, axis=-1, keepdims=True)
            yn = cen * lax.rsqrt(var + EPS) * lng_ref[...] + lnb_ref[...]
            ybuf[r0:r0 + CONV_ROWS, :] = _silu(yn).astype(BF16)

        yc_ref[0, rows, :] = (_dot(ybuf[rows, :], wpw2_ref[...]) * gcbuf[rows, :]).astype(BF16)

    abuf[0:HALO, :] = abuf[tm:tm + HALO, :]


def _in_projection(x, mod_l, g_pre, w_packed, conv_w, conv_b, ln_g, ln_b, w_pw2, q_g, w_uq, w_qi,
                   kv_g, *, tm):
    bsz, s, d = x.shape
    row = lambda v: v.reshape(1, -1)
    full = lambda a: pl.BlockSpec(a.shape, lambda b, j: (0,) * a.ndim)
    args = (x, mod_l, row(g_pre), w_packed, conv_w, row(conv_b), row(ln_g), row(ln_b), w_pw2,
            row(q_g), w_uq, w_qi, row(kv_g))
    in_specs = [
        pl.BlockSpec((1, tm, d), lambda b, j: (b, j, 0)),
        pl.BlockSpec((1, 1, 3 * d), lambda b, j: (b, 0, 0)),
    ] + [full(a) for a in args[2:]]
    blk = lambda w: pl.BlockSpec((1, tm, w), lambda b, j: (b, j, 0))
    out_shape = (
        jax.ShapeDtypeStruct((bsz, s, D_CONV), BF16),
        jax.ShapeDtypeStruct((bsz, s, N_HEADS * KV_LORA), BF16),
        jax.ShapeDtypeStruct((bsz, s, N_IDX_HEADS * D_IDX), BF16),
        jax.ShapeDtypeStruct((bsz, s, KV_LORA), BF16),
        jax.ShapeDtypeStruct((bsz, KV_LORA, s), BF16),
        jax.ShapeDtypeStruct((bsz, s, D_IDX), BF16),
        jax.ShapeDtypeStruct((bsz, N_IDX_HEADS, s), F32),
        jax.ShapeDtypeStruct((bsz, s, D_ATTN), F32),
    )
    out_specs = (
        blk(D_CONV), blk(N_HEADS * KV_LORA), blk(N_IDX_HEADS * D_IDX), blk(KV_LORA),
        pl.BlockSpec((1, KV_LORA, tm), lambda b, j: (b, 0, j)),
        blk(D_IDX),
        pl.BlockSpec((1, N_IDX_HEADS, tm), lambda b, j: (b, 0, j)),
        blk(D_ATTN),
    )
    return pl.pallas_call(
        functools.partial(_inproj_kernel, tm=tm, d_model=d),
        name="in_projection",
        out_shape=out_shape,
        grid=(bsz, s // tm),
        in_specs=in_specs,
        out_specs=out_specs,
        scratch_shapes=[
            pltpu.VMEM((tm, d), BF16),
            pltpu.VMEM((tm + HALO, D_CONV), F32),
            pltpu.VMEM((tm, D_CONV), BF16),
            pltpu.VMEM((SUBLANES - 1, tm + HALO, D_CONV), F32),
            pltpu.VMEM((tm, D_CONV), F32),
        ],
        compiler_params=pltpu.CompilerParams(
            dimension_semantics=("arbitrary", "arbitrary"), vmem_limit_bytes=VMEM_LIMIT),
    )(*args)


def _sortable(v):
    b = pltpu.bitcast(v, I32)
    return jnp.where(b < 0, INT_MIN - b, b)


def _attn_kernel(q_ref, qi_ref, wt_ref, kv_ref, kvt_ref, ki_ref, sga_ref, wuv_ref, btile_ref,
                 o_ref, keys_ref, qis_ref, tau_ref, m_ref, mask_ref, acc_ref, sa0_ref, sa1_ref,
                 s2a_ref, s2b_ref, hi_ref, lo_ref, cmaxa_ref, cmaxb_ref, *, t, k_top):
    sa_refs = (sa0_ref, sa1_ref)
    s2_refs = (s2a_ref, s2b_ref)
    cmax_refs = (cmaxa_ref, cmaxb_ref)
    j = pl.program_id(1)
    nchunk = j + 1
    npair = (j + 2) // 2
    groups = t // SUBLANES

    def chunk_start(c):
        return pl.multiple_of(c * t, t)

    for h in range(N_IDX_HEADS):
        qis_ref[h * t:(h + 1) * t, :] = qi_ref[0, :, h * D_IDX:(h + 1) * D_IDX]

    def by_parity(c, fn):
        for parity, (cur, nxt) in enumerate(((0, 1), (1, 0))):
            pl.when((c & 1) == parity)(functools.partial(fn, cur, nxt))

    def chunk_loop(n, step):
        def pair_body(cc, carry):
            step(2 * cc, 0, 1)
            step(2 * cc + 1, 1, 0)
            return carry
        lax.fori_loop(0, n // 2, pair_body, 0)
        pl.when(n % 2 == 1)(lambda: step(n - 1, 0, 1))

    def score_matmul(c, dst):
        kc = ki_ref[0, pl.ds(chunk_start(c), t), :]
        dst[...] = _dot_nt(kc, qis_ref[...])

    def score_finish(src):
        blocks = []
        for r0 in range(0, t, SCORE_ROWS):
            acc = jnp.zeros((SCORE_ROWS, t), F32)
            for h in range(N_IDX_HEADS):
                s = src[r0:r0 + SCORE_ROWS, h * t:(h + 1) * t]
                acc = acc + wt_ref[0, h:h + 1, :] * jnp.maximum(s, 0.0)
            blocks.append(_sortable(acc))
        return jnp.concatenate(blocks, axis=0)

    def store_keys(c, key):
        keys_ref[pl.ds(chunk_start(c), t), :] = key
        hi_ref[pl.ds(chunk_start(c), t), :] = (key >> 16).astype(I16)
        lo_ref[pl.ds(chunk_start(c), t), :] = ((key & 0xFFFF) - 0x8000).astype(I16)

    def score_step(c, cur, nxt):
        score_matmul(c + 1, sa_refs[nxt])
        store_keys(c, score_finish(sa_refs[cur]))

    score_matmul(0, sa_refs[0])
    chunk_loop(j, score_step)
    sk = lax.broadcasted_iota(I32, (t, t), 0)
    tq = lax.broadcasted_iota(I32, (t, t), 1)

    by_parity(j, lambda cur, nxt: store_keys(
        j, jnp.where(sk <= tq, score_finish(sa_refs[cur]), NEG_KEY)))
    store_keys(j + 1, jnp.full((t, t), NEG_KEY, I32))

    tau_ref[...] = jnp.full((SUBLANES, t), NEG_KEY + 1, I32)

    def packed_rows(ref, cc):
        return ref[2 * t * cc:2 * t * (cc + 1), :].reshape(2 * t // PACKED_ROWS, PACKED_ROWS, t)

    def count16(plane_ref, pred, pairs):
        parts = []
        for cc in range(pairs):
            ones = jnp.where(pred(packed_rows(plane_ref, cc)), jnp.ones((), BF16),
                             jnp.zeros((), BF16))
            parts += [ones[i] for i in range(2 * t // PACKED_ROWS)]
        while len(parts) > 1:
            parts = [sum(parts[i:i + 2][1:], parts[i]) for i in range(0, len(parts), 2)]
        return jnp.sum(parts[0].astype(F32), axis=0, keepdims=True).astype(I32)

    def bisect16(plane_ref, want, n_all, pairs):
        def bit_body(i, carry):
            prefix, n_ge = carry
            cand = prefix | jnp.left_shift(jnp.int32(1), 15 - i)
            cand_b = jnp.broadcast_to((cand - 0x8000).astype(I16), (PACKED_ROWS, t))
            cnt = count16(plane_ref, lambda v: v >= cand_b[None], pairs)
            ok = cnt >= want
            return jnp.where(ok, cand, prefix), jnp.where(ok, cnt, n_ge)
        return lax.fori_loop(0, 16, bit_body, (jnp.zeros((1, t), I32), n_all))

    def count(pred):
        def body(cc, cnt):
            start = pl.multiple_of(cc * (2 * t), 2 * t)
            kk = keys_ref[pl.ds(start, 2 * t), :].reshape(2 * groups, SUBLANES, t)
            return cnt + jnp.sum(jnp.where(pred(kk, cc * (2 * t)), 1, 0), axis=0)
        cnt = lax.fori_loop(0, npair, body, jnp.zeros((SUBLANES, t), I32))
        return jnp.sum(cnt, axis=0, keepdims=True)

    def select_threshold(pairs):
        hi_u, n_ge_hi = bisect16(hi_ref, k_top, jnp.full((1, t), nchunk * t, I32), pairs)
        hi_b = jnp.broadcast_to((hi_u - 0x8000).astype(I16), (PACKED_ROWS, t))
        n_gt_hi = count16(hi_ref, lambda v: v > hi_b[None], pairs)
        for cc in range(pairs):
            lo_ref[2 * t * cc:2 * t * (cc + 1), :] = jnp.where(
                packed_rows(hi_ref, cc) == hi_b[None], packed_rows(lo_ref, cc),
                jnp.full((), -0x8000, I16)).reshape(2 * t, t)
        lo_u, n_ge_lo = bisect16(lo_ref, k_top - n_gt_hi, n_ge_hi - n_gt_hi, pairs)
        tau = (hi_u - 0x8000) * 65536 + lo_u
        n_ge = n_gt_hi + n_ge_lo
        tau_ref[...] = jnp.broadcast_to(jnp.maximum(tau, NEG_KEY + 1), (SUBLANES, t))

        tie = jnp.logical_and(n_ge > k_top, tau > NEG_KEY)

        @pl.when(jnp.max(jnp.where(tie, 1, 0)) > 0)
        def _():
            tau_b = jnp.broadcast_to(tau, (SUBLANES, t))
            n_gt = count(lambda kk, base: kk > tau_b[None])
            want = k_top - n_gt
            row = lax.broadcasted_iota(I32, (2 * groups, SUBLANES, t), 0) * SUBLANES + \
                lax.broadcasted_iota(I32, (2 * groups, SUBLANES, t), 1)

            def pos_body(i, pos):
                cand = pos | jnp.left_shift(jnp.int32(1), 15 - i)
                cand_b = jnp.broadcast_to(cand, (SUBLANES, t))
                cnt = count(lambda kk, base: jnp.logical_and(kk == tau_b[None],
                                                            row + base < cand_b[None]))
                return jnp.where(cnt < want, cand, pos)

            pos = lax.fori_loop(0, 16, pos_body, jnp.zeros((1, t), I32))
            pos_b = jnp.broadcast_to(jnp.where(tie, pos, jnp.int32(2 ** 30)), (SUBLANES, t))

            def demote_body(cc, carry):
                start = pl.multiple_of(cc * (2 * t), 2 * t)
                kk = keys_ref[pl.ds(start, 2 * t), :].reshape(2 * groups, SUBLANES, t)
                drop = jnp.logical_and(kk == tau_b[None], row + cc * (2 * t) > pos_b[None])
                keys_ref[pl.ds(start, 2 * t), :] = jnp.where(drop, NEG_KEY, kk).reshape(2 * t, t)
                return carry

            lax.fori_loop(0, npair, demote_body, 0)

    max_pairs = (keys_ref.shape[0] // t) // 2
    for pairs in range(1, max_pairs + 1):
        pl.when(jnp.logical_and(npair == pairs, nchunk * t > k_top))(
            functools.partial(select_threshold, pairs))

    m_ref[...] = jnp.full(m_ref.shape, -jnp.inf, F32)
    acc_ref[...] = jnp.zeros(acc_ref.shape, F32)
    tau_sel = tau_ref[...]
    ones_rows = jnp.ones((PACKED_ROWS, t), BF16)

    def set_mask(c):
        kk = keys_ref[pl.ds(chunk_start(c), t), :].reshape(groups, SUBLANES, t)
        mask_ref[...] = jnp.where(kk >= tau_sel[None], 0.0, -jnp.inf).reshape(t, t)

    def logits_matmul(c, slot, h):
        kvc = kv_ref[0, pl.ds(chunk_start(c), t), :]
        s = _dot_nt(kvc, q_ref[0, :, h * KV_LORA:(h + 1) * KV_LORA]) + mask_ref[...]
        s2_refs[slot][h] = s
        cmax_refs[slot][h] = jnp.max(s.reshape(groups, SUBLANES, t), axis=0)

    def attend_step(c, cur, nxt, bias_tile):
        kvtc = jnp.concatenate([kvt_ref[0, :, pl.ds(chunk_start(c), t)], ones_rows], axis=0)
        if nxt is not None:
            set_mask(c + 1)
        pending = []

        def flush():
            h, alpha, pv = pending.pop(0)
            acc_ref[h] = alpha * acc_ref[h] + pv

        for h in range(N_HEADS):
            if nxt is not None:
                logits_matmul(c + 1, nxt, h)
            s = s2_refs[cur][h]
            if bias_tile is None:
                cmax = jnp.max(cmax_refs[cur][h], axis=0, keepdims=True)
            else:
                s = s + btile_ref[bias_tile, h]
                cmax = jnp.max(s, axis=0, keepdims=True)
            m_old = m_ref[h]
            m_new = jnp.maximum(m_old, cmax)
            m_safe = jnp.where(m_new == -jnp.inf, 0.0, m_new)
            alpha = jnp.exp2(m_old - m_safe)
            p = jnp.exp2(s - m_safe)
            m_ref[h] = m_new
            pending.append((h, alpha, _dot(kvtc, p.astype(BF16))))
            if len(pending) > PV_DELAY:
                flush()
        while pending:
            flush()

    set_mask(0)
    for h in range(N_HEADS):
        logits_matmul(0, 0, h)
    chunk_loop(jnp.maximum(j - 1, 0), functools.partial(attend_step, bias_tile=None))
    pl.when(j >= 1)(lambda: by_parity(
        j - 1, lambda cur, nxt: attend_step(j - 1, cur, nxt, bias_tile=1)))
    by_parity(j, lambda cur, nxt: attend_step(j, cur, None, bias_tile=0))

    ys = []
    for h in range(N_HEADS):
        denom = acc_ref[h, KV_LORA:KV_LORA + 1, :]
        o_t = (acc_ref[h, 0:KV_LORA, :] * (1.0 / denom)).astype(BF16)
        ys.append(_dot_tn(o_t, wuv_ref[h]))
    y = jnp.concatenate(ys, axis=-1)
    o_ref[0] = (y * sga_ref[0]).astype(BF16)


def _sparse_attention(q, qi, wt, kv, kvt, ki, sga, w_uv, btiles, *, t):
    bsz, s, _ = q.shape
    k_top = min(TOPK_MAX, s // 4)
    assert (s + t) // PACKED_ROWS <= 256, "per-position key counts must stay exact in bf16"
    blk = lambda w: pl.BlockSpec((1, t, w), lambda b, j: (b, j, 0))
    in_specs = [
        blk(N_HEADS * KV_LORA),
        blk(N_IDX_HEADS * D_IDX),
        pl.BlockSpec((1, N_IDX_HEADS, t), lambda b, j: (b, 0, j)),
        pl.BlockSpec((1, s, KV_LORA), lambda b, j: (b, 0, 0)),
        pl.BlockSpec((1, KV_LORA, s), lambda b, j: (b, 0, 0)),
        pl.BlockSpec((1, s, D_IDX), lambda b, j: (b, 0, 0)),
        blk(D_ATTN),
        pl.BlockSpec(w_uv.shape, lambda b, j: (0, 0, 0)),
        pl.BlockSpec(btiles.shape, lambda b, j: (0, 0, 0, 0)),
    ]
    return pl.pallas_call(
        functools.partial(_attn_kernel, t=t, k_top=k_top),
        name="sparse_attention",
        out_shape=jax.ShapeDtypeStruct((bsz, s, D_ATTN), BF16),
        grid=(bsz, s // t),
        in_specs=in_specs,
        out_specs=blk(D_ATTN),
        scratch_shapes=[
            pltpu.VMEM((s + t, t), I32),
            pltpu.VMEM((N_IDX_HEADS * t, D_IDX), BF16),
            pltpu.VMEM((SUBLANES, t), I32),
            pltpu.VMEM((N_HEADS, 1, t), F32),
            pltpu.VMEM((t, t), F32),
            pltpu.VMEM((N_HEADS, KV_LORA + PACKED_ROWS, t), F32),
            pltpu.VMEM((t, N_IDX_HEADS * t), F32),
            pltpu.VMEM((t, N_IDX_HEADS * t), F32),
            pltpu.VMEM((N_HEADS, t, t), F32),
            pltpu.VMEM((N_HEADS, t, t), F32),
            pltpu.VMEM((s + t, t), I16),
            pltpu.VMEM((s + t, t), I16),
            pltpu.VMEM((N_HEADS, SUBLANES, t), F32),
            pltpu.VMEM((N_HEADS, SUBLANES, t), F32),
        ],
        compiler_params=pltpu.CompilerParams(
            dimension_semantics=("arbitrary", "arbitrary"), vmem_limit_bytes=VMEM_LIMIT),
    )(q, qi, wt, kv, kvt, ki, sga, w_uv, btiles)


def _outproj_kernel(x_ref, yc_ref, ya_ref, wo_ref, mod_ref, gpost_ref, o_ref, *, d_model):
    y = _dot(jnp.concatenate([yc_ref[0], ya_ref[0]], axis=-1), wo_ref[...])
    yn = (y * lax.rsqrt(jnp.mean(y * y, axis=-1, keepdims=True) + EPS)) * gpost_ref[...]
    gate = mod_ref[0, :, 2 * d_model:3 * d_model]
    o_ref[0] = x_ref[0] + gate * yn


def _out_projection(x, yc, ya, w_out, mod_l, g_post, *, tm):
    bsz, s, d = x.shape
    return pl.pallas_call(
        functools.partial(_outproj_kernel, d_model=d),
        name="out_projection",
        out_shape=jax.ShapeDtypeStruct((bsz, s, d), F32),
        grid=(bsz, s // tm),
        in_specs=[
            pl.BlockSpec((1, tm, d), lambda b, j: (b, j, 0)),
            pl.BlockSpec((1, tm, D_CONV), lambda b, j: (b, j, 0)),
            pl.BlockSpec((1, tm, D_ATTN), lambda b, j: (b, j, 0)),
            pl.BlockSpec(w_out.shape, lambda b, j: (0, 0)),
            pl.BlockSpec((1, 1, 3 * d), lambda b, j: (b, 0, 0)),
            pl.BlockSpec((1, d), lambda b, j: (0, 0)),
        ],
        out_specs=pl.BlockSpec((1, tm, d), lambda b, j: (b, j, 0)),
        compiler_params=pltpu.CompilerParams(
            dimension_semantics=("arbitrary", "arbitrary"), vmem_limit_bytes=VMEM_LIMIT),
    )(x, yc, ya, w_out, mod_l, g_post.reshape(1, d))


def _pack_w_in(w_in_l):
    d = w_in_l.shape[0]
    split = OFF_KW + D_IDX + N_IDX_HEADS
    return jnp.concatenate(
        [w_in_l[:, :split], jnp.zeros((d, KW_PAD), w_in_l.dtype), w_in_l[:, split:]],
        axis=-1).astype(BF16)


def kernel(x, c, w_ada, b_ada, g_pre, w_in, conv_w, conv_b, conv_ln_g, conv_ln_b, w_pw2, q_norm_g,
           w_uq, w_qidx, kv_norm_g, w_uv, rel_bias, w_out, g_post):
    depth = w_ada.shape[0]
    bsz, s, d = x.shape
    tm = min(512, s)
    t = min(ATTN_BLOCK, s)
    assert s % tm == 0 and s % t == 0 and d % LANES == 0
    assert tm % INPROJ_SUB == 0 and INPROJ_SUB % CONV_ROWS == 0

    mod = _modulation(c, w_ada, b_ada)
    btiles = _bias_tiles(rel_bias, t)
    for l in range(depth):
        mod_l = mod[l].reshape(bsz, 1, 3 * d)
        yc, q, qi, kv, kvt, ki, wt, sga = _in_projection(
            x, mod_l, g_pre[l], _pack_w_in(w_in[l]), conv_w[l], conv_b[l], conv_ln_g[l],
            conv_ln_b[l], w_pw2[l].astype(BF16), q_norm_g[l], w_uq[l].astype(BF16),
            w_qidx[l].astype(BF16), kv_norm_g[l], tm=tm)
        ya = _sparse_attention(q, qi, wt, kv, kvt, ki, sga, w_uv[l].astype(BF16), btiles, t=t)
        x = _out_projection(x, yc, ya, w_out[l].astype(BF16), mod_l, g_post[l], tm=tm)
    return x
```

```python
import functools
import math

import jax
import jax.numpy as jnp
from jax import lax
from jax.experimental import pallas as pl
from jax.experimental.pallas import tpu as pltpu

F32 = jnp.float32
BF16 = jnp.bfloat16
I32 = jnp.int32
I16 = jnp.int16

D_CONV = 512
CONV_WIDTH = 31
N_HEADS = 8
D_HEAD_OUT = 64
D_ATTN = N_HEADS * D_HEAD_OUT
Q_LORA = 256
KV_LORA = 128
N_IDX_HEADS = 8
D_IDX = 64
TOPK_MAX = 256
N_BUCKETS = 32
MAX_DISTANCE = 128
EPS = 1e-6

KW_PAD = 128 - D_IDX - N_IDX_HEADS
OFF_UVAL = 0
OFF_UGATE = OFF_UVAL + D_CONV
OFF_GCONV = OFF_UGATE + D_CONV
OFF_CQ = OFF_GCONV + D_CONV
OFF_CKV = OFF_CQ + Q_LORA
OFF_KW = OFF_CKV + KV_LORA
OFF_GATTN = OFF_KW + 128
D_IN_PACKED = OFF_GATTN + D_ATTN

HALO = 32
CONV_ROWS = 64
INPROJ_SUB = 256
ATTN_BLOCK = 256
PV_DELAY = 2
SCORE_ROWS = 32
LANES = 128
SUBLANES = 8
PACKED_ROWS = 2 * SUBLANES
VMEM_LIMIT = 48 * 1024 * 1024

INT_MIN = -(2 ** 31)
NEG_KEY = -0x7F800000
LOG2E = math.log2(math.e)


def _dot(a, b):
    return jnp.dot(a, b, preferred_element_type=F32)


def _dot_nt(a, b):
    return lax.dot_general(a, b, (((1,), (1,)), ((), ())), preferred_element_type=F32)


def _dot_tn(a, b):
    return lax.dot_general(a, b, (((0,), (0,)), ((), ())), preferred_element_type=F32)


def _silu(v):
    return v * jax.nn.sigmoid(v)


def _mod_kernel(c_ref, w_ref, b_ref, o_ref):
    c = c_ref[...]
    o_ref[0] = _dot(_silu(c).astype(BF16), w_ref[0].astype(BF16)) + b_ref[0]


def _modulation(c, w_ada, b_ada):
    depth, d, d3 = w_ada.shape
    bsz = c.shape[0]
    tn = 1024
    return pl.pallas_call(
        _mod_kernel,
        name="adaln_mod",
        out_shape=jax.ShapeDtypeStruct((depth, bsz, d3), F32),
        grid=(depth, d3 // tn),
        in_specs=[
            pl.BlockSpec((bsz, d), lambda l, n: (0, 0)),
            pl.BlockSpec((1, d, tn), lambda l, n: (l, 0, n)),
            pl.BlockSpec((1, 1, tn), lambda l, n: (l, 0, n)),
        ],
        out_specs=pl.BlockSpec((1, bsz, tn), lambda l, n: (l, 0, n)),
        compiler_params=pltpu.CompilerParams(
            dimension_semantics=("arbitrary", "arbitrary"), vmem_limit_bytes=VMEM_LIMIT),
    )(c, w_ada, b_ada.reshape(depth, 1, d3))


def _t5_bucket(n):
    max_exact = N_BUCKETS // 2
    n = jnp.maximum(n, 0)
    nf = jnp.maximum(n, 1).astype(F32)
    large = max_exact + (jnp.log(nf / max_exact) / math.log(MAX_DISTANCE / max_exact)
                         * (N_BUCKETS - max_exact)).astype(I32)
    large = jnp.minimum(large, N_BUCKETS - 1)
    return jnp.where(n < max_exact, n, large)


def _bias_kernel(rb_ref, o_ref, *, t):
    sk = lax.broadcasted_iota(I32, (t, t), 0)
    tq = lax.broadcasted_iota(I32, (t, t), 1)
    for kind in range(2):
        bucket = _t5_bucket(tq - sk + kind * t)
        for h in range(N_HEADS):
            val = jnp.zeros((t, t), F32)
            for b in range(N_BUCKETS):
                val = jnp.where(bucket == b, rb_ref[b, h], val)
            o_ref[kind, h] = (val - rb_ref[N_BUCKETS - 1, h]) * LOG2E


def _bias_tiles(rel_bias, t):
    return pl.pallas_call(
        functools.partial(_bias_kernel, t=t),
        name="t5_bias_tiles",
        out_shape=jax.ShapeDtypeStruct((2, N_HEADS, t, t), F32),
        in_specs=[pl.BlockSpec(memory_space=pltpu.SMEM)],
        out_specs=pl.BlockSpec(memory_space=pltpu.VMEM),
    )(rel_bias)


def _inproj_kernel(x_ref, mod_ref, gpre_ref, w_ref, cw_ref, cb_ref, lng_ref, lnb_ref, wpw2_ref,
                   qg_ref, wuq_ref, wqi_ref, kvg_ref,
                   yc_ref, q_ref, qi_ref, kv_ref, kvt_ref, ki_ref, wt_ref, sga_ref,
                   hbuf, abuf, ybuf, shbuf, gcbuf, *, tm, d_model):
    j = pl.program_id(1)

    @pl.when(j == 0)
    def _():
        abuf[0:HALO, :] = jnp.zeros((HALO, D_CONV), F32)

    shift = mod_ref[0, :, 0:d_model]
    scale = mod_ref[0, :, d_model:2 * d_model]
    first = HALO - (CONV_WIDTH - 1)

    for base in range(0, tm, INPROJ_SUB):
        rows = slice(base, base + INPROJ_SUB)
        x = x_ref[0, rows, :]
        ms = jnp.mean(x * x, axis=-1, keepdims=True)
        xn = (x * lax.rsqrt(ms + EPS)) * gpre_ref[...]
        hbuf[rows, :] = (xn * (1.0 + scale) + shift).astype(BF16)
        h = hbuf[rows, :]

        u_val = _dot(h, w_ref[:, OFF_UVAL:OFF_UVAL + D_CONV])
        u_gate = _dot(h, w_ref[:, OFF_UGATE:OFF_UGATE + D_CONV])
        abuf[HALO + base:HALO + base + INPROJ_SUB, :] = u_val * jax.nn.sigmoid(u_gate)

        span = slice(base, base + INPROJ_SUB + HALO - SUBLANES)
        for r in range(1, SUBLANES):
            shbuf[r - 1, span, :] = abuf[base + r:base + r + INPROJ_SUB + HALO - SUBLANES, :]

        gcbuf[rows, :] = _silu(_dot(h, w_ref[:, OFF_GCONV:OFF_GCONV + D_CONV]))

        c_q = _dot(h, w_ref[:, OFF_CQ:OFF_CQ + Q_LORA])
        cq = (c_q * lax.rsqrt(jnp.mean(c_q * c_q, axis=-1, keepdims=True) + EPS)) * qg_ref[...]
        cq = cq.astype(BF16)
        q = _dot(cq, wuq_ref[...]) * (KV_LORA ** -0.5 * LOG2E)
        q_ref[0, :, rows] = jnp.transpose(q).astype(BF16)
        qi_ref[0, :, rows] = jnp.transpose(_dot(cq, wqi_ref[...])).astype(BF16)

        c_kv = _dot(h, w_ref[:, OFF_CKV:OFF_CKV + KV_LORA])
        kvn = (c_kv * lax.rsqrt(jnp.mean(c_kv * c_kv, axis=-1, keepdims=True) + EPS)) * kvg_ref[...]
        kv_ref[0, rows, :] = kvn.astype(BF16)
        kvt_ref[0, :, rows] = jnp.transpose(kvn).astype(BF16)

        kw = _dot(h, w_ref[:, OFF_KW:OFF_KW + 128])
        ki_ref[0, rows, :] = kw[:, 0:D_IDX].astype(BF16)
        kwt = jnp.transpose(kw)
        wt_ref[0, :, rows] = kwt[D_IDX:D_IDX + N_IDX_HEADS, :] * (N_IDX_HEADS ** -0.5 * D_IDX ** -0.5)

        sga_ref[0, rows, :] = _silu(_dot(h, w_ref[:, OFF_GATTN:OFF_GATTN + D_ATTN]))

    for base in range(0, tm, INPROJ_SUB):
        rows = slice(base, base + INPROJ_SUB)
        for r0 in range(base, base + INPROJ_SUB, CONV_ROWS):
            acc = jnp.broadcast_to(cb_ref[...], (CONV_ROWS, D_CONV))
            for k in range(CONV_WIDTH):
                q8, r = divmod(first + k, SUBLANES)
                taps = slice(r0 + q8 * SUBLANES, r0 + q8 * SUBLANES + CONV_ROWS)
                tap = abuf[taps, :] if r == 0 else shbuf[r - 1, taps, :]
                acc = acc + cw_ref[k:k + 1, :] * tap
            mu = jnp.mean(acc, axis=-1, keepdims=True)
            cen = acc - mu
            var = jnp.mean(cen * cen, axis=-1, keepdims=True)
            yn = cen * lax.rsqrt(var + EPS) * lng_ref[...] + lnb_ref[...]
            ybuf[r0:r0 + CONV_ROWS, :] = _silu(yn).astype(BF16)

        yc_ref[0, rows, :] = (_dot(ybuf[rows, :], wpw2_ref[...]) * gcbuf[rows, :]).astype(BF16)

    abuf[0:HALO, :] = abuf[tm:tm + HALO, :]


def _in_projection(x, mod_l, g_pre, w_packed, conv_w, conv_b, ln_g, ln_b, w_pw2, q_g, w_uq, w_qi,
                   kv_g, *, tm):
    bsz, s, d = x.shape
    row = lambda v: v.reshape(1, -1)
    full = lambda a: pl.BlockSpec(a.shape, lambda b, j: (0,) * a.ndim)
    args = (x, mod_l, row(g_pre), w_packed, conv_w, row(conv_b), row(ln_g), row(ln_b), w_pw2,
            row(q_g), w_uq, w_qi, row(kv_g))
    in_specs = [
        pl.BlockSpec((1, tm, d), lambda b, j: (b, j, 0)),
        pl.BlockSpec((1, 1, 3 * d), lambda b, j: (b, 0, 0)),
    ] + [full(a) for a in args[2:]]
    blk = lambda w: pl.BlockSpec((1, tm, w), lambda b, j: (b, j, 0))
    out_shape = (
        jax.ShapeDtypeStruct((bsz, s, D_CONV), BF16),
        jax.ShapeDtypeStruct((bsz, N_HEADS * KV_LORA, s), BF16),
        jax.ShapeDtypeStruct((bsz, N_IDX_HEADS * D_IDX, s), BF16),
        jax.ShapeDtypeStruct((bsz, s, KV_LORA), BF16),
        jax.ShapeDtypeStruct((bsz, KV_LORA, s), BF16),
        jax.ShapeDtypeStruct((bsz, s, D_IDX), BF16),
        jax.ShapeDtypeStruct((bsz, N_IDX_HEADS, s), F32),
        jax.ShapeDtypeStruct((bsz, s, D_ATTN), F32),
    )
    out_specs = (
        blk(D_CONV),
        pl.BlockSpec((1, N_HEADS * KV_LORA, tm), lambda b, j: (b, 0, j)),
        pl.BlockSpec((1, N_IDX_HEADS * D_IDX, tm), lambda b, j: (b, 0, j)),
        blk(KV_LORA),
        pl.BlockSpec((1, KV_LORA, tm), lambda b, j: (b, 0, j)),
        blk(D_IDX),
        pl.BlockSpec((1, N_IDX_HEADS, tm), lambda b, j: (b, 0, j)),
        blk(D_ATTN),
    )
    return pl.pallas_call(
        functools.partial(_inproj_kernel, tm=tm, d_model=d),
        name="in_projection",
        out_shape=out_shape,
        grid=(bsz, s // tm),
        in_specs=in_specs,
        out_specs=out_specs,
        scratch_shapes=[
            pltpu.VMEM((tm, d), BF16),
            pltpu.VMEM((tm + HALO, D_CONV), F32),
            pltpu.VMEM((tm, D_CONV), BF16),
            pltpu.VMEM((SUBLANES - 1, tm + HALO, D_CONV), F32),
            pltpu.VMEM((tm, D_CONV), F32),
        ],
        compiler_params=pltpu.CompilerParams(
            dimension_semantics=("arbitrary", "arbitrary"), vmem_limit_bytes=VMEM_LIMIT),
    )(*args)


def _sortable(v):
    b = pltpu.bitcast(v, I32)
    return jnp.where(b < 0, INT_MIN - b, b)


def _attn_kernel(q_ref, qi_ref, wt_ref, kv_ref, kvt_ref, ki_ref, sga_ref, wuv_ref, btile_ref,
                 o_ref, keys_ref, qis_ref, tau_ref, m_ref, mask_ref, acc_ref, sa0_ref, sa1_ref,
                 s2a_ref, s2b_ref, hi_ref, lo_ref, cmaxa_ref, cmaxb_ref, *, t, k_top):
    sa_refs = (sa0_ref, sa1_ref)
    s2_refs = (s2a_ref, s2b_ref)
    cmax_refs = (cmaxa_ref, cmaxb_ref)
    j = pl.program_id(1)
    nchunk = j + 1
    npair = (j + 2) // 2
    groups = t // SUBLANES

    def chunk_start(c):
        return pl.multiple_of(c * t, t)

    for h in range(N_IDX_HEADS):
        qis_ref[:, h * t:(h + 1) * t] = qi_ref[0, h * D_IDX:(h + 1) * D_IDX, :]

    def by_parity(c, fn):
        for parity, (cur, nxt) in enumerate(((0, 1), (1, 0))):
            pl.when((c & 1) == parity)(functools.partial(fn, cur, nxt))

    def chunk_loop(n, step):
        def pair_body(cc, carry):
            step(2 * cc, 0, 1)
            step(2 * cc + 1, 1, 0)
            return carry
        lax.fori_loop(0, n // 2, pair_body, 0)
        pl.when(n % 2 == 1)(lambda: step(n - 1, 0, 1))

    def score_matmul(c, dst):
        kc = ki_ref[0, pl.ds(chunk_start(c), t), :]
        dst[...] = _dot(kc, qis_ref[...])

    def score_finish(src):
        blocks = []
        for r0 in range(0, t, SCORE_ROWS):
            acc = jnp.zeros((SCORE_ROWS, t), F32)
            for h in range(N_IDX_HEADS):
                s = src[r0:r0 + SCORE_ROWS, h * t:(h + 1) * t]
                acc = acc + wt_ref[0, h:h + 1, :] * jnp.maximum(s, 0.0)
            blocks.append(_sortable(acc))
        return jnp.concatenate(blocks, axis=0)

    def store_keys(c, key):
        keys_ref[pl.ds(chunk_start(c), t), :] = key
        hi_ref[pl.ds(chunk_start(c), t), :] = (key >> 16).astype(I16)
        lo_ref[pl.ds(chunk_start(c), t), :] = ((key & 0xFFFF) - 0x8000).astype(I16)

    def score_step(c, cur, nxt):
        score_matmul(c + 1, sa_refs[nxt])
        store_keys(c, score_finish(sa_refs[cur]))

    score_matmul(0, sa_refs[0])
    chunk_loop(j, score_step)
    sk = lax.broadcasted_iota(I32, (t, t), 0)
    tq = lax.broadcasted_iota(I32, (t, t), 1)

    by_parity(j, lambda cur, nxt: store_keys(
        j, jnp.where(sk <= tq, score_finish(sa_refs[cur]), NEG_KEY)))
    store_keys(j + 1, jnp.full((t, t), NEG_KEY, I32))

    tau_ref[...] = jnp.full((SUBLANES, t), NEG_KEY + 1, I32)

    def packed_rows(ref, cc):
        return ref[2 * t * cc:2 * t * (cc + 1), :].reshape(2 * t // PACKED_ROWS, PACKED_ROWS, t)

    def count16(plane_ref, pred, pairs):
        parts = []
        for cc in range(pairs):
            ones = jnp.where(pred(packed_rows(plane_ref, cc)), jnp.ones((), BF16),
                             jnp.zeros((), BF16))
            parts += [ones[i] for i in range(2 * t // PACKED_ROWS)]
        while len(parts) > 1:
            parts = [sum(parts[i:i + 2][1:], parts[i]) for i in range(0, len(parts), 2)]
        return jnp.sum(parts[0].astype(F32), axis=0, keepdims=True).astype(I32)

    def bisect16(plane_ref, want, n_all, pairs):
        def bit_body(i, carry):
            prefix, n_ge = carry
            cand = prefix | jnp.left_shift(jnp.int32(1), 15 - i)
            cand_b = jnp.broadcast_to((cand - 0x8000).astype(I16), (PACKED_ROWS, t))
            cnt = count16(plane_ref, lambda v: v >= cand_b[None], pairs)
            ok = cnt >= want
            return jnp.where(ok, cand, prefix), jnp.where(ok, cnt, n_ge)
        return lax.fori_loop(0, 16, bit_body, (jnp.zeros((1, t), I32), n_all))

    def count(pred):
        def body(cc, cnt):
            start = pl.multiple_of(cc * (2 * t), 2 * t)
            kk = keys_ref[pl.ds(start, 2 * t), :].reshape(2 * groups, SUBLANES, t)
            return cnt + jnp.sum(jnp.where(pred(kk, cc * (2 * t)), 1, 0), axis=0)
        cnt = lax.fori_loop(0, npair, body, jnp.zeros((SUBLANES, t), I32))
        return jnp.sum(cnt, axis=0, keepdims=True)

    def select_threshold(pairs):
        hi_u, n_ge_hi = bisect16(hi_ref, k_top, jnp.full((1, t), nchunk * t, I32), pairs)
        hi_b = jnp.broadcast_to((hi_u - 0x8000).astype(I16), (PACKED_ROWS, t))
        n_gt_hi = count16(hi_ref, lambda v: v > hi_b[None], pairs)
        for cc in range(pairs):
            lo_ref[2 * t * cc:2 * t * (cc + 1), :] = jnp.where(
                packed_rows(hi_ref, cc) == hi_b[None], packed_rows(lo_ref, cc),
                jnp.full((), -0x8000, I16)).reshape(2 * t, t)
        lo_u, n_ge_lo = bisect16(lo_ref, k_top - n_gt_hi, n_ge_hi - n_gt_hi, pairs)
        tau = (hi_u - 0x8000) * 65536 + lo_u
        n_ge = n_gt_hi + n_ge_lo
        tau_ref[...] = jnp.broadcast_to(jnp.maximum(tau, NEG_KEY + 1), (SUBLANES, t))

        tie = jnp.logical_and(n_ge > k_top, tau > NEG_KEY)

        @pl.when(jnp.max(jnp.where(tie, 1, 0)) > 0)
        def _():
            tau_b = jnp.broadcast_to(tau, (SUBLANES, t))
            n_gt = count(lambda kk, base: kk > tau_b[None])
            want = k_top - n_gt
            row = lax.broadcasted_iota(I32, (2 * groups, SUBLANES, t), 0) * SUBLANES + \
                lax.broadcasted_iota(I32, (2 * groups, SUBLANES, t), 1)

            def pos_body(i, pos):
                cand = pos | jnp.left_shift(jnp.int32(1), 15 - i)
                cand_b = jnp.broadcast_to(cand, (SUBLANES, t))
                cnt = count(lambda kk, base: jnp.logical_and(kk == tau_b[None],
                                                            row + base < cand_b[None]))
                return jnp.where(cnt < want, cand, pos)

            pos = lax.fori_loop(0, 16, pos_body, jnp.zeros((1, t), I32))
            pos_b = jnp.broadcast_to(jnp.where(tie, pos, jnp.int32(2 ** 30)), (SUBLANES, t))

            def demote_body(cc, carry):
                start = pl.multiple_of(cc * (2 * t), 2 * t)
                kk = keys_ref[pl.ds(start, 2 * t), :].reshape(2 * groups, SUBLANES, t)
                drop = jnp.logical_and(kk == tau_b[None], row + cc * (2 * t) > pos_b[None])
                keys_ref[pl.ds(start, 2 * t), :] = jnp.where(drop, NEG_KEY, kk).reshape(2 * t, t)
                return carry

            lax.fori_loop(0, npair, demote_body, 0)

    max_pairs = (keys_ref.shape[0] // t) // 2
    for pairs in range(1, max_pairs + 1):
        pl.when(jnp.logical_and(npair == pairs, nchunk * t > k_top))(
            functools.partial(select_threshold, pairs))

    m_ref[...] = jnp.full(m_ref.shape, -jnp.inf, F32)
    acc_ref[...] = jnp.zeros(acc_ref.shape, F32)
    tau_sel = tau_ref[...]
    ones_rows = jnp.ones((PACKED_ROWS, t), BF16)

    def set_mask(c):
        kk = keys_ref[pl.ds(chunk_start(c), t), :].reshape(groups, SUBLANES, t)
        mask_ref[...] = jnp.where(kk >= tau_sel[None], 0.0, -jnp.inf).reshape(t, t)

    def logits_matmul(c, slot, h):
        kvc = kv_ref[0, pl.ds(chunk_start(c), t), :]
        s = _dot(kvc, q_ref[0, h * KV_LORA:(h + 1) * KV_LORA, :]) + mask_ref[...]
        s2_refs[slot][h] = s
        cmax_refs[slot][h] = jnp.max(s.reshape(groups, SUBLANES, t), axis=0)

    def attend_step(c, cur, nxt, bias_tile):
        kvtc = jnp.concatenate([kvt_ref[0, :, pl.ds(chunk_start(c), t)], ones_rows], axis=0)
        if nxt is not None:
            set_mask(c + 1)
        pending = []

        def flush():
            h, alpha, pv = pending.pop(0)
            acc_ref[h] = alpha * acc_ref[h] + pv

        for h in range(N_HEADS):
            if nxt is not None:
                logits_matmul(c + 1, nxt, h)
            s = s2_refs[cur][h]
            if bias_tile is None:
                cmax = jnp.max(cmax_refs[cur][h], axis=0, keepdims=True)
            else:
                s = s + btile_ref[bias_tile, h]
                cmax = jnp.max(s, axis=0, keepdims=True)
            m_old = m_ref[h]
            m_new = jnp.maximum(m_old, cmax)
            m_safe = jnp.where(m_new == -jnp.inf, 0.0, m_new)
            alpha = jnp.exp2(m_old - m_safe)
            p = jnp.exp2(s - m_safe)
            m_ref[h] = m_new
            pending.append((h, alpha, _dot(kvtc, p.astype(BF16))))
            if len(pending) > PV_DELAY:
                flush()
        while pending:
            flush()

    set_mask(0)
    for h in range(N_HEADS):
        logits_matmul(0, 0, h)
    chunk_loop(jnp.maximum(j - 1, 0), functools.partial(attend_step, bias_tile=None))
    pl.when(j >= 1)(lambda: by_parity(
        j - 1, lambda cur, nxt: attend_step(j - 1, cur, nxt, bias_tile=1)))
    by_parity(j, lambda cur, nxt: attend_step(j, cur, None, bias_tile=0))

    ys = []
    for h in range(N_HEADS):
        denom = acc_ref[h, KV_LORA:KV_LORA + 1, :]
        o_t = (acc_ref[h, 0:KV_LORA, :] * (1.0 / denom)).astype(BF16)
        ys.append(_dot_tn(o_t, wuv_ref[h]))
    y = jnp.concatenate(ys, axis=-1)
    o_ref[0] = (y * sga_ref[0]).astype(BF16)


def _sparse_attention(q, qi, wt, kv, kvt, ki, sga, w_uv, btiles, *, t):
    bsz, s, _ = kv.shape
    k_top = min(TOPK_MAX, s // 4)
    assert (s + t) // PACKED_ROWS <= 256, "per-position key counts must stay exact in bf16"
    blk = lambda w: pl.BlockSpec((1, t, w), lambda b, j: (b, j, 0))
    cols = lambda rows: pl.BlockSpec((1, rows, t), lambda b, j: (b, 0, j))
    in_specs = [
        cols(N_HEADS * KV_LORA),
        cols(N_IDX_HEADS * D_IDX),
        cols(N_IDX_HEADS),
        pl.BlockSpec((1, s, KV_LORA), lambda b, j: (b, 0, 0)),
        pl.BlockSpec((1, KV_LORA, s), lambda b, j: (b, 0, 0)),
        pl.BlockSpec((1, s, D_IDX), lambda b, j: (b, 0, 0)),
        blk(D_ATTN),
        pl.BlockSpec(w_uv.shape, lambda b, j: (0, 0, 0)),
        pl.BlockSpec(btiles.shape, lambda b, j: (0, 0, 0, 0)),
    ]
    return pl.pallas_call(
        functools.partial(_attn_kernel, t=t, k_top=k_top),
        name="sparse_attention",
        out_shape=jax.ShapeDtypeStruct((bsz, s, D_ATTN), BF16),
        grid=(bsz, s // t),
        in_specs=in_specs,
        out_specs=blk(D_ATTN),
        scratch_shapes=[
            pltpu.VMEM((s + t, t), I32),
            pltpu.VMEM((D_IDX, N_IDX_HEADS * t), BF16),
            pltpu.VMEM((SUBLANES, t), I32),
            pltpu.VMEM((N_HEADS, 1, t), F32),
            pltpu.VMEM((t, t), F32),
            pltpu.VMEM((N_HEADS, KV_LORA + PACKED_ROWS, t), F32),
            pltpu.VMEM((t, N_IDX_HEADS * t), F32),
            pltpu.VMEM((t, N_IDX_HEADS * t), F32),
            pltpu.VMEM((N_HEADS, t, t), F32),
            pltpu.VMEM((N_HEADS, t, t), F32),
            pltpu.VMEM((s + t, t), I16),
            pltpu.VMEM((s + t, t), I16),
            pltpu.VMEM((N_HEADS, SUBLANES, t), F32),
            pltpu.VMEM((N_HEADS, SUBLANES, t), F32),
        ],
        compiler_params=pltpu.CompilerParams(
            dimension_semantics=("arbitrary", "arbitrary"), vmem_limit_bytes=VMEM_LIMIT),
    )(q, qi, wt, kv, kvt, ki, sga, w_uv, btiles)


def _outproj_kernel(x_ref, yc_ref, ya_ref, wo_ref, mod_ref, gpost_ref, o_ref, *, d_model):
    y = _dot(jnp.concatenate([yc_ref[0], ya_ref[0]], axis=-1), wo_ref[...])
    yn = (y * lax.rsqrt(jnp.mean(y * y, axis=-1, keepdims=True) + EPS)) * gpost_ref[...]
    gate = mod_ref[0, :, 2 * d_model:3 * d_model]
    o_ref[0] = x_ref[0] + gate * yn


def _out_projection(x, yc, ya, w_out, mod_l, g_post, *, tm):
    bsz, s, d = x.shape
    return pl.pallas_call(
        functools.partial(_outproj_kernel, d_model=d),
        name="out_projection",
        out_shape=jax.ShapeDtypeStruct((bsz, s, d), F32),
        grid=(bsz, s // tm),
        in_specs=[
            pl.BlockSpec((1, tm, d), lambda b, j: (b, j, 0)),
            pl.BlockSpec((1, tm, D_CONV), lambda b, j: (b, j, 0)),
            pl.BlockSpec((1, tm, D_ATTN), lambda b, j: (b, j, 0)),
            pl.BlockSpec(w_out.shape, lambda b, j: (0, 0)),
            pl.BlockSpec((1, 1, 3 * d), lambda b, j: (b, 0, 0)),
            pl.BlockSpec((1, d), lambda b, j: (0, 0)),
        ],
        out_specs=pl.BlockSpec((1, tm, d), lambda b, j: (b, j, 0)),
        compiler_params=pltpu.CompilerParams(
            dimension_semantics=("arbitrary", "arbitrary"), vmem_limit_bytes=VMEM_LIMIT),
    )(x, yc, ya, w_out, mod_l, g_post.reshape(1, d))


def _pack_w_in(w_in_l):
    d = w_in_l.shape[0]
    split = OFF_KW + D_IDX + N_IDX_HEADS
    return jnp.concatenate(
        [w_in_l[:, :split], jnp.zeros((d, KW_PAD), w_in_l.dtype), w_in_l[:, split:]],
        axis=-1).astype(BF16)


def kernel(x, c, w_ada, b_ada, g_pre, w_in, conv_w, conv_b, conv_ln_g, conv_ln_b, w_pw2, q_norm_g,
           w_uq, w_qidx, kv_norm_g, w_uv, rel_bias, w_out, g_post):
    depth = w_ada.shape[0]
    bsz, s, d = x.shape
    tm = min(512, s)
    t = min(ATTN_BLOCK, s)
    assert s % tm == 0 and s % t == 0 and d % LANES == 0
    assert tm % INPROJ_SUB == 0 and INPROJ_SUB % CONV_ROWS == 0

    mod = _modulation(c, w_ada, b_ada)
    btiles = _bias_tiles(rel_bias, t)
    for l in range(depth):
        mod_l = mod[l].reshape(bsz, 1, 3 * d)
        yc, q, qi, kv, kvt, ki, wt, sga = _in_projection(
            x, mod_l, g_pre[l], _pack_w_in(w_in[l]), conv_w[l], conv_b[l], conv_ln_g[l],
            conv_ln_b[l], w_pw2[l].astype(BF16), q_norm_g[l], w_uq[l].astype(BF16),
            w_qidx[l].astype(BF16), kv_norm_g[l], tm=tm)
        ya = _sparse_attention(q, qi, wt, kv, kvt, ki, sga, w_uv[l].astype(BF16), btiles, t=t)
        x = _out_projection(x, yc, ya, w_out[l].astype(BF16), mod_l, g_post[l], tm=tm)
    return x
```

```python
import functools
import math

import jax
import jax.numpy as jnp
from jax import lax
from jax.experimental import pallas as pl
from jax.experimental.pallas import tpu as pltpu

F32 = jnp.float32
BF16 = jnp.bfloat16
I32 = jnp.int32
I16 = jnp.int16

D_CONV = 512
CONV_WIDTH = 31
N_HEADS = 8
D_HEAD_OUT = 64
D_ATTN = N_HEADS * D_HEAD_OUT
Q_LORA = 256
KV_LORA = 128
N_IDX_HEADS = 8
D_IDX = 64
TOPK_MAX = 256
N_BUCKETS = 32
MAX_DISTANCE = 128
EPS = 1e-6

KW_PAD = 128 - D_IDX - N_IDX_HEADS
OFF_UVAL = 0
OFF_UGATE = OFF_UVAL + D_CONV
OFF_GCONV = OFF_UGATE + D_CONV
OFF_CQ = OFF_GCONV + D_CONV
OFF_CKV = OFF_CQ + Q_LORA
OFF_KW = OFF_CKV + KV_LORA
OFF_GATTN = OFF_KW + 128
D_IN_PACKED = OFF_GATTN + D_ATTN

HALO = 32
CONV_ROWS = 64
INPROJ_SUB = 256
ATTN_BLOCK = 256
PV_DELAY = 2
SCORE_ROWS = 32
LANES = 128
SUBLANES = 8
PACKED_ROWS = 2 * SUBLANES
VMEM_LIMIT = 48 * 1024 * 1024

INT_MIN = -(2 ** 31)
MASK_KEY = INT_MIN + 1
LOG2E = math.log2(math.e)


def _dot(a, b):
    return jnp.dot(a, b, preferred_element_type=F32)


def _dot_nt(a, b):
    return lax.dot_general(a, b, (((1,), (1,)), ((), ())), preferred_element_type=F32)


def _dot_tn(a, b):
    return lax.dot_general(a, b, (((0,), (0,)), ((), ())), preferred_element_type=F32)


def _silu(v):
    return v * jax.nn.sigmoid(v)


def _mod_kernel(c_ref, w_ref, b_ref, o_ref):
    c = c_ref[...]
    o_ref[0] = _dot(_silu(c).astype(BF16), w_ref[0].astype(BF16)) + b_ref[0]


def _modulation(c, w_ada, b_ada):
    depth, d, d3 = w_ada.shape
    bsz = c.shape[0]
    tn = 1024
    return pl.pallas_call(
        _mod_kernel,
        name="adaln_mod",
        out_shape=jax.ShapeDtypeStruct((depth, bsz, d3), F32),
        grid=(depth, d3 // tn),
        in_specs=[
            pl.BlockSpec((bsz, d), lambda l, n: (0, 0)),
            pl.BlockSpec((1, d, tn), lambda l, n: (l, 0, n)),
            pl.BlockSpec((1, 1, tn), lambda l, n: (l, 0, n)),
        ],
        out_specs=pl.BlockSpec((1, bsz, tn), lambda l, n: (l, 0, n)),
        compiler_params=pltpu.CompilerParams(
            dimension_semantics=("arbitrary", "arbitrary"), vmem_limit_bytes=VMEM_LIMIT),
    )(c, w_ada, b_ada.reshape(depth, 1, d3))


def _t5_bucket(n):
    max_exact = N_BUCKETS // 2
    n = jnp.maximum(n, 0)
    nf = jnp.maximum(n, 1).astype(F32)
    large = max_exact + (jnp.log(nf / max_exact) / math.log(MAX_DISTANCE / max_exact)
                         * (N_BUCKETS - max_exact)).astype(I32)
    large = jnp.minimum(large, N_BUCKETS - 1)
    return jnp.where(n < max_exact, n, large)


def _bias_kernel(rb_ref, o_ref, *, t):
    sk = lax.broadcasted_iota(I32, (t, t), 0)
    tq = lax.broadcasted_iota(I32, (t, t), 1)
    for kind in range(2):
        bucket = _t5_bucket(tq - sk + kind * t)
        for h in range(N_HEADS):
            val = jnp.zeros((t, t), F32)
            for b in range(N_BUCKETS):
                val = jnp.where(bucket == b, rb_ref[b, h], val)
            o_ref[kind, h] = (val - rb_ref[N_BUCKETS - 1, h]) * LOG2E


def _bias_tiles(rel_bias, t):
    return pl.pallas_call(
        functools.partial(_bias_kernel, t=t),
        name="t5_bias_tiles",
        out_shape=jax.ShapeDtypeStruct((2, N_HEADS, t, t), F32),
        in_specs=[pl.BlockSpec(memory_space=pltpu.SMEM)],
        out_specs=pl.BlockSpec(memory_space=pltpu.VMEM),
    )(rel_bias)


def _inproj_kernel(x_ref, mod_ref, gpre_ref, w_ref, cw_ref, cb_ref, lng_ref, lnb_ref, wpw2_ref,
                   qg_ref, wuq_ref, wqi_ref, kvg_ref,
                   yc_ref, q_ref, qi_ref, kv_ref, kvt_ref, ki_ref, wt_ref, sga_ref,
                   hbuf, abuf, ybuf, shbuf, gcbuf, *, tm, d_model):
    j = pl.program_id(1)

    @pl.when(j == 0)
    def _():
        abuf[0:HALO, :] = jnp.zeros((HALO, D_CONV), F32)

    shift = mod_ref[0, :, 0:d_model]
    scale = mod_ref[0, :, d_model:2 * d_model]
    first = HALO - (CONV_WIDTH - 1)

    for base in range(0, tm, INPROJ_SUB):
        rows = slice(base, base + INPROJ_SUB)
        x = x_ref[0, rows, :]
        ms = jnp.mean(x * x, axis=-1, keepdims=True)
        xn = (x * lax.rsqrt(ms + EPS)) * gpre_ref[...]
        hbuf[rows, :] = (xn * (1.0 + scale) + shift).astype(BF16)
        h = hbuf[rows, :]

        u_val = _dot(h, w_ref[:, OFF_UVAL:OFF_UVAL + D_CONV])
        u_gate = _dot(h, w_ref[:, OFF_UGATE:OFF_UGATE + D_CONV])
        abuf[HALO + base:HALO + base + INPROJ_SUB, :] = u_val * jax.nn.sigmoid(u_gate)

        span = slice(base, base + INPROJ_SUB + HALO - SUBLANES)
        for r in range(1, SUBLANES):
            shbuf[r - 1, span, :] = abuf[base + r:base + r + INPROJ_SUB + HALO - SUBLANES, :]

        gcbuf[rows, :] = _silu(_dot(h, w_ref[:, OFF_GCONV:OFF_GCONV + D_CONV]))

        c_q = _dot(h, w_ref[:, OFF_CQ:OFF_CQ + Q_LORA])
        cq = (c_q * lax.rsqrt(jnp.mean(c_q * c_q, axis=-1, keepdims=True) + EPS)) * qg_ref[...]
        cq = cq.astype(BF16)
        q = _dot(cq, wuq_ref[...]) * (KV_LORA ** -0.5 * LOG2E)
        q_ref[0, :, rows] = jnp.transpose(q).astype(BF16)
        qi_ref[0, :, rows] = jnp.transpose(_dot(cq, wqi_ref[...])).astype(BF16)

        c_kv = _dot(h, w_ref[:, OFF_CKV:OFF_CKV + KV_LORA])
        kvn = (c_kv * lax.rsqrt(jnp.mean(c_kv * c_kv, axis=-1, keepdims=True) + EPS)) * kvg_ref[...]
        kv_ref[0, rows, :] = kvn.astype(BF16)
        kvt_ref[0, :, rows] = jnp.transpose(kvn).astype(BF16)

        kw = _dot(h, w_ref[:, OFF_KW:OFF_KW + 128])
        ki_ref[0, rows, :] = kw[:, 0:D_IDX].astype(BF16)
        kwt = jnp.transpose(kw)
        wt_ref[0, :, rows] = kwt[D_IDX:D_IDX + N_IDX_HEADS, :] * (N_IDX_HEADS ** -0.5 * D_IDX ** -0.5)

        sga_ref[0, rows, :] = _silu(_dot(h, w_ref[:, OFF_GATTN:OFF_GATTN + D_ATTN]))

    for base in range(0, tm, INPROJ_SUB):
        rows = slice(base, base + INPROJ_SUB)
        for r0 in range(base, base + INPROJ_SUB, CONV_ROWS):
            acc = jnp.broadcast_to(cb_ref[...], (CONV_ROWS, D_CONV))
            for k in range(CONV_WIDTH):
                q8, r = divmod(first + k, SUBLANES)
                taps = slice(r0 + q8 * SUBLANES, r0 + q8 * SUBLANES + CONV_ROWS)
                tap = abuf[taps, :] if r == 0 else shbuf[r - 1, taps, :]
                acc = acc + cw_ref[k:k + 1, :] * tap
            mu = jnp.mean(acc, axis=-1, keepdims=True)
            cen = acc - mu
            var = jnp.mean(cen * cen, axis=-1, keepdims=True)
            yn = cen * lax.rsqrt(var + EPS) * lng_ref[...] + lnb_ref[...]
            ybuf[r0:r0 + CONV_ROWS, :] = _silu(yn).astype(BF16)

        yc_ref[0, rows, :] = (_dot(ybuf[rows, :], wpw2_ref[...]) * gcbuf[rows, :]).astype(BF16)

    abuf[0:HALO, :] = abuf[tm:tm + HALO, :]


def _in_projection(x, mod_l, g_pre, w_packed, conv_w, conv_b, ln_g, ln_b, w_pw2, q_g, w_uq, w_qi,
                   kv_g, *, tm):
    bsz, s, d = x.shape
    row = lambda v: v.reshape(1, -1)
    full = lambda a: pl.BlockSpec(a.shape, lambda b, j: (0,) * a.ndim)
    args = (x, mod_l, row(g_pre), w_packed, conv_w, row(conv_b), row(ln_g), row(ln_b), w_pw2,
            row(q_g), w_uq, w_qi, row(kv_g))
    in_specs = [
        pl.BlockSpec((1, tm, d), lambda b, j: (b, j, 0)),
        pl.BlockSpec((1, 1, 3 * d), lambda b, j: (b, 0, 0)),
    ] + [full(a) for a in args[2:]]
    blk = lambda w: pl.BlockSpec((1, tm, w), lambda b, j: (b, j, 0))
    out_shape = (
        jax.ShapeDtypeStruct((bsz, s, D_CONV), BF16),
        jax.ShapeDtypeStruct((bsz, N_HEADS * KV_LORA, s), BF16),
        jax.ShapeDtypeStruct((bsz, N_IDX_HEADS * D_IDX, s), BF16),
        jax.ShapeDtypeStruct((bsz, s, KV_LORA), BF16),
        jax.ShapeDtypeStruct((bsz, KV_LORA, s), BF16),
        jax.ShapeDtypeStruct((bsz, s, D_IDX), BF16),
        jax.ShapeDtypeStruct((bsz, N_IDX_HEADS, s), F32),
        jax.ShapeDtypeStruct((bsz, s, D_ATTN), F32),
    )
    out_specs = (
        blk(D_CONV),
        pl.BlockSpec((1, N_HEADS * KV_LORA, tm), lambda b, j: (b, 0, j)),
        pl.BlockSpec((1, N_IDX_HEADS * D_IDX, tm), lambda b, j: (b, 0, j)),
        blk(KV_LORA),
        pl.BlockSpec((1, KV_LORA, tm), lambda b, j: (b, 0, j)),
        blk(D_IDX),
        pl.BlockSpec((1, N_IDX_HEADS, tm), lambda b, j: (b, 0, j)),
        blk(D_ATTN),
    )
    return pl.pallas_call(
        functools.partial(_inproj_kernel, tm=tm, d_model=d),
        name="in_projection",
        out_shape=out_shape,
        grid=(bsz, s // tm),
        in_specs=in_specs,
        out_specs=out_specs,
        scratch_shapes=[
            pltpu.VMEM((tm, d), BF16),
            pltpu.VMEM((tm + HALO, D_CONV), F32),
            pltpu.VMEM((tm, D_CONV), BF16),
            pltpu.VMEM((SUBLANES - 1, tm + HALO, D_CONV), F32),
            pltpu.VMEM((tm, D_CONV), F32),
        ],
        compiler_params=pltpu.CompilerParams(
            dimension_semantics=("arbitrary", "arbitrary"), vmem_limit_bytes=VMEM_LIMIT),
    )(*args)


def _sortable(v):
    b = pltpu.bitcast(v, I32)
    return jnp.where(b < 0, INT_MIN - b, b)


def _attn_kernel(q_ref, qi_ref, wt_ref, kv_ref, kvt_ref, ki_ref, sga_ref, wuv_ref, btile_ref,
                 o_ref, keys_ref, qis_ref, tau_ref, m_ref, mask_ref, acc_ref, sa0_ref, sa1_ref,
                 s2a_ref, s2b_ref, hi_ref, lo_ref, cmaxa_ref, cmaxb_ref, *, t, k_top):
    sa_refs = (sa0_ref, sa1_ref)
    s2_refs = (s2a_ref, s2b_ref)
    cmax_refs = (cmaxa_ref, cmaxb_ref)
    j = pl.program_id(1)
    nchunk = j + 1
    npair = (j + 2) // 2
    groups = t // SUBLANES

    def chunk_start(c):
        return pl.multiple_of(c * t, t)

    for h in range(N_IDX_HEADS):
        qis_ref[:, h * t:(h + 1) * t] = qi_ref[0, h * D_IDX:(h + 1) * D_IDX, :]

    def by_parity(c, fn):
        for parity, (cur, nxt) in enumerate(((0, 1), (1, 0))):
            pl.when((c & 1) == parity)(functools.partial(fn, cur, nxt))

    def chunk_loop(n, step):
        def pair_body(cc, carry):
            step(2 * cc, 0, 1)
            step(2 * cc + 1, 1, 0)
            return carry
        lax.fori_loop(0, n // 2, pair_body, 0)
        pl.when(n % 2 == 1)(lambda: step(n - 1, 0, 1))

    def score_matmul(c, dst):
        kc = ki_ref[0, pl.ds(chunk_start(c), t), :]
        dst[...] = _dot(kc, qis_ref[...])

    span = keys_ref.shape[0]
    neg_row = -1 - lax.broadcasted_iota(I32, (SCORE_ROWS, t), 0)

    def score_finish(src, c):
        blocks = []
        for r0 in range(0, t, SCORE_ROWS):
            acc = jnp.zeros((SCORE_ROWS, t), F32)
            for h in range(N_IDX_HEADS):
                s = src[r0:r0 + SCORE_ROWS, h * t:(h + 1) * t]
                acc = acc + wt_ref[0, h:h + 1, :] * jnp.maximum(s, 0.0)
            key = _sortable(acc)
            zero_rank = neg_row - (c * t + r0)
            blocks.append(jnp.where(key == 0, zero_rank, jnp.where(key < 0, key - span, key)))
        return jnp.concatenate(blocks, axis=0)

    def store_keys(c, key):
        keys_ref[pl.ds(chunk_start(c), t), :] = key
        hi_ref[pl.ds(chunk_start(c), t), :] = (key >> 16).astype(I16)
        lo_ref[pl.ds(chunk_start(c), t), :] = ((key & 0xFFFF) - 0x8000).astype(I16)

    def score_step(c, cur, nxt):
        score_matmul(c + 1, sa_refs[nxt])
        store_keys(c, score_finish(sa_refs[cur], c))

    score_matmul(0, sa_refs[0])
    chunk_loop(j, score_step)
    sk = lax.broadcasted_iota(I32, (t, t), 0)
    tq = lax.broadcasted_iota(I32, (t, t), 1)

    by_parity(j, lambda cur, nxt: store_keys(
        j, jnp.where(sk <= tq, score_finish(sa_refs[cur], j), MASK_KEY)))
    store_keys(j + 1, jnp.full((t, t), MASK_KEY, I32))

    tau_ref[...] = jnp.full((SUBLANES, t), MASK_KEY + 1, I32)

    def packed_rows(ref, cc):
        return ref[2 * t * cc:2 * t * (cc + 1), :].reshape(2 * t // PACKED_ROWS, PACKED_ROWS, t)

    def count16(plane_ref, pred, pairs):
        parts = []
        for cc in range(pairs):
            ones = jnp.where(pred(packed_rows(plane_ref, cc)), jnp.ones((), BF16),
                             jnp.zeros((), BF16))
            parts += [ones[i] for i in range(2 * t // PACKED_ROWS)]
        while len(parts) > 1:
            parts = [sum(parts[i:i + 2][1:], parts[i]) for i in range(0, len(parts), 2)]
        return jnp.sum(parts[0].astype(F32), axis=0, keepdims=True).astype(I32)

    def bisect16(plane_ref, want, n_all, pairs):
        def bit_body(i, carry):
            prefix, n_ge = carry
            cand = prefix | jnp.left_shift(jnp.int32(1), 15 - i)
            cand_b = jnp.broadcast_to((cand - 0x8000).astype(I16), (PACKED_ROWS, t))
            cnt = count16(plane_ref, lambda v: v >= cand_b[None], pairs)
            ok = cnt >= want
            return jnp.where(ok, cand, prefix), jnp.where(ok, cnt, n_ge)
        return lax.fori_loop(0, 16, bit_body, (jnp.zeros((1, t), I32), n_all))

    def count(pred):
        def body(cc, cnt):
            start = pl.multiple_of(cc * (2 * t), 2 * t)
            kk = keys_ref[pl.ds(start, 2 * t), :].reshape(2 * groups, SUBLANES, t)
            return cnt + jnp.sum(jnp.where(pred(kk, cc * (2 * t)), 1, 0), axis=0)
        cnt = lax.fori_loop(0, npair, body, jnp.zeros((SUBLANES, t), I32))
        return jnp.sum(cnt, axis=0, keepdims=True)

    def select_threshold(pairs):
        hi_u, n_ge_hi = bisect16(hi_ref, k_top, jnp.full((1, t), nchunk * t, I32), pairs)
        hi_b = jnp.broadcast_to((hi_u - 0x8000).astype(I16), (PACKED_ROWS, t))
        n_gt_hi = count16(hi_ref, lambda v: v > hi_b[None], pairs)
        for cc in range(pairs):
            lo_ref[2 * t * cc:2 * t * (cc + 1), :] = jnp.where(
                packed_rows(hi_ref, cc) == hi_b[None], packed_rows(lo_ref, cc),
                jnp.full((), -0x8000, I16)).reshape(2 * t, t)
        lo_u, n_ge_lo = bisect16(lo_ref, k_top - n_gt_hi, n_ge_hi - n_gt_hi, pairs)
        tau = (hi_u - 0x8000) * 65536 + lo_u
        n_ge = n_gt_hi + n_ge_lo
        tau_ref[...] = jnp.broadcast_to(jnp.maximum(tau, MASK_KEY + 1), (SUBLANES, t))

        tie = jnp.logical_and(n_ge > k_top, tau > MASK_KEY)

        @pl.when(jnp.max(jnp.where(tie, 1, 0)) > 0)
        def _():
            tau_b = jnp.broadcast_to(tau, (SUBLANES, t))
            n_gt = count(lambda kk, base: kk > tau_b[None])
            want = k_top - n_gt
            row = lax.broadcasted_iota(I32, (2 * groups, SUBLANES, t), 0) * SUBLANES + \
                lax.broadcasted_iota(I32, (2 * groups, SUBLANES, t), 1)

            def pos_body(i, pos):
                cand = pos | jnp.left_shift(jnp.int32(1), 15 - i)
                cand_b = jnp.broadcast_to(cand, (SUBLANES, t))
                cnt = count(lambda kk, base: jnp.logical_and(kk == tau_b[None],
                                                            row + base < cand_b[None]))
                return jnp.where(cnt < want, cand, pos)

            pos = lax.fori_loop(0, 16, pos_body, jnp.zeros((1, t), I32))
            pos_b = jnp.broadcast_to(jnp.where(tie, pos, jnp.int32(2 ** 30)), (SUBLANES, t))

            def demote_body(cc, carry):
                start = pl.multiple_of(cc * (2 * t), 2 * t)
                kk = keys_ref[pl.ds(start, 2 * t), :].reshape(2 * groups, SUBLANES, t)
                drop = jnp.logical_and(kk == tau_b[None], row + cc * (2 * t) > pos_b[None])
                keys_ref[pl.ds(start, 2 * t), :] = jnp.where(drop, MASK_KEY, kk).reshape(2 * t, t)
                return carry

            lax.fori_loop(0, npair, demote_body, 0)

    max_pairs = (keys_ref.shape[0] // t) // 2
    for pairs in range(1, max_pairs + 1):
        pl.when(jnp.logical_and(npair == pairs, nchunk * t > k_top))(
            functools.partial(select_threshold, pairs))

    m_ref[...] = jnp.full(m_ref.shape, -jnp.inf, F32)
    acc_ref[...] = jnp.zeros(acc_ref.shape, F32)
    tau_sel = tau_ref[...]
    ones_rows = jnp.ones((PACKED_ROWS, t), BF16)

    def set_mask(c):
        kk = keys_ref[pl.ds(chunk_start(c), t), :].reshape(groups, SUBLANES, t)
        mask_ref[...] = jnp.where(kk >= tau_sel[None], 0.0, -jnp.inf).reshape(t, t)

    def logits_matmul(c, slot, h):
        kvc = kv_ref[0, pl.ds(chunk_start(c), t), :]
        s = _dot(kvc, q_ref[0, h * KV_LORA:(h + 1) * KV_LORA, :]) + mask_ref[...]
        s2_refs[slot][h] = s
        cmax_refs[slot][h] = jnp.max(s.reshape(groups, SUBLANES, t), axis=0)

    def attend_step(c, cur, nxt, bias_tile):
        kvtc = jnp.concatenate([kvt_ref[0, :, pl.ds(chunk_start(c), t)], ones_rows], axis=0)
        if nxt is not None:
            set_mask(c + 1)
        pending = []

        def flush():
            h, alpha, pv = pending.pop(0)
            acc_ref[h] = alpha * acc_ref[h] + pv

        for h in range(N_HEADS):
            if nxt is not None:
                logits_matmul(c + 1, nxt, h)
            s = s2_refs[cur][h]
            if bias_tile is None:
                cmax = jnp.max(cmax_refs[cur][h], axis=0, keepdims=True)
            else:
                s = s + btile_ref[bias_tile, h]
                cmax = jnp.max(s, axis=0, keepdims=True)
            m_old = m_ref[h]
            m_new = jnp.maximum(m_old, cmax)
            m_safe = jnp.where(m_new == -jnp.inf, 0.0, m_new)
            alpha = jnp.exp2(m_old - m_safe)
            p = jnp.exp2(s - m_safe)
            m_ref[h] = m_new
            pending.append((h, alpha, _dot(kvtc, p.astype(BF16))))
            if len(pending) > PV_DELAY:
                flush()
        while pending:
            flush()

    set_mask(0)
    for h in range(N_HEADS):
        logits_matmul(0, 0, h)
    chunk_loop(jnp.maximum(j - 1, 0), functools.partial(attend_step, bias_tile=None))
    pl.when(j >= 1)(lambda: by_parity(
        j - 1, lambda cur, nxt: attend_step(j - 1, cur, nxt, bias_tile=1)))
    by_parity(j, lambda cur, nxt: attend_step(j, cur, None, bias_tile=0))

    ys = []
    for h in range(N_HEADS):
        denom = acc_ref[h, KV_LORA:KV_LORA + 1, :]
        o_t = (acc_ref[h, 0:KV_LORA, :] * (1.0 / denom)).astype(BF16)
        ys.append(_dot_tn(o_t, wuv_ref[h]))
    y = jnp.concatenate(ys, axis=-1)
    o_ref[0] = (y * sga_ref[0]).astype(BF16)


def _sparse_attention(q, qi, wt, kv, kvt, ki, sga, w_uv, btiles, *, t):
    bsz, s, _ = kv.shape
    k_top = min(TOPK_MAX, s // 4)
    assert (s + t) // PACKED_ROWS <= 256, "per-position key counts must stay exact in bf16"
    blk = lambda w: pl.BlockSpec((1, t, w), lambda b, j: (b, j, 0))
    cols = lambda rows: pl.BlockSpec((1, rows, t), lambda b, j: (b, 0, j))
    in_specs = [
        cols(N_HEADS * KV_LORA),
        cols(N_IDX_HEADS * D_IDX),
        cols(N_IDX_HEADS),
        pl.BlockSpec((1, s, KV_LORA), lambda b, j: (b, 0, 0)),
        pl.BlockSpec((1, KV_LORA, s), lambda b, j: (b, 0, 0)),
        pl.BlockSpec((1, s, D_IDX), lambda b, j: (b, 0, 0)),
        blk(D_ATTN),
        pl.BlockSpec(w_uv.shape, lambda b, j: (0, 0, 0)),
        pl.BlockSpec(btiles.shape, lambda b, j: (0, 0, 0, 0)),
    ]
    return pl.pallas_call(
        functools.partial(_attn_kernel, t=t, k_top=k_top),
        name="sparse_attention",
        out_shape=jax.ShapeDtypeStruct((bsz, s, D_ATTN), BF16),
        grid=(bsz, s // t),
        in_specs=in_specs,
        out_specs=blk(D_ATTN),
        scratch_shapes=[
            pltpu.VMEM((s + t, t), I32),
            pltpu.VMEM((D_IDX, N_IDX_HEADS * t), BF16),
            pltpu.VMEM((SUBLANES, t), I32),
            pltpu.VMEM((N_HEADS, 1, t), F32),
            pltpu.VMEM((t, t), F32),
            pltpu.VMEM((N_HEADS, KV_LORA + PACKED_ROWS, t), F32),
            pltpu.VMEM((t, N_IDX_HEADS * t), F32),
            pltpu.VMEM((t, N_IDX_HEADS * t), F32),
            pltpu.VMEM((N_HEADS, t, t), F32),
            pltpu.VMEM((N_HEADS, t, t), F32),
            pltpu.VMEM((s + t, t), I16),
            pltpu.VMEM((s + t, t), I16),
            pltpu.VMEM((N_HEADS, SUBLANES, t), F32),
            pltpu.VMEM((N_HEADS, SUBLANES, t), F32),
        ],
        compiler_params=pltpu.CompilerParams(
            dimension_semantics=("arbitrary", "arbitrary"), vmem_limit_bytes=VMEM_LIMIT),
    )(q, qi, wt, kv, kvt, ki, sga, w_uv, btiles)


def _outproj_kernel(x_ref, yc_ref, ya_ref, wo_ref, mod_ref, gpost_ref, o_ref, *, d_model):
    y = _dot(jnp.concatenate([yc_ref[0], ya_ref[0]], axis=-1), wo_ref[...])
    yn = (y * lax.rsqrt(jnp.mean(y * y, axis=-1, keepdims=True) + EPS)) * gpost_ref[...]
    gate = mod_ref[0, :, 2 * d_model:3 * d_model]
    o_ref[0] = x_ref[0] + gate * yn


def _out_projection(x, yc, ya, w_out, mod_l, g_post, *, tm):
    bsz, s, d = x.shape
    return pl.pallas_call(
        functools.partial(_outproj_kernel, d_model=d),
        name="out_projection",
        out_shape=jax.ShapeDtypeStruct((bsz, s, d), F32),
        grid=(bsz, s // tm),
        in_specs=[
            pl.BlockSpec((1, tm, d), lambda b, j: (b, j, 0)),
            pl.BlockSpec((1, tm, D_CONV), lambda b, j: (b, j, 0)),
            pl.BlockSpec((1, tm, D_ATTN), lambda b, j: (b, j, 0)),
            pl.BlockSpec(w_out.shape, lambda b, j: (0, 0)),
            pl.BlockSpec((1, 1, 3 * d), lambda b, j: (b, 0, 0)),
            pl.BlockSpec((1, d), lambda b, j: (0, 0)),
        ],
        out_specs=pl.BlockSpec((1, tm, d), lambda b, j: (b, j, 0)),
        compiler_params=pltpu.CompilerParams(
            dimension_semantics=("arbitrary", "arbitrary"), vmem_limit_bytes=VMEM_LIMIT),
    )(x, yc, ya, w_out, mod_l, g_post.reshape(1, d))


def _pack_w_in(w_in_l):
    d = w_in_l.shape[0]
    split = OFF_KW + D_IDX + N_IDX_HEADS
    return jnp.concatenate(
        [w_in_l[:, :split], jnp.zeros((d, KW_PAD), w_in_l.dtype), w_in_l[:, split:]],
        axis=-1).astype(BF16)


def kernel(x, c, w_ada, b_ada, g_pre, w_in, conv_w, conv_b, conv_ln_g, conv_ln_b, w_pw2, q_norm_g,
           w_uq, w_qidx, kv_norm_g, w_uv, rel_bias, w_out, g_post):
    depth = w_ada.shape[0]
    bsz, s, d = x.shape
    tm = min(512, s)
    t = min(ATTN_BLOCK, s)
    assert s % tm == 0 and s % t == 0 and d % LANES == 0
    assert tm % INPROJ_SUB == 0 and INPROJ_SUB % CONV_ROWS == 0

    mod = _modulation(c, w_ada, b_ada)
    btiles = _bias_tiles(rel_bias, t)
    for l in range(depth):
        mod_l = mod[l].reshape(bsz, 1, 3 * d)
        yc, q, qi, kv, kvt, ki, wt, sga = _in_projection(
            x, mod_l, g_pre[l], _pack_w_in(w_in[l]), conv_w[l], conv_b[l], conv_ln_g[l],
            conv_ln_b[l], w_pw2[l].astype(BF16), q_norm_g[l], w_uq[l].astype(BF16),
            w_qidx[l].astype(BF16), kv_norm_g[l], tm=tm)
        ya = _sparse_attention(q, qi, wt, kv, kvt, ki, sga, w_uv[l].astype(BF16), btiles, t=t)
        x = _out_projection(x, yc, ya, w_out[l].astype(BF16), mod_l, g_post[l], tm=tm)
    return x
```

```python
import functools
import math

import jax
import jax.numpy as jnp
from jax import lax
from jax.experimental import pallas as pl
from jax.experimental.pallas import tpu as pltpu

F32 = jnp.float32
BF16 = jnp.bfloat16
I32 = jnp.int32
I16 = jnp.int16

D_CONV = 512
CONV_WIDTH = 31
N_HEADS = 8
D_HEAD_OUT = 64
D_ATTN = N_HEADS * D_HEAD_OUT
Q_LORA = 256
KV_LORA = 128
N_IDX_HEADS = 8
D_IDX = 64
TOPK_MAX = 256
N_BUCKETS = 32
MAX_DISTANCE = 128
EPS = 1e-6

KW_PAD = 128 - D_IDX - N_IDX_HEADS
OFF_UVAL = 0
OFF_UGATE = OFF_UVAL + D_CONV
OFF_GCONV = OFF_UGATE + D_CONV
OFF_CQ = OFF_GCONV + D_CONV
OFF_CKV = OFF_CQ + Q_LORA
OFF_KW = OFF_CKV + KV_LORA
OFF_GATTN = OFF_KW + 128
D_IN_PACKED = OFF_GATTN + D_ATTN

HALO = 32
CONV_ROWS = 64
INPROJ_SUB = 256
OUTPROJ_ROWS = 1024
ATTN_BLOCK = 256
PV_DELAY = 2
SCORE_ROWS = 32
LANES = 128
SUBLANES = 8
PACKED_ROWS = 2 * SUBLANES
VMEM_LIMIT = 48 * 1024 * 1024

INT_MIN = -(2 ** 31)
MASK_KEY = INT_MIN + 1
LOG2E = math.log2(math.e)


def _dot(a, b):
    return jnp.dot(a, b, preferred_element_type=F32)


def _dot_nt(a, b):
    return lax.dot_general(a, b, (((1,), (1,)), ((), ())), preferred_element_type=F32)


def _dot_tn(a, b):
    return lax.dot_general(a, b, (((0,), (0,)), ((), ())), preferred_element_type=F32)


def _silu(v):
    return v * jax.nn.sigmoid(v)


def _mod_kernel(c_ref, w_ref, b_ref, o_ref):
    c = c_ref[...]
    o_ref[0] = _dot(_silu(c).astype(BF16), w_ref[0].astype(BF16)) + b_ref[0]


def _modulation(c, w_ada, b_ada):
    depth, d, d3 = w_ada.shape
    bsz = c.shape[0]
    tn = 1024
    return pl.pallas_call(
        _mod_kernel,
        name="adaln_mod",
        out_shape=jax.ShapeDtypeStruct((depth, bsz, d3), F32),
        grid=(depth, d3 // tn),
        in_specs=[
            pl.BlockSpec((bsz, d), lambda l, n: (0, 0)),
            pl.BlockSpec((1, d, tn), lambda l, n: (l, 0, n)),
            pl.BlockSpec((1, 1, tn), lambda l, n: (l, 0, n)),
        ],
        out_specs=pl.BlockSpec((1, bsz, tn), lambda l, n: (l, 0, n)),
        compiler_params=pltpu.CompilerParams(
            dimension_semantics=("arbitrary", "arbitrary"), vmem_limit_bytes=VMEM_LIMIT),
    )(c, w_ada, b_ada.reshape(depth, 1, d3))


def _t5_bucket(n):
    max_exact = N_BUCKETS // 2
    n = jnp.maximum(n, 0)
    nf = jnp.maximum(n, 1).astype(F32)
    large = max_exact + (jnp.log(nf / max_exact) / math.log(MAX_DISTANCE / max_exact)
                         * (N_BUCKETS - max_exact)).astype(I32)
    large = jnp.minimum(large, N_BUCKETS - 1)
    return jnp.where(n < max_exact, n, large)


def _bias_kernel(rb_ref, o_ref, *, t):
    sk = lax.broadcasted_iota(I32, (t, t), 0)
    tq = lax.broadcasted_iota(I32, (t, t), 1)
    for kind in range(2):
        bucket = _t5_bucket(tq - sk + kind * t)
        for h in range(N_HEADS):
            val = jnp.zeros((t, t), F32)
            for b in range(N_BUCKETS):
                val = jnp.where(bucket == b, rb_ref[b, h], val)
            o_ref[kind, h] = (val - rb_ref[N_BUCKETS - 1, h]) * LOG2E


def _bias_tiles(rel_bias, t):
    return pl.pallas_call(
        functools.partial(_bias_kernel, t=t),
        name="t5_bias_tiles",
        out_shape=jax.ShapeDtypeStruct((2, N_HEADS, t, t), F32),
        in_specs=[pl.BlockSpec(memory_space=pltpu.SMEM)],
        out_specs=pl.BlockSpec(memory_space=pltpu.VMEM),
    )(rel_bias)


def _inproj_kernel(x_ref, mod_ref, gpre_ref, w_ref, cw_ref, cb_ref, lng_ref, lnb_ref, wpw2_ref,
                   qg_ref, wuq_ref, wqi_ref, kvg_ref,
                   yc_ref, q_ref, qi_ref, kv_ref, kvt_ref, ki_ref, wt_ref, sga_ref,
                   hbuf, abuf, ybuf, shbuf, gcbuf, *, tm, d_model):
    j = pl.program_id(1)

    @pl.when(j == 0)
    def _():
        abuf[0:HALO, :] = jnp.zeros((HALO, D_CONV), F32)

    shift = mod_ref[0, :, 0:d_model]
    scale = mod_ref[0, :, d_model:2 * d_model]
    first = HALO - (CONV_WIDTH - 1)

    for base in range(0, tm, INPROJ_SUB):
        rows = slice(base, base + INPROJ_SUB)
        x = x_ref[0, rows, :]
        ms = jnp.mean(x * x, axis=-1, keepdims=True)
        xn = (x * lax.rsqrt(ms + EPS)) * gpre_ref[...]
        hbuf[rows, :] = (xn * (1.0 + scale) + shift).astype(BF16)
        h = hbuf[rows, :]

        u_val = _dot(h, w_ref[:, OFF_UVAL:OFF_UVAL + D_CONV])
        u_gate = _dot(h, w_ref[:, OFF_UGATE:OFF_UGATE + D_CONV])
        abuf[HALO + base:HALO + base + INPROJ_SUB, :] = u_val * jax.nn.sigmoid(u_gate)

        span = slice(base, base + INPROJ_SUB + HALO - SUBLANES)
        for r in range(1, SUBLANES):
            shbuf[r - 1, span, :] = abuf[base + r:base + r + INPROJ_SUB + HALO - SUBLANES, :]

        gcbuf[rows, :] = _silu(_dot(h, w_ref[:, OFF_GCONV:OFF_GCONV + D_CONV]))

        c_q = _dot(h, w_ref[:, OFF_CQ:OFF_CQ + Q_LORA])
        cq = (c_q * lax.rsqrt(jnp.mean(c_q * c_q, axis=-1, keepdims=True) + EPS)) * qg_ref[...]
        cq = cq.astype(BF16)
        q = _dot(cq, wuq_ref[...]) * (KV_LORA ** -0.5 * LOG2E)
        q_ref[0, :, rows] = jnp.transpose(q).astype(BF16)
        qi_ref[0, :, rows] = jnp.transpose(_dot(cq, wqi_ref[...])).astype(BF16)

        c_kv = _dot(h, w_ref[:, OFF_CKV:OFF_CKV + KV_LORA])
        kvn = (c_kv * lax.rsqrt(jnp.mean(c_kv * c_kv, axis=-1, keepdims=True) + EPS)) * kvg_ref[...]
        kv_ref[0, rows, :] = kvn.astype(BF16)
        kvt_ref[0, :, rows] = jnp.transpose(kvn).astype(BF16)

        kw = _dot(h, w_ref[:, OFF_KW:OFF_KW + 128])
        ki_ref[0, rows, :] = kw[:, 0:D_IDX].astype(BF16)
        kwt = jnp.transpose(kw)
        wt_ref[0, :, rows] = kwt[D_IDX:D_IDX + N_IDX_HEADS, :] * (N_IDX_HEADS ** -0.5 * D_IDX ** -0.5)

        sga_ref[0, rows, :] = _silu(_dot(h, w_ref[:, OFF_GATTN:OFF_GATTN + D_ATTN]))

    for base in range(0, tm, INPROJ_SUB):
        rows = slice(base, base + INPROJ_SUB)
        for r0 in range(base, base + INPROJ_SUB, CONV_ROWS):
            acc = jnp.broadcast_to(cb_ref[...], (CONV_ROWS, D_CONV))
            for k in range(CONV_WIDTH):
                q8, r = divmod(first + k, SUBLANES)
                taps = slice(r0 + q8 * SUBLANES, r0 + q8 * SUBLANES + CONV_ROWS)
                tap = abuf[taps, :] if r == 0 else shbuf[r - 1, taps, :]
                acc = acc + cw_ref[k:k + 1, :] * tap
            mu = jnp.mean(acc, axis=-1, keepdims=True)
            cen = acc - mu
            var = jnp.mean(cen * cen, axis=-1, keepdims=True)
            yn = cen * lax.rsqrt(var + EPS) * lng_ref[...] + lnb_ref[...]
            ybuf[r0:r0 + CONV_ROWS, :] = _silu(yn).astype(BF16)

        yc_ref[0, rows, :] = (_dot(ybuf[rows, :], wpw2_ref[...]) * gcbuf[rows, :]).astype(BF16)

    abuf[0:HALO, :] = abuf[tm:tm + HALO, :]


def _in_projection(x, mod_l, g_pre, w_packed, conv_w, conv_b, ln_g, ln_b, w_pw2, q_g, w_uq, w_qi,
                   kv_g, *, tm):
    bsz, s, d = x.shape
    row = lambda v: v.reshape(1, -1)
    full = lambda a: pl.BlockSpec(a.shape, lambda b, j: (0,) * a.ndim)
    args = (x, mod_l, row(g_pre), w_packed, conv_w, row(conv_b), row(ln_g), row(ln_b), w_pw2,
            row(q_g), w_uq, w_qi, row(kv_g))
    in_specs = [
        pl.BlockSpec((1, tm, d), lambda b, j: (b, j, 0)),
        pl.BlockSpec((1, 1, 3 * d), lambda b, j: (b, 0, 0)),
    ] + [full(a) for a in args[2:]]
    blk = lambda w: pl.BlockSpec((1, tm, w), lambda b, j: (b, j, 0))
    out_shape = (
        jax.ShapeDtypeStruct((bsz, s, D_CONV), BF16),
        jax.ShapeDtypeStruct((bsz, N_HEADS * KV_LORA, s), BF16),
        jax.ShapeDtypeStruct((bsz, N_IDX_HEADS * D_IDX, s), BF16),
        jax.ShapeDtypeStruct((bsz, s, KV_LORA), BF16),
        jax.ShapeDtypeStruct((bsz, KV_LORA, s), BF16),
        jax.ShapeDtypeStruct((bsz, s, D_IDX), BF16),
        jax.ShapeDtypeStruct((bsz, N_IDX_HEADS, s), F32),
        jax.ShapeDtypeStruct((bsz, s, D_ATTN), F32),
    )
    out_specs = (
        blk(D_CONV),
        pl.BlockSpec((1, N_HEADS * KV_LORA, tm), lambda b, j: (b, 0, j)),
        pl.BlockSpec((1, N_IDX_HEADS * D_IDX, tm), lambda b, j: (b, 0, j)),
        blk(KV_LORA),
        pl.BlockSpec((1, KV_LORA, tm), lambda b, j: (b, 0, j)),
        blk(D_IDX),
        pl.BlockSpec((1, N_IDX_HEADS, tm), lambda b, j: (b, 0, j)),
        blk(D_ATTN),
    )
    return pl.pallas_call(
        functools.partial(_inproj_kernel, tm=tm, d_model=d),
        name="in_projection",
        out_shape=out_shape,
        grid=(bsz, s // tm),
        in_specs=in_specs,
        out_specs=out_specs,
        scratch_shapes=[
            pltpu.VMEM((tm, d), BF16),
            pltpu.VMEM((tm + HALO, D_CONV), F32),
            pltpu.VMEM((tm, D_CONV), BF16),
            pltpu.VMEM((SUBLANES - 1, tm + HALO, D_CONV), F32),
            pltpu.VMEM((tm, D_CONV), F32),
        ],
        compiler_params=pltpu.CompilerParams(
            dimension_semantics=("arbitrary", "arbitrary"), vmem_limit_bytes=VMEM_LIMIT),
    )(*args)


def _sortable(v):
    b = pltpu.bitcast(v, I32)
    return jnp.where(b < 0, INT_MIN - b, b)


def _attn_kernel(q_ref, qi_ref, wt_ref, kv_ref, kvt_ref, ki_ref, sga_ref, wuv_ref, btile_ref,
                 o_ref, keys_ref, qis_ref, tau_ref, m_ref, mask_ref, acc_ref, sa0_ref, sa1_ref,
                 s2a_ref, s2b_ref, hi_ref, lo_ref, cmaxa_ref, cmaxb_ref, *, t, k_top):
    sa_refs = (sa0_ref, sa1_ref)
    s2_refs = (s2a_ref, s2b_ref)
    cmax_refs = (cmaxa_ref, cmaxb_ref)
    j = pl.program_id(1)
    nchunk = j + 1
    npair = (j + 2) // 2
    groups = t // SUBLANES

    def chunk_start(c):
        return pl.multiple_of(c * t, t)

    for h in range(N_IDX_HEADS):
        qis_ref[:, h * t:(h + 1) * t] = qi_ref[0, h * D_IDX:(h + 1) * D_IDX, :]

    def by_parity(c, fn):
        for parity, (cur, nxt) in enumerate(((0, 1), (1, 0))):
            pl.when((c & 1) == parity)(functools.partial(fn, cur, nxt))

    def chunk_loop(n, step):
        def pair_body(cc, carry):
            step(2 * cc, 0, 1)
            step(2 * cc + 1, 1, 0)
            return carry
        lax.fori_loop(0, n // 2, pair_body, 0)
        pl.when(n % 2 == 1)(lambda: step(n - 1, 0, 1))

    def score_matmul(c, dst):
        kc = ki_ref[0, pl.ds(chunk_start(c), t), :]
        dst[...] = _dot(kc, qis_ref[...])

    span = keys_ref.shape[0]
    neg_row = -1 - lax.broadcasted_iota(I32, (SCORE_ROWS, t), 0)

    def score_finish(src, c):
        blocks = []
        for r0 in range(0, t, SCORE_ROWS):
            acc = jnp.zeros((SCORE_ROWS, t), F32)
            for h in range(N_IDX_HEADS):
                s = src[r0:r0 + SCORE_ROWS, h * t:(h + 1) * t]
                acc = acc + wt_ref[0, h:h + 1, :] * jnp.maximum(s, 0.0)
            key = _sortable(acc)
            zero_rank = neg_row - (c * t + r0)
            blocks.append(jnp.where(key == 0, zero_rank, jnp.where(key < 0, key - span, key)))
        return jnp.concatenate(blocks, axis=0)

    def store_keys(c, key):
        keys_ref[pl.ds(chunk_start(c), t), :] = key
        hi_ref[pl.ds(chunk_start(c), t), :] = (key >> 16).astype(I16)
        lo_ref[pl.ds(chunk_start(c), t), :] = ((key & 0xFFFF) - 0x8000).astype(I16)

    def score_step(c, cur, nxt):
        score_matmul(c + 1, sa_refs[nxt])
        store_keys(c, score_finish(sa_refs[cur], c))

    score_matmul(0, sa_refs[0])
    chunk_loop(j, score_step)
    sk = lax.broadcasted_iota(I32, (t, t), 0)
    tq = lax.broadcasted_iota(I32, (t, t), 1)

    by_parity(j, lambda cur, nxt: store_keys(
        j, jnp.where(sk <= tq, score_finish(sa_refs[cur], j), MASK_KEY)))
    store_keys(j + 1, jnp.full((t, t), MASK_KEY, I32))

    tau_ref[...] = jnp.full((SUBLANES, t), MASK_KEY + 1, I32)

    def packed_rows(ref, cc):
        return ref[2 * t * cc:2 * t * (cc + 1), :].reshape(2 * t // PACKED_ROWS, PACKED_ROWS, t)

    def count16(plane_ref, pred, pairs):
        parts = []
        for cc in range(pairs):
            ones = jnp.where(pred(packed_rows(plane_ref, cc)), jnp.ones((), BF16),
                             jnp.zeros((), BF16))
            parts += [ones[i] for i in range(2 * t // PACKED_ROWS)]
        while len(parts) > 1:
            parts = [sum(parts[i:i + 2][1:], parts[i]) for i in range(0, len(parts), 2)]
        return jnp.sum(parts[0].astype(F32), axis=0, keepdims=True).astype(I32)

    def bisect16(plane_ref, want, n_all, pairs):
        def bit_body(i, carry):
            prefix, n_ge = carry
            cand = prefix | jnp.left_shift(jnp.int32(1), 15 - i)
            cand_b = jnp.broadcast_to((cand - 0x8000).astype(I16), (PACKED_ROWS, t))
            cnt = count16(plane_ref, lambda v: v >= cand_b[None], pairs)
            ok = cnt >= want
            return jnp.where(ok, cand, prefix), jnp.where(ok, cnt, n_ge)
        return lax.fori_loop(0, 16, bit_body, (jnp.zeros((1, t), I32), n_all))

    def count(pred):
        def body(cc, cnt):
            start = pl.multiple_of(cc * (2 * t), 2 * t)
            kk = keys_ref[pl.ds(start, 2 * t), :].reshape(2 * groups, SUBLANES, t)
            return cnt + jnp.sum(jnp.where(pred(kk, cc * (2 * t)), 1, 0), axis=0)
        cnt = lax.fori_loop(0, npair, body, jnp.zeros((SUBLANES, t), I32))
        return jnp.sum(cnt, axis=0, keepdims=True)

    def select_threshold(pairs):
        hi_u, n_ge_hi = bisect16(hi_ref, k_top, jnp.full((1, t), nchunk * t, I32), pairs)
        hi_b = jnp.broadcast_to((hi_u - 0x8000).astype(I16), (PACKED_ROWS, t))
        n_gt_hi = count16(hi_ref, lambda v: v > hi_b[None], pairs)
        for cc in range(pairs):
            lo_ref[2 * t * cc:2 * t * (cc + 1), :] = jnp.where(
                packed_rows(hi_ref, cc) == hi_b[None], packed_rows(lo_ref, cc),
                jnp.full((), -0x8000, I16)).reshape(2 * t, t)
        lo_u, n_ge_lo = bisect16(lo_ref, k_top - n_gt_hi, n_ge_hi - n_gt_hi, pairs)
        tau = (hi_u - 0x8000) * 65536 + lo_u
        n_ge = n_gt_hi + n_ge_lo
        tau_ref[...] = jnp.broadcast_to(jnp.maximum(tau, MASK_KEY + 1), (SUBLANES, t))

        tie = jnp.logical_and(n_ge > k_top, tau > MASK_KEY)

        @pl.when(jnp.max(jnp.where(tie, 1, 0)) > 0)
        def _():
            tau_b = jnp.broadcast_to(tau, (SUBLANES, t))
            n_gt = count(lambda kk, base: kk > tau_b[None])
            want = k_top - n_gt
            row = lax.broadcasted_iota(I32, (2 * groups, SUBLANES, t), 0) * SUBLANES + \
                lax.broadcasted_iota(I32, (2 * groups, SUBLANES, t), 1)

            def pos_body(i, pos):
                cand = pos | jnp.left_shift(jnp.int32(1), 15 - i)
                cand_b = jnp.broadcast_to(cand, (SUBLANES, t))
                cnt = count(lambda kk, base: jnp.logical_and(kk == tau_b[None],
                                                            row + base < cand_b[None]))
                return jnp.where(cnt < want, cand, pos)

            pos = lax.fori_loop(0, 16, pos_body, jnp.zeros((1, t), I32))
            pos_b = jnp.broadcast_to(jnp.where(tie, pos, jnp.int32(2 ** 30)), (SUBLANES, t))

            def demote_body(cc, carry):
                start = pl.multiple_of(cc * (2 * t), 2 * t)
                kk = keys_ref[pl.ds(start, 2 * t), :].reshape(2 * groups, SUBLANES, t)
                drop = jnp.logical_and(kk == tau_b[None], row + cc * (2 * t) > pos_b[None])
                keys_ref[pl.ds(start, 2 * t), :] = jnp.where(drop, MASK_KEY, kk).reshape(2 * t, t)
                return carry

            lax.fori_loop(0, npair, demote_body, 0)

    max_pairs = (keys_ref.shape[0] // t) // 2
    for pairs in range(1, max_pairs + 1):
        pl.when(jnp.logical_and(npair == pairs, nchunk * t > k_top))(
            functools.partial(select_threshold, pairs))

    m_ref[...] = jnp.full(m_ref.shape, -jnp.inf, F32)
    acc_ref[...] = jnp.zeros(acc_ref.shape, F32)
    tau_sel = tau_ref[...]
    ones_rows = jnp.ones((PACKED_ROWS, t), BF16)

    def set_mask(c):
        kk = keys_ref[pl.ds(chunk_start(c), t), :].reshape(groups, SUBLANES, t)
        mask_ref[...] = jnp.where(kk >= tau_sel[None], 0.0, -jnp.inf).reshape(t, t)

    FAR, NEAR, DIAG = None, 1, 0

    def logits_matmul(c, slot, h, kind):
        kvc = kv_ref[0, pl.ds(chunk_start(c), t), :]
        s = _dot(kvc, q_ref[0, h * KV_LORA:(h + 1) * KV_LORA, :]) + mask_ref[...]
        if kind is not FAR:
            s = s + btile_ref[kind, h]
        s2_refs[slot][h] = s
        cmax_refs[slot][h] = jnp.max(s.reshape(groups, SUBLANES, t), axis=0)

    def attend_step(c, cur, nxt, next_kind):
        kvtc = jnp.concatenate([kvt_ref[0, :, pl.ds(chunk_start(c), t)], ones_rows], axis=0)
        if nxt is not None:
            set_mask(c + 1)
        pending = []

        def flush():
            h, alpha, pv = pending.pop(0)
            acc_ref[h] = alpha * acc_ref[h] + pv

        for h in range(N_HEADS):
            if nxt is not None:
                logits_matmul(c + 1, nxt, h, next_kind)
            s = s2_refs[cur][h]
            cmax = jnp.max(cmax_refs[cur][h], axis=0, keepdims=True)
            m_old = m_ref[h]
            m_new = jnp.maximum(m_old, cmax)
            m_safe = jnp.where(m_new == -jnp.inf, 0.0, m_new)
            alpha = jnp.exp2(m_old - m_safe)
            p = jnp.exp2(s - m_safe)
            m_ref[h] = m_new
            pending.append((h, alpha, _dot(kvtc, p.astype(BF16))))
            if len(pending) > PV_DELAY:
                flush()
        while pending:
            flush()

    def first_logits(kind):
        for h in range(N_HEADS):
            logits_matmul(0, 0, h, kind)

    set_mask(0)
    for first_kind, when in ((DIAG, j == 0), (NEAR, j == 1), (FAR, j >= 2)):
        pl.when(when)(functools.partial(first_logits, first_kind))
    chunk_loop(jnp.maximum(j - 2, 0), functools.partial(attend_step, next_kind=FAR))
    pl.when(j >= 2)(lambda: by_parity(
        j - 2, lambda cur, nxt: attend_step(j - 2, cur, nxt, NEAR)))
    pl.when(j >= 1)(lambda: by_parity(
        j - 1, lambda cur, nxt: attend_step(j - 1, cur, nxt, DIAG)))
    by_parity(j, lambda cur, nxt: attend_step(j, cur, None, None))

    ys = []
    for h in range(N_HEADS):
        denom = acc_ref[h, KV_LORA:KV_LORA + 1, :]
        o_t = (acc_ref[h, 0:KV_LORA, :] * (1.0 / denom)).astype(BF16)
        ys.append(_dot_tn(o_t, wuv_ref[h]))
    y = jnp.concatenate(ys, axis=-1)
    o_ref[0] = (y * sga_ref[0]).astype(BF16)


def _sparse_attention(q, qi, wt, kv, kvt, ki, sga, w_uv, btiles, *, t):
    bsz, s, _ = kv.shape
    k_top = min(TOPK_MAX, s // 4)
    assert (s + t) // PACKED_ROWS <= 256, "per-position key counts must stay exact in bf16"
    blk = lambda w: pl.BlockSpec((1, t, w), lambda b, j: (b, j, 0))
    cols = lambda rows: pl.BlockSpec((1, rows, t), lambda b, j: (b, 0, j))
    in_specs = [
        cols(N_HEADS * KV_LORA),
        cols(N_IDX_HEADS * D_IDX),
        cols(N_IDX_HEADS),
        pl.BlockSpec((1, s, KV_LORA), lambda b, j: (b, 0, 0)),
        pl.BlockSpec((1, KV_LORA, s), lambda b, j: (b, 0, 0)),
        pl.BlockSpec((1, s, D_IDX), lambda b, j: (b, 0, 0)),
        blk(D_ATTN),
        pl.BlockSpec(w_uv.shape, lambda b, j: (0, 0, 0)),
        pl.BlockSpec(btiles.shape, lambda b, j: (0, 0, 0, 0)),
    ]
    return pl.pallas_call(
        functools.partial(_attn_kernel, t=t, k_top=k_top),
        name="sparse_attention",
        out_shape=jax.ShapeDtypeStruct((bsz, s, D_ATTN), BF16),
        grid=(bsz, s // t),
        in_specs=in_specs,
        out_specs=blk(D_ATTN),
        scratch_shapes=[
            pltpu.VMEM((s + t, t), I32),
            pltpu.VMEM((D_IDX, N_IDX_HEADS * t), BF16),
            pltpu.VMEM((SUBLANES, t), I32),
            pltpu.VMEM((N_HEADS, 1, t), F32),
            pltpu.VMEM((t, t), F32),
            pltpu.VMEM((N_HEADS, KV_LORA + PACKED_ROWS, t), F32),
            pltpu.VMEM((t, N_IDX_HEADS * t), F32),
            pltpu.VMEM((t, N_IDX_HEADS * t), F32),
            pltpu.VMEM((N_HEADS, t, t), F32),
            pltpu.VMEM((N_HEADS, t, t), F32),
            pltpu.VMEM((s + t, t), I16),
            pltpu.VMEM((s + t, t), I16),
            pltpu.VMEM((N_HEADS, SUBLANES, t), F32),
            pltpu.VMEM((N_HEADS, SUBLANES, t), F32),
        ],
        compiler_params=pltpu.CompilerParams(
            dimension_semantics=("arbitrary", "arbitrary"), vmem_limit_bytes=VMEM_LIMIT),
    )(q, qi, wt, kv, kvt, ki, sga, w_uv, btiles)


def _outproj_kernel(x_ref, yc_ref, ya_ref, wo_ref, mod_ref, gpost_ref, o_ref, *, d_model):
    y = _dot(jnp.concatenate([yc_ref[0], ya_ref[0]], axis=-1), wo_ref[...])
    yn = (y * lax.rsqrt(jnp.mean(y * y, axis=-1, keepdims=True) + EPS)) * gpost_ref[...]
    gate = mod_ref[0, :, 2 * d_model:3 * d_model]
    o_ref[0] = x_ref[0] + gate * yn


def _out_projection(x, yc, ya, w_out, mod_l, g_post, *, tm):
    bsz, s, d = x.shape
    assert s % tm == 0
    return pl.pallas_call(
        functools.partial(_outproj_kernel, d_model=d),
        name="out_projection",
        out_shape=jax.ShapeDtypeStruct((bsz, s, d), F32),
        grid=(bsz, s // tm),
        in_specs=[
            pl.BlockSpec((1, tm, d), lambda b, j: (b, j, 0)),
            pl.BlockSpec((1, tm, D_CONV), lambda b, j: (b, j, 0)),
            pl.BlockSpec((1, tm, D_ATTN), lambda b, j: (b, j, 0)),
            pl.BlockSpec(w_out.shape, lambda b, j: (0, 0)),
            pl.BlockSpec((1, 1, 3 * d), lambda b, j: (b, 0, 0)),
            pl.BlockSpec((1, d), lambda b, j: (0, 0)),
        ],
        out_specs=pl.BlockSpec((1, tm, d), lambda b, j: (b, j, 0)),
        compiler_params=pltpu.CompilerParams(
            dimension_semantics=("arbitrary", "arbitrary"), vmem_limit_bytes=VMEM_LIMIT),
    )(x, yc, ya, w_out, mod_l, g_post.reshape(1, d))


def _pack_w_in(w_in_l):
    d = w_in_l.shape[0]
    split = OFF_KW + D_IDX + N_IDX_HEADS
    return jnp.concatenate(
        [w_in_l[:, :split], jnp.zeros((d, KW_PAD), w_in_l.dtype), w_in_l[:, split:]],
        axis=-1).astype(BF16)


def kernel(x, c, w_ada, b_ada, g_pre, w_in, conv_w, conv_b, conv_ln_g, conv_ln_b, w_pw2, q_norm_g,
           w_uq, w_qidx, kv_norm_g, w_uv, rel_bias, w_out, g_post):
    depth = w_ada.shape[0]
    bsz, s, d = x.shape
    tm = min(512, s)
    t = min(ATTN_BLOCK, s)
    assert s % tm == 0 and s % t == 0 and d % LANES == 0
    assert tm % INPROJ_SUB == 0 and INPROJ_SUB % CONV_ROWS == 0

    mod = _modulation(c, w_ada, b_ada)
    btiles = _bias_tiles(rel_bias, t)
    for l in range(depth):
        mod_l = mod[l].reshape(bsz, 1, 3 * d)
        yc, q, qi, kv, kvt, ki, wt, sga = _in_projection(
            x, mod_l, g_pre[l], _pack_w_in(w_in[l]), conv_w[l], conv_b[l], conv_ln_g[l],
            conv_ln_b[l], w_pw2[l].astype(BF16), q_norm_g[l], w_uq[l].astype(BF16),
            w_qidx[l].astype(BF16), kv_norm_g[l], tm=tm)
        ya = _sparse_attention(q, qi, wt, kv, kvt, ki, sga, w_uv[l].astype(BF16), btiles, t=t)
        x = _out_projection(x, yc, ya, w_out[l].astype(BF16), mod_l, g_post[l],
                            tm=min(OUTPROJ_ROWS, s))
    return x
```

```python
import functools
import math

import jax
import jax.numpy as jnp
from jax import lax
from jax.experimental import pallas as pl
from jax.experimental.pallas import tpu as pltpu

F32 = jnp.float32
BF16 = jnp.bfloat16
I32 = jnp.int32
I16 = jnp.int16

D_CONV = 512
CONV_WIDTH = 31
N_HEADS = 8
D_HEAD_OUT = 64
D_ATTN = N_HEADS * D_HEAD_OUT
Q_LORA = 256
KV_LORA = 128
N_IDX_HEADS = 8
D_IDX = 64
TOPK_MAX = 256
N_BUCKETS = 32
MAX_DISTANCE = 128
EPS = 1e-6

KW_PAD = 128 - D_IDX - N_IDX_HEADS
OFF_UVAL = 0
OFF_UGATE = OFF_UVAL + D_CONV
OFF_GCONV = OFF_UGATE + D_CONV
OFF_CQ = OFF_GCONV + D_CONV
OFF_CKV = OFF_CQ + Q_LORA
OFF_KW = OFF_CKV + KV_LORA
OFF_GATTN = OFF_KW + 128
D_IN_PACKED = OFF_GATTN + D_ATTN

HALO = 32
CONV_ROWS = 64
INPROJ_SUB = 256
OUTPROJ_ROWS = 1024
ATTN_BLOCK = 256
PV_DELAY = 2
SCORE_ROWS = 32
LANES = 128
SUBLANES = 8
PACKED_ROWS = 2 * SUBLANES
VMEM_LIMIT = 48 * 1024 * 1024

INT_MIN = -(2 ** 31)
MASK_KEY = INT_MIN + 1
LOG2E = math.log2(math.e)


def _dot(a, b):
    return jnp.dot(a, b, preferred_element_type=F32)


def _dot_nt(a, b):
    return lax.dot_general(a, b, (((1,), (1,)), ((), ())), preferred_element_type=F32)


def _dot_tn(a, b):
    return lax.dot_general(a, b, (((0,), (0,)), ((), ())), preferred_element_type=F32)


def _silu(v):
    return v * jax.nn.sigmoid(v)


def _mod_kernel(c_ref, w_ref, b_ref, o_ref):
    c = c_ref[...]
    o_ref[0] = _dot(_silu(c).astype(BF16), w_ref[0].astype(BF16)) + b_ref[0]


def _modulation(c, w_ada, b_ada):
    depth, d, d3 = w_ada.shape
    bsz = c.shape[0]
    tn = 1024
    return pl.pallas_call(
        _mod_kernel,
        name="adaln_mod",
        out_shape=jax.ShapeDtypeStruct((depth, bsz, d3), F32),
        grid=(depth, d3 // tn),
        in_specs=[
            pl.BlockSpec((bsz, d), lambda l, n: (0, 0)),
            pl.BlockSpec((1, d, tn), lambda l, n: (l, 0, n)),
            pl.BlockSpec((1, 1, tn), lambda l, n: (l, 0, n)),
        ],
        out_specs=pl.BlockSpec((1, bsz, tn), lambda l, n: (l, 0, n)),
        compiler_params=pltpu.CompilerParams(
            dimension_semantics=("arbitrary", "arbitrary"), vmem_limit_bytes=VMEM_LIMIT),
    )(c, w_ada, b_ada.reshape(depth, 1, d3))


def _t5_bucket(n):
    max_exact = N_BUCKETS // 2
    n = jnp.maximum(n, 0)
    nf = jnp.maximum(n, 1).astype(F32)
    large = max_exact + (jnp.log(nf / max_exact) / math.log(MAX_DISTANCE / max_exact)
                         * (N_BUCKETS - max_exact)).astype(I32)
    large = jnp.minimum(large, N_BUCKETS - 1)
    return jnp.where(n < max_exact, n, large)


def _bias_kernel(rb_ref, o_ref, *, t):
    sk = lax.broadcasted_iota(I32, (t, t), 0)
    tq = lax.broadcasted_iota(I32, (t, t), 1)
    for kind in range(2):
        bucket = _t5_bucket(tq - sk + kind * t)
        for h in range(N_HEADS):
            val = jnp.zeros((t, t), F32)
            for b in range(N_BUCKETS):
                val = jnp.where(bucket == b, rb_ref[b, h], val)
            o_ref[kind, h] = (val - rb_ref[N_BUCKETS - 1, h]) * LOG2E


def _bias_tiles(rel_bias, t):
    return pl.pallas_call(
        functools.partial(_bias_kernel, t=t),
        name="t5_bias_tiles",
        out_shape=jax.ShapeDtypeStruct((2, N_HEADS, t, t), F32),
        in_specs=[pl.BlockSpec(memory_space=pltpu.SMEM)],
        out_specs=pl.BlockSpec(memory_space=pltpu.VMEM),
    )(rel_bias)


def _inproj_kernel(x_ref, mod_ref, gpre_ref, w_ref, cw_ref, cb_ref, lng_ref, lnb_ref, wpw2_ref,
                   qg_ref, wuq_ref, wqi_ref, kvg_ref,
                   yc_ref, q_ref, qi_ref, kv_ref, kvt_ref, ki_ref, wt_ref, sga_ref,
                   hbuf, abuf, ybuf, shbuf, gcbuf, *, tm, d_model):
    j = pl.program_id(1)

    @pl.when(j == 0)
    def _():
        abuf[0:HALO, :] = jnp.zeros((HALO, D_CONV), F32)

    shift = mod_ref[0, :, 0:d_model]
    scale = mod_ref[0, :, d_model:2 * d_model]
    first = HALO - (CONV_WIDTH - 1)

    for base in range(0, tm, INPROJ_SUB):
        rows = slice(base, base + INPROJ_SUB)
        x = x_ref[0, rows, :]
        ms = jnp.mean(x * x, axis=-1, keepdims=True)
        xn = (x * lax.rsqrt(ms + EPS)) * gpre_ref[...]
        hbuf[rows, :] = (xn * (1.0 + scale) + shift).astype(BF16)
        h = hbuf[rows, :]

        u_val = _dot(h, w_ref[:, OFF_UVAL:OFF_UVAL + D_CONV])
        u_gate = _dot(h, w_ref[:, OFF_UGATE:OFF_UGATE + D_CONV])
        abuf[HALO + base:HALO + base + INPROJ_SUB, :] = u_val * jax.nn.sigmoid(u_gate)

        span = slice(base, base + INPROJ_SUB + HALO - SUBLANES)
        for r in range(1, SUBLANES):
            shbuf[r - 1, span, :] = abuf[base + r:base + r + INPROJ_SUB + HALO - SUBLANES, :]

        gcbuf[rows, :] = _silu(_dot(h, w_ref[:, OFF_GCONV:OFF_GCONV + D_CONV]))

        c_q = _dot(h, w_ref[:, OFF_CQ:OFF_CQ + Q_LORA])
        cq = (c_q * lax.rsqrt(jnp.mean(c_q * c_q, axis=-1, keepdims=True) + EPS)) * qg_ref[...]
        cq = cq.astype(BF16)
        q = _dot(cq, wuq_ref[...]) * (KV_LORA ** -0.5 * LOG2E)
        q_ref[0, :, rows] = jnp.transpose(q).astype(BF16)
        qi_ref[0, :, rows] = jnp.transpose(_dot(cq, wqi_ref[...])).astype(BF16)

        c_kv = _dot(h, w_ref[:, OFF_CKV:OFF_CKV + KV_LORA])
        kvn = (c_kv * lax.rsqrt(jnp.mean(c_kv * c_kv, axis=-1, keepdims=True) + EPS)) * kvg_ref[...]
        kv_ref[0, rows, :] = kvn.astype(BF16)
        kvt_ref[0, :, rows] = jnp.transpose(kvn).astype(BF16)

        kw = _dot(h, w_ref[:, OFF_KW:OFF_KW + 128])
        ki_ref[0, rows, :] = kw[:, 0:D_IDX].astype(BF16)
        kwt = jnp.transpose(kw)
        wt_ref[0, :, rows] = kwt[D_IDX:D_IDX + N_IDX_HEADS, :] * (N_IDX_HEADS ** -0.5 * D_IDX ** -0.5)

        sga_ref[0, rows, :] = _silu(_dot(h, w_ref[:, OFF_GATTN:OFF_GATTN + D_ATTN]))

    for base in range(0, tm, INPROJ_SUB):
        rows = slice(base, base + INPROJ_SUB)
        for r0 in range(base, base + INPROJ_SUB, CONV_ROWS):
            acc = jnp.broadcast_to(cb_ref[...], (CONV_ROWS, D_CONV))
            for k in range(CONV_WIDTH):
                q8, r = divmod(first + k, SUBLANES)
                taps = slice(r0 + q8 * SUBLANES, r0 + q8 * SUBLANES + CONV_ROWS)
                tap = abuf[taps, :] if r == 0 else shbuf[r - 1, taps, :]
                acc = acc + cw_ref[k:k + 1, :] * tap
            mu = jnp.mean(acc, axis=-1, keepdims=True)
            cen = acc - mu
            var = jnp.mean(cen * cen, axis=-1, keepdims=True)
            yn = cen * lax.rsqrt(var + EPS) * lng_ref[...] + lnb_ref[...]
            ybuf[r0:r0 + CONV_ROWS, :] = _silu(yn).astype(BF16)

        yc_ref[0, rows, :] = (_dot(ybuf[rows, :], wpw2_ref[...]) * gcbuf[rows, :]).astype(BF16)

    abuf[0:HALO, :] = abuf[tm:tm + HALO, :]


def _layer_operand(stacked, layer):
    if stacked.ndim == 2:
        stacked = stacked.reshape(stacked.shape[0], 1, stacked.shape[1])
    rest = stacked.shape[1:]
    return stacked, pl.BlockSpec((None,) + rest, lambda b, j: (layer,) + (0,) * len(rest))


def _mod_operand(mod, layer):
    depth, bsz, d3 = mod.shape
    return (mod.reshape(depth, bsz, 1, d3),
            pl.BlockSpec((None, 1, 1, d3), lambda b, j: (layer, b, 0, 0)))


def _in_projection(x, mod, layer, g_pre, w_packed, conv_w, conv_b, ln_g, ln_b, w_pw2, q_g, w_uq,
                   w_qi, kv_g, *, tm):
    bsz, s, d = x.shape
    operands = [_mod_operand(mod, layer)] + [
        _layer_operand(a, layer) for a in (g_pre, w_packed, conv_w, conv_b, ln_g, ln_b, w_pw2,
                                           q_g, w_uq, w_qi, kv_g)]
    args = (x,) + tuple(a for a, _ in operands)
    in_specs = [pl.BlockSpec((1, tm, d), lambda b, j: (b, j, 0))] + [sp for _, sp in operands]
    blk = lambda w: pl.BlockSpec((1, tm, w), lambda b, j: (b, j, 0))
    out_shape = (
        jax.ShapeDtypeStruct((bsz, s, D_CONV), BF16),
        jax.ShapeDtypeStruct((bsz, N_HEADS * KV_LORA, s), BF16),
        jax.ShapeDtypeStruct((bsz, N_IDX_HEADS * D_IDX, s), BF16),
        jax.ShapeDtypeStruct((bsz, s, KV_LORA), BF16),
        jax.ShapeDtypeStruct((bsz, KV_LORA, s), BF16),
        jax.ShapeDtypeStruct((bsz, s, D_IDX), BF16),
        jax.ShapeDtypeStruct((bsz, N_IDX_HEADS, s), F32),
        jax.ShapeDtypeStruct((bsz, s, D_ATTN), F32),
    )
    out_specs = (
        blk(D_CONV),
        pl.BlockSpec((1, N_HEADS * KV_LORA, tm), lambda b, j: (b, 0, j)),
        pl.BlockSpec((1, N_IDX_HEADS * D_IDX, tm), lambda b, j: (b, 0, j)),
        blk(KV_LORA),
        pl.BlockSpec((1, KV_LORA, tm), lambda b, j: (b, 0, j)),
        blk(D_IDX),
        pl.BlockSpec((1, N_IDX_HEADS, tm), lambda b, j: (b, 0, j)),
        blk(D_ATTN),
    )
    return pl.pallas_call(
        functools.partial(_inproj_kernel, tm=tm, d_model=d),
        name="in_projection",
        out_shape=out_shape,
        grid=(bsz, s // tm),
        in_specs=in_specs,
        out_specs=out_specs,
        scratch_shapes=[
            pltpu.VMEM((tm, d), BF16),
            pltpu.VMEM((tm + HALO, D_CONV), F32),
            pltpu.VMEM((tm, D_CONV), BF16),
            pltpu.VMEM((SUBLANES - 1, tm + HALO, D_CONV), F32),
            pltpu.VMEM((tm, D_CONV), F32),
        ],
        compiler_params=pltpu.CompilerParams(
            dimension_semantics=("arbitrary", "arbitrary"), vmem_limit_bytes=VMEM_LIMIT),
    )(*args)


def _sortable(v):
    b = pltpu.bitcast(v, I32)
    return jnp.where(b < 0, INT_MIN - b, b)


def _attn_kernel(q_ref, qi_ref, wt_ref, kv_ref, kvt_ref, ki_ref, sga_ref, wuv_ref, btile_ref,
                 o_ref, keys_ref, qis_ref, tau_ref, m_ref, mask_ref, acc_ref, sa0_ref, sa1_ref,
                 s2a_ref, s2b_ref, hi_ref, lo_ref, cmaxa_ref, cmaxb_ref, *, t, k_top):
    sa_refs = (sa0_ref, sa1_ref)
    s2_refs = (s2a_ref, s2b_ref)
    cmax_refs = (cmaxa_ref, cmaxb_ref)
    j = pl.program_id(1)
    nchunk = j + 1
    npair = (j + 2) // 2
    groups = t // SUBLANES

    def chunk_start(c):
        return pl.multiple_of(c * t, t)

    for h in range(N_IDX_HEADS):
        qis_ref[:, h * t:(h + 1) * t] = qi_ref[0, h * D_IDX:(h + 1) * D_IDX, :]

    def by_parity(c, fn):
        for parity, (cur, nxt) in enumerate(((0, 1), (1, 0))):
            pl.when((c & 1) == parity)(functools.partial(fn, cur, nxt))

    def chunk_loop(n, step):
        def pair_body(cc, carry):
            step(2 * cc, 0, 1)
            step(2 * cc + 1, 1, 0)
            return carry
        lax.fori_loop(0, n // 2, pair_body, 0)
        pl.when(n % 2 == 1)(lambda: step(n - 1, 0, 1))

    def score_matmul(c, dst):
        kc = ki_ref[0, pl.ds(chunk_start(c), t), :]
        dst[...] = _dot(kc, qis_ref[...])

    span = keys_ref.shape[0]
    neg_row = -1 - lax.broadcasted_iota(I32, (SCORE_ROWS, t), 0)

    def score_finish(src, c):
        blocks = []
        for r0 in range(0, t, SCORE_ROWS):
            acc = jnp.zeros((SCORE_ROWS, t), F32)
            for h in range(N_IDX_HEADS):
                s = src[r0:r0 + SCORE_ROWS, h * t:(h + 1) * t]
                acc = acc + wt_ref[0, h:h + 1, :] * jnp.maximum(s, 0.0)
            key = _sortable(acc)
            zero_rank = neg_row - (c * t + r0)
            blocks.append(jnp.where(key == 0, zero_rank, jnp.where(key < 0, key - span, key)))
        return jnp.concatenate(blocks, axis=0)

    def store_keys(c, key):
        keys_ref[pl.ds(chunk_start(c), t), :] = key
        hi_ref[pl.ds(chunk_start(c), t), :] = (key >> 16).astype(I16)
        lo_ref[pl.ds(chunk_start(c), t), :] = ((key & 0xFFFF) - 0x8000).astype(I16)

    def score_step(c, cur, nxt):
        score_matmul(c + 1, sa_refs[nxt])
        store_keys(c, score_finish(sa_refs[cur], c))

    score_matmul(0, sa_refs[0])
    chunk_loop(j, score_step)
    sk = lax.broadcasted_iota(I32, (t, t), 0)
    tq = lax.broadcasted_iota(I32, (t, t), 1)

    by_parity(j, lambda cur, nxt: store_keys(
        j, jnp.where(sk <= tq, score_finish(sa_refs[cur], j), MASK_KEY)))
    store_keys(j + 1, jnp.full((t, t), MASK_KEY, I32))

    tau_ref[...] = jnp.full((SUBLANES, t), MASK_KEY + 1, I32)

    def packed_rows(ref, cc):
        return ref[2 * t * cc:2 * t * (cc + 1), :].reshape(2 * t // PACKED_ROWS, PACKED_ROWS, t)

    def count16(plane_ref, pred, pairs):
        parts = []
        for cc in range(pairs):
            ones = jnp.where(pred(packed_rows(plane_ref, cc)), jnp.ones((), BF16),
                             jnp.zeros((), BF16))
            parts += [ones[i] for i in range(2 * t // PACKED_ROWS)]
        while len(parts) > 1:
            parts = [sum(parts[i:i + 2][1:], parts[i]) for i in range(0, len(parts), 2)]
        return jnp.sum(parts[0].astype(F32), axis=0, keepdims=True).astype(I32)

    def bisect16(plane_ref, want, n_all, pairs):
        def bit_body(i, carry):
            prefix, n_ge = carry
            cand = prefix | jnp.left_shift(jnp.int32(1), 15 - i)
            cand_b = jnp.broadcast_to((cand - 0x8000).astype(I16), (PACKED_ROWS, t))
            cnt = count16(plane_ref, lambda v: v >= cand_b[None], pairs)
            ok = cnt >= want
            return jnp.where(ok, cand, prefix), jnp.where(ok, cnt, n_ge)
        return lax.fori_loop(0, 16, bit_body, (jnp.zeros((1, t), I32), n_all))

    def count(pred):
        def body(cc, cnt):
            start = pl.multiple_of(cc * (2 * t), 2 * t)
            kk = keys_ref[pl.ds(start, 2 * t), :].reshape(2 * groups, SUBLANES, t)
            return cnt + jnp.sum(jnp.where(pred(kk, cc * (2 * t)), 1, 0), axis=0)
        cnt = lax.fori_loop(0, npair, body, jnp.zeros((SUBLANES, t), I32))
        return jnp.sum(cnt, axis=0, keepdims=True)

    def select_threshold(pairs):
        hi_u, n_ge_hi = bisect16(hi_ref, k_top, jnp.full((1, t), nchunk * t, I32), pairs)
        hi_b = jnp.broadcast_to((hi_u - 0x8000).astype(I16), (PACKED_ROWS, t))
        n_gt_hi = count16(hi_ref, lambda v: v > hi_b[None], pairs)
        for cc in range(pairs):
            lo_ref[2 * t * cc:2 * t * (cc + 1), :] = jnp.where(
                packed_rows(hi_ref, cc) == hi_b[None], packed_rows(lo_ref, cc),
                jnp.full((), -0x8000, I16)).reshape(2 * t, t)
        lo_u, n_ge_lo = bisect16(lo_ref, k_top - n_gt_hi, n_ge_hi - n_gt_hi, pairs)
        tau = (hi_u - 0x8000) * 65536 + lo_u
        n_ge = n_gt_hi + n_ge_lo
        tau_ref[...] = jnp.broadcast_to(jnp.maximum(tau, MASK_KEY + 1), (SUBLANES, t))

        tie = jnp.logical_and(n_ge > k_top, tau > MASK_KEY)

        @pl.when(jnp.max(jnp.where(tie, 1, 0)) > 0)
        def _():
            tau_b = jnp.broadcast_to(tau, (SUBLANES, t))
            n_gt = count(lambda kk, base: kk > tau_b[None])
            want = k_top - n_gt
            row = lax.broadcasted_iota(I32, (2 * groups, SUBLANES, t), 0) * SUBLANES + \
                lax.broadcasted_iota(I32, (2 * groups, SUBLANES, t), 1)

            def pos_body(i, pos):
                cand = pos | jnp.left_shift(jnp.int32(1), 15 - i)
                cand_b = jnp.broadcast_to(cand, (SUBLANES, t))
                cnt = count(lambda kk, base: jnp.logical_and(kk == tau_b[None],
                                                            row + base < cand_b[None]))
                return jnp.where(cnt < want, cand, pos)

            pos = lax.fori_loop(0, 16, pos_body, jnp.zeros((1, t), I32))
            pos_b = jnp.broadcast_to(jnp.where(tie, pos, jnp.int32(2 ** 30)), (SUBLANES, t))

            def demote_body(cc, carry):
                start = pl.multiple_of(cc * (2 * t), 2 * t)
                kk = keys_ref[pl.ds(start, 2 * t), :].reshape(2 * groups, SUBLANES, t)
                drop = jnp.logical_and(kk == tau_b[None], row + cc * (2 * t) > pos_b[None])
                keys_ref[pl.ds(start, 2 * t), :] = jnp.where(drop, MASK_KEY, kk).reshape(2 * t, t)
                return carry

            lax.fori_loop(0, npair, demote_body, 0)

    max_pairs = (keys_ref.shape[0] // t) // 2
    for pairs in range(1, max_pairs + 1):
        pl.when(jnp.logical_and(npair == pairs, nchunk * t > k_top))(
            functools.partial(select_threshold, pairs))

    m_ref[...] = jnp.full(m_ref.shape, -jnp.inf, F32)
    acc_ref[...] = jnp.zeros(acc_ref.shape, F32)
    tau_sel = tau_ref[...]
    ones_rows = jnp.ones((PACKED_ROWS, t), BF16)

    def set_mask(c):
        kk = keys_ref[pl.ds(chunk_start(c), t), :].reshape(groups, SUBLANES, t)
        mask_ref[...] = jnp.where(kk >= tau_sel[None], 0.0, -jnp.inf).reshape(t, t)

    FAR, NEAR, DIAG = None, 1, 0

    def logits_matmul(c, slot, h, kind):
        kvc = kv_ref[0, pl.ds(chunk_start(c), t), :]
        s = _dot(kvc, q_ref[0, h * KV_LORA:(h + 1) * KV_LORA, :]) + mask_ref[...]
        if kind is not FAR:
            s = s + btile_ref[kind, h]
        s2_refs[slot][h] = s
        cmax_refs[slot][h] = jnp.max(s.reshape(groups, SUBLANES, t), axis=0)

    def attend_step(c, cur, nxt, next_kind):
        kvtc = jnp.concatenate([kvt_ref[0, :, pl.ds(chunk_start(c), t)], ones_rows], axis=0)
        if nxt is not None:
            set_mask(c + 1)
        pending = []

        def flush():
            h, alpha, pv = pending.pop(0)
            acc_ref[h] = alpha * acc_ref[h] + pv

        for h in range(N_HEADS):
            if nxt is not None:
                logits_matmul(c + 1, nxt, h, next_kind)
            s = s2_refs[cur][h]
            cmax = jnp.max(cmax_refs[cur][h], axis=0, keepdims=True)
            m_old = m_ref[h]
            m_new = jnp.maximum(m_old, cmax)
            m_safe = jnp.where(m_new == -jnp.inf, 0.0, m_new)
            alpha = jnp.exp2(m_old - m_safe)
            p = jnp.exp2(s - m_safe)
            m_ref[h] = m_new
            pending.append((h, alpha, _dot(kvtc, p.astype(BF16))))
            if len(pending) > PV_DELAY:
                flush()
        while pending:
            flush()

    def first_logits(kind):
        for h in range(N_HEADS):
            logits_matmul(0, 0, h, kind)

    set_mask(0)
    for first_kind, when in ((DIAG, j == 0), (NEAR, j == 1), (FAR, j >= 2)):
        pl.when(when)(functools.partial(first_logits, first_kind))
    chunk_loop(jnp.maximum(j - 2, 0), functools.partial(attend_step, next_kind=FAR))
    pl.when(j >= 2)(lambda: by_parity(
        j - 2, lambda cur, nxt: attend_step(j - 2, cur, nxt, NEAR)))
    pl.when(j >= 1)(lambda: by_parity(
        j - 1, lambda cur, nxt: attend_step(j - 1, cur, nxt, DIAG)))
    by_parity(j, lambda cur, nxt: attend_step(j, cur, None, None))

    ys = []
    for h in range(N_HEADS):
        denom = acc_ref[h, KV_LORA:KV_LORA + 1, :]
        o_t = (acc_ref[h, 0:KV_LORA, :] * (1.0 / denom)).astype(BF16)
        ys.append(_dot_tn(o_t, wuv_ref[h]))
    y = jnp.concatenate(ys, axis=-1)
    o_ref[0] = (y * sga_ref[0]).astype(BF16)


def _sparse_attention(q, qi, wt, kv, kvt, ki, sga, w_uv, layer, btiles, *, t):
    w_uv, w_uv_spec = _layer_operand(w_uv, layer)
    bsz, s, _ = kv.shape
    k_top = min(TOPK_MAX, s // 4)
    assert (s + t) // PACKED_ROWS <= 256, "per-position key counts must stay exact in bf16"
    blk = lambda w: pl.BlockSpec((1, t, w), lambda b, j: (b, j, 0))
    cols = lambda rows: pl.BlockSpec((1, rows, t), lambda b, j: (b, 0, j))
    in_specs = [
        cols(N_HEADS * KV_LORA),
        cols(N_IDX_HEADS * D_IDX),
        cols(N_IDX_HEADS),
        pl.BlockSpec((1, s, KV_LORA), lambda b, j: (b, 0, 0)),
        pl.BlockSpec((1, KV_LORA, s), lambda b, j: (b, 0, 0)),
        pl.BlockSpec((1, s, D_IDX), lambda b, j: (b, 0, 0)),
        blk(D_ATTN),
        w_uv_spec,
        pl.BlockSpec(btiles.shape, lambda b, j: (0, 0, 0, 0)),
    ]
    return pl.pallas_call(
        functools.partial(_attn_kernel, t=t, k_top=k_top),
        name="sparse_attention",
        out_shape=jax.ShapeDtypeStruct((bsz, s, D_ATTN), BF16),
        grid=(bsz, s // t),
        in_specs=in_specs,
        out_specs=blk(D_ATTN),
        scratch_shapes=[
            pltpu.VMEM((s + t, t), I32),
            pltpu.VMEM((D_IDX, N_IDX_HEADS * t), BF16),
            pltpu.VMEM((SUBLANES, t), I32),
            pltpu.VMEM((N_HEADS, 1, t), F32),
            pltpu.VMEM((t, t), F32),
            pltpu.VMEM((N_HEADS, KV_LORA + PACKED_ROWS, t), F32),
            pltpu.VMEM((t, N_IDX_HEADS * t), F32),
            pltpu.VMEM((t, N_IDX_HEADS * t), F32),
            pltpu.VMEM((N_HEADS, t, t), F32),
            pltpu.VMEM((N_HEADS, t, t), F32),
            pltpu.VMEM((s + t, t), I16),
            pltpu.VMEM((s + t, t), I16),
            pltpu.VMEM((N_HEADS, SUBLANES, t), F32),
            pltpu.VMEM((N_HEADS, SUBLANES, t), F32),
        ],
        compiler_params=pltpu.CompilerParams(
            dimension_semantics=("arbitrary", "arbitrary"), vmem_limit_bytes=VMEM_LIMIT),
    )(q, qi, wt, kv, kvt, ki, sga, w_uv, btiles)


def _outproj_kernel(x_ref, yc_ref, ya_ref, wo_ref, mod_ref, gpost_ref, o_ref, *, d_model):
    y = _dot(jnp.concatenate([yc_ref[0], ya_ref[0]], axis=-1), wo_ref[...])
    yn = (y * lax.rsqrt(jnp.mean(y * y, axis=-1, keepdims=True) + EPS)) * gpost_ref[...]
    gate = mod_ref[0, :, 2 * d_model:3 * d_model]
    o_ref[0] = x_ref[0] + gate * yn


def _out_projection(x, yc, ya, w_out, mod, g_post, layer, *, tm):
    bsz, s, d = x.shape
    assert s % tm == 0
    (w_out, w_spec), (mod, mod_spec), (g_post, g_spec) = (
        _layer_operand(w_out, layer), _mod_operand(mod, layer), _layer_operand(g_post, layer))
    return pl.pallas_call(
        functools.partial(_outproj_kernel, d_model=d),
        name="out_projection",
        out_shape=jax.ShapeDtypeStruct((bsz, s, d), F32),
        grid=(bsz, s // tm),
        in_specs=[
            pl.BlockSpec((1, tm, d), lambda b, j: (b, j, 0)),
            pl.BlockSpec((1, tm, D_CONV), lambda b, j: (b, j, 0)),
            pl.BlockSpec((1, tm, D_ATTN), lambda b, j: (b, j, 0)),
            w_spec, mod_spec, g_spec,
        ],
        out_specs=pl.BlockSpec((1, tm, d), lambda b, j: (b, j, 0)),
        compiler_params=pltpu.CompilerParams(
            dimension_semantics=("arbitrary", "arbitrary"), vmem_limit_bytes=VMEM_LIMIT),
    )(x, yc, ya, w_out, mod, g_post)


def _pack_w_in(w_in):
    split = OFF_KW + D_IDX + N_IDX_HEADS
    pad = jnp.zeros(w_in.shape[:-1] + (KW_PAD,), BF16)
    return jnp.concatenate(
        [w_in[..., :split].astype(BF16), pad, w_in[..., split:].astype(BF16)], axis=-1)


def kernel(x, c, w_ada, b_ada, g_pre, w_in, conv_w, conv_b, conv_ln_g, conv_ln_b, w_pw2, q_norm_g,
           w_uq, w_qidx, kv_norm_g, w_uv, rel_bias, w_out, g_post):
    depth = w_ada.shape[0]
    bsz, s, d = x.shape
    tm = min(512, s)
    t = min(ATTN_BLOCK, s)
    assert s % tm == 0 and s % t == 0 and d % LANES == 0
    assert tm % INPROJ_SUB == 0 and INPROJ_SUB % CONV_ROWS == 0

    mod = _modulation(c, w_ada, b_ada)
    btiles = _bias_tiles(rel_bias, t)
    w_packed = _pack_w_in(w_in)
    w_pw2, w_uq, w_qidx, w_uv, w_out = (a.astype(BF16) for a in (w_pw2, w_uq, w_qidx, w_uv, w_out))
    for l in range(depth):
        yc, q, qi, kv, kvt, ki, wt, sga = _in_projection(
            x, mod, l, g_pre, w_packed, conv_w, conv_b, conv_ln_g, conv_ln_b, w_pw2, q_norm_g,
            w_uq, w_qidx, kv_norm_g, tm=tm)
        ya = _sparse_attention(q, qi, wt, kv, kvt, ki, sga, w_uv, l, btiles, t=t)
        x = _out_projection(x, yc, ya, w_out, mod, g_post, l, tm=min(OUTPROJ_ROWS, s))
    return x
```

```python
import functools
import math

import jax
import jax.numpy as jnp
from jax import lax
from jax.experimental import pallas as pl
from jax.experimental.pallas import tpu as pltpu

F32 = jnp.float32
BF16 = jnp.bfloat16
I32 = jnp.int32
I16 = jnp.int16

D_CONV = 512
CONV_WIDTH = 31
N_HEADS = 8
D_HEAD_OUT = 64
D_ATTN = N_HEADS * D_HEAD_OUT
Q_LORA = 256
KV_LORA = 128
N_IDX_HEADS = 8
D_IDX = 64
TOPK_MAX = 256
N_BUCKETS = 32
MAX_DISTANCE = 128
EPS = 1e-6

LANES = 128
SUBLANES = 8
PACKED_ROWS = 2 * SUBLANES

KW_WIDTH = LANES
KW_PAD = KW_WIDTH - D_IDX - N_IDX_HEADS
OFF_UVAL = 0
OFF_UGATE = OFF_UVAL + D_CONV
OFF_GCONV = OFF_UGATE + D_CONV
OFF_CQ = OFF_GCONV + D_CONV
OFF_CKV = OFF_CQ + Q_LORA
OFF_KW = OFF_CKV + KV_LORA
OFF_GATTN = OFF_KW + KW_WIDTH

HALO = 32
CONV_ROWS = 64
INPROJ_SUB = 256
OUTPROJ_ROWS = 1024
ATTN_BLOCK = 256
PV_DELAY = 2
SCORE_ROWS = 32
VMEM_LIMIT = 48 * 1024 * 1024

INT_MIN = -(2 ** 31)
MASK_KEY = INT_MIN + 1
LOG2E = math.log2(math.e)


def _dot(a, b):
    return jnp.dot(a, b, preferred_element_type=F32)


def _dot_tn(a, b):
    return lax.dot_general(a, b, (((0,), (0,)), ((), ())), preferred_element_type=F32)


def _silu(v):
    return v * jax.nn.sigmoid(v)


def _mod_kernel(c_ref, w_ref, b_ref, o_ref):
    c = c_ref[...]
    o_ref[0] = _dot(_silu(c).astype(BF16), w_ref[0].astype(BF16)) + b_ref[0]


def _modulation(c, w_ada, b_ada):
    depth, d, d3 = w_ada.shape
    bsz = c.shape[0]
    tn = 1024
    return pl.pallas_call(
        _mod_kernel,
        name="adaln_mod",
        out_shape=jax.ShapeDtypeStruct((depth, bsz, d3), F32),
        grid=(depth, d3 // tn),
        in_specs=[
            pl.BlockSpec((bsz, d), lambda l, n: (0, 0)),
            pl.BlockSpec((1, d, tn), lambda l, n: (l, 0, n)),
            pl.BlockSpec((1, 1, tn), lambda l, n: (l, 0, n)),
        ],
        out_specs=pl.BlockSpec((1, bsz, tn), lambda l, n: (l, 0, n)),
        compiler_params=pltpu.CompilerParams(
            dimension_semantics=("arbitrary", "arbitrary"), vmem_limit_bytes=VMEM_LIMIT),
    )(c, w_ada, b_ada.reshape(depth, 1, d3))


def _t5_bucket(n):
    max_exact = N_BUCKETS // 2
    n = jnp.maximum(n, 0)
    nf = jnp.maximum(n, 1).astype(F32)
    large = max_exact + (jnp.log(nf / max_exact) / math.log(MAX_DISTANCE / max_exact)
                         * (N_BUCKETS - max_exact)).astype(I32)
    large = jnp.minimum(large, N_BUCKETS - 1)
    return jnp.where(n < max_exact, n, large)


def _bias_kernel(rb_ref, o_ref, *, t):
    sk = lax.broadcasted_iota(I32, (t, t), 0)
    tq = lax.broadcasted_iota(I32, (t, t), 1)
    for kind in range(2):
        bucket = _t5_bucket(tq - sk + kind * t)
        for h in range(N_HEADS):
            val = jnp.zeros((t, t), F32)
            for b in range(N_BUCKETS):
                val = jnp.where(bucket == b, rb_ref[b, h], val)
            o_ref[kind, h] = (val - rb_ref[N_BUCKETS - 1, h]) * LOG2E


def _bias_tiles(rel_bias, t):
    return pl.pallas_call(
        functools.partial(_bias_kernel, t=t),
        name="t5_bias_tiles",
        out_shape=jax.ShapeDtypeStruct((2, N_HEADS, t, t), F32),
        in_specs=[pl.BlockSpec(memory_space=pltpu.SMEM)],
        out_specs=pl.BlockSpec(memory_space=pltpu.VMEM),
    )(rel_bias)


def _inproj_kernel(x_ref, mod_ref, gpre_ref, w_ref, cw_ref, cb_ref, lng_ref, lnb_ref, wpw2_ref,
                   qg_ref, wuq_ref, wqi_ref, kvg_ref,
                   yc_ref, q_ref, qi_ref, kv_ref, kvt_ref, ki_ref, wt_ref, sga_ref,
                   hbuf, abuf, ybuf, shbuf, gcbuf, *, tm, d_model):
    j = pl.program_id(1)

    @pl.when(j == 0)
    def _():
        abuf[0:HALO, :] = jnp.zeros((HALO, D_CONV), F32)

    shift = mod_ref[0, :, 0:d_model]
    scale = mod_ref[0, :, d_model:2 * d_model]
    first = HALO - (CONV_WIDTH - 1)

    for base in range(0, tm, INPROJ_SUB):
        rows = slice(base, base + INPROJ_SUB)
        x = x_ref[0, rows, :]
        ms = jnp.mean(x * x, axis=-1, keepdims=True)
        xn = (x * lax.rsqrt(ms + EPS)) * gpre_ref[...]
        hbuf[rows, :] = (xn * (1.0 + scale) + shift).astype(BF16)
        h = hbuf[rows, :]

        u_val = _dot(h, w_ref[:, OFF_UVAL:OFF_UVAL + D_CONV])
        u_gate = _dot(h, w_ref[:, OFF_UGATE:OFF_UGATE + D_CONV])
        abuf[HALO + base:HALO + base + INPROJ_SUB, :] = u_val * jax.nn.sigmoid(u_gate)

        span = slice(base, base + INPROJ_SUB + HALO - SUBLANES)
        for r in range(1, SUBLANES):
            shbuf[r - 1, span, :] = abuf[base + r:base + r + INPROJ_SUB + HALO - SUBLANES, :]

        gcbuf[rows, :] = _silu(_dot(h, w_ref[:, OFF_GCONV:OFF_GCONV + D_CONV]))

        c_q = _dot(h, w_ref[:, OFF_CQ:OFF_CQ + Q_LORA])
        cq = (c_q * lax.rsqrt(jnp.mean(c_q * c_q, axis=-1, keepdims=True) + EPS)) * qg_ref[...]
        cq = cq.astype(BF16)
        q = _dot(cq, wuq_ref[...]) * (KV_LORA ** -0.5 * LOG2E)
        q_ref[0, :, rows] = jnp.transpose(q).astype(BF16)
        qi_ref[0, :, rows] = jnp.transpose(_dot(cq, wqi_ref[...])).astype(BF16)

        c_kv = _dot(h, w_ref[:, OFF_CKV:OFF_CKV + KV_LORA])
        kvn = (c_kv * lax.rsqrt(jnp.mean(c_kv * c_kv, axis=-1, keepdims=True) + EPS)) * kvg_ref[...]
        kv_ref[0, rows, :] = kvn.astype(BF16)
        kvt_ref[0, :, rows] = jnp.transpose(kvn).astype(BF16)

        kw = _dot(h, w_ref[:, OFF_KW:OFF_KW + KW_WIDTH])
        ki_ref[0, rows, :] = kw[:, 0:D_IDX].astype(BF16)
        kwt = jnp.transpose(kw)
        wt_ref[0, :, rows] = kwt[D_IDX:D_IDX + N_IDX_HEADS, :] * (N_IDX_HEADS ** -0.5 * D_IDX ** -0.5)

        sga_ref[0, rows, :] = _silu(_dot(h, w_ref[:, OFF_GATTN:OFF_GATTN + D_ATTN]))

    for base in range(0, tm, INPROJ_SUB):
        rows = slice(base, base + INPROJ_SUB)
        for r0 in range(base, base + INPROJ_SUB, CONV_ROWS):
            acc = jnp.broadcast_to(cb_ref[...], (CONV_ROWS, D_CONV))
            for k in range(CONV_WIDTH):
                q8, r = divmod(first + k, SUBLANES)
                taps = slice(r0 + q8 * SUBLANES, r0 + q8 * SUBLANES + CONV_ROWS)
                tap = abuf[taps, :] if r == 0 else shbuf[r - 1, taps, :]
                acc = acc + cw_ref[k:k + 1, :] * tap
            mu = jnp.mean(acc, axis=-1, keepdims=True)
            cen = acc - mu
            var = jnp.mean(cen * cen, axis=-1, keepdims=True)
            yn = cen * lax.rsqrt(var + EPS) * lng_ref[...] + lnb_ref[...]
            ybuf[r0:r0 + CONV_ROWS, :] = _silu(yn).astype(BF16)

        yc_ref[0, rows, :] = (_dot(ybuf[rows, :], wpw2_ref[...]) * gcbuf[rows, :]).astype(BF16)

    abuf[0:HALO, :] = abuf[tm:tm + HALO, :]


def _layer_operand(stacked, layer):
    if stacked.ndim == 2:
        stacked = stacked.reshape(stacked.shape[0], 1, stacked.shape[1])
    rest = stacked.shape[1:]
    return stacked, pl.BlockSpec((None,) + rest, lambda b, j: (layer,) + (0,) * len(rest))


def _mod_operand(mod, layer):
    depth, bsz, d3 = mod.shape
    return (mod.reshape(depth, bsz, 1, d3),
            pl.BlockSpec((None, 1, 1, d3), lambda b, j: (layer, b, 0, 0)))


def _in_projection(x, mod, layer, g_pre, w_packed, conv_w, conv_b, ln_g, ln_b, w_pw2, q_g, w_uq,
                   w_qi, kv_g, *, tm):
    bsz, s, d = x.shape
    operands = [_mod_operand(mod, layer)] + [
        _layer_operand(a, layer) for a in (g_pre, w_packed, conv_w, conv_b, ln_g, ln_b, w_pw2,
                                           q_g, w_uq, w_qi, kv_g)]
    args = (x,) + tuple(a for a, _ in operands)
    in_specs = [pl.BlockSpec((1, tm, d), lambda b, j: (b, j, 0))] + [sp for _, sp in operands]
    blk = lambda w: pl.BlockSpec((1, tm, w), lambda b, j: (b, j, 0))
    out_shape = (
        jax.ShapeDtypeStruct((bsz, s, D_CONV), BF16),
        jax.ShapeDtypeStruct((bsz, N_HEADS * KV_LORA, s), BF16),
        jax.ShapeDtypeStruct((bsz, N_IDX_HEADS * D_IDX, s), BF16),
        jax.ShapeDtypeStruct((bsz, s, KV_LORA), BF16),
        jax.ShapeDtypeStruct((bsz, KV_LORA, s), BF16),
        jax.ShapeDtypeStruct((bsz, s, D_IDX), BF16),
        jax.ShapeDtypeStruct((bsz, N_IDX_HEADS, s), F32),
        jax.ShapeDtypeStruct((bsz, s, D_ATTN), F32),
    )
    out_specs = (
        blk(D_CONV),
        pl.BlockSpec((1, N_HEADS * KV_LORA, tm), lambda b, j: (b, 0, j)),
        pl.BlockSpec((1, N_IDX_HEADS * D_IDX, tm), lambda b, j: (b, 0, j)),
        blk(KV_LORA),
        pl.BlockSpec((1, KV_LORA, tm), lambda b, j: (b, 0, j)),
        blk(D_IDX),
        pl.BlockSpec((1, N_IDX_HEADS, tm), lambda b, j: (b, 0, j)),
        blk(D_ATTN),
    )
    return pl.pallas_call(
        functools.partial(_inproj_kernel, tm=tm, d_model=d),
        name="in_projection",
        out_shape=out_shape,
        grid=(bsz, s // tm),
        in_specs=in_specs,
        out_specs=out_specs,
        scratch_shapes=[
            pltpu.VMEM((tm, d), BF16),
            pltpu.VMEM((tm + HALO, D_CONV), F32),
            pltpu.VMEM((tm, D_CONV), BF16),
            pltpu.VMEM((SUBLANES - 1, tm + HALO, D_CONV), F32),
            pltpu.VMEM((tm, D_CONV), F32),
        ],
        compiler_params=pltpu.CompilerParams(
            dimension_semantics=("arbitrary", "arbitrary"), vmem_limit_bytes=VMEM_LIMIT),
    )(*args)


def _sortable(v):
    b = pltpu.bitcast(v, I32)
    return jnp.where(b < 0, INT_MIN - b, b)


def _attn_kernel(q_ref, qi_ref, wt_ref, kv_ref, kvt_ref, ki_ref, sga_ref, wuv_ref, btile_ref,
                 o_ref, keys_ref, qis_ref, tau_ref, m_ref, mask_ref, acc_ref, sa0_ref, sa1_ref,
                 s2a_ref, s2b_ref, hi_ref, lo_ref, cmaxa_ref, cmaxb_ref, *, t, k_top):
    sa_refs = (sa0_ref, sa1_ref)
    s2_refs = (s2a_ref, s2b_ref)
    cmax_refs = (cmaxa_ref, cmaxb_ref)
    j = pl.program_id(1)
    nchunk = j + 1
    npair = (j + 2) // 2
    groups = t // SUBLANES

    def chunk_start(c):
        return pl.multiple_of(c * t, t)

    for h in range(N_IDX_HEADS):
        qis_ref[:, h * t:(h + 1) * t] = qi_ref[0, h * D_IDX:(h + 1) * D_IDX, :]

    def by_parity(c, fn):
        for parity, (cur, nxt) in enumerate(((0, 1), (1, 0))):
            pl.when((c & 1) == parity)(functools.partial(fn, cur, nxt))

    def chunk_loop(n, step):
        def pair_body(cc, carry):
            step(2 * cc, 0, 1)
            step(2 * cc + 1, 1, 0)
            return carry
        lax.fori_loop(0, n // 2, pair_body, 0)
        pl.when(n % 2 == 1)(lambda: step(n - 1, 0, 1))

    def score_matmul(c, dst):
        kc = ki_ref[0, pl.ds(chunk_start(c), t), :]
        dst[...] = _dot(kc, qis_ref[...])

    span = keys_ref.shape[0]
    neg_row = -1 - lax.broadcasted_iota(I32, (SCORE_ROWS, t), 0)

    def score_finish(src, c):
        blocks = []
        for r0 in range(0, t, SCORE_ROWS):
            acc = jnp.zeros((SCORE_ROWS, t), F32)
            for h in range(N_IDX_HEADS):
                s = src[r0:r0 + SCORE_ROWS, h * t:(h + 1) * t]
                acc = acc + wt_ref[0, h:h + 1, :] * jnp.maximum(s, 0.0)
            key = _sortable(acc)
            zero_rank = neg_row - (c * t + r0)
            blocks.append(jnp.where(key == 0, zero_rank, jnp.where(key < 0, key - span, key)))
        return jnp.concatenate(blocks, axis=0)

    def store_keys(c, key):
        keys_ref[pl.ds(chunk_start(c), t), :] = key
        hi_ref[pl.ds(chunk_start(c), t), :] = (key >> 16).astype(I16)
        lo_ref[pl.ds(chunk_start(c), t), :] = ((key & 0xFFFF) - 0x8000).astype(I16)

    def score_step(c, cur, nxt):
        score_matmul(c + 1, sa_refs[nxt])
        store_keys(c, score_finish(sa_refs[cur], c))

    score_matmul(0, sa_refs[0])
    chunk_loop(j, score_step)
    sk = lax.broadcasted_iota(I32, (t, t), 0)
    tq = lax.broadcasted_iota(I32, (t, t), 1)

    by_parity(j, lambda cur, nxt: store_keys(
        j, jnp.where(sk <= tq, score_finish(sa_refs[cur], j), MASK_KEY)))
    store_keys(j + 1, jnp.full((t, t), MASK_KEY, I32))

    tau_ref[...] = jnp.full((SUBLANES, t), MASK_KEY + 1, I32)

    def packed_rows(ref, cc):
        return ref[2 * t * cc:2 * t * (cc + 1), :].reshape(2 * t // PACKED_ROWS, PACKED_ROWS, t)

    def count16(plane_ref, pred, pairs):
        parts = []
        for cc in range(pairs):
            ones = jnp.where(pred(packed_rows(plane_ref, cc)), jnp.ones((), BF16),
                             jnp.zeros((), BF16))
            parts += [ones[i] for i in range(2 * t // PACKED_ROWS)]
        while len(parts) > 1:
            parts = [sum(parts[i:i + 2][1:], parts[i]) for i in range(0, len(parts), 2)]
        return jnp.sum(parts[0].astype(F32), axis=0, keepdims=True).astype(I32)

    def bisect16(plane_ref, want, n_all, pairs):
        def bit_body(i, carry):
            prefix, n_ge = carry
            cand = prefix | jnp.left_shift(jnp.int32(1), 15 - i)
            cand_b = jnp.broadcast_to((cand - 0x8000).astype(I16), (PACKED_ROWS, t))
            cnt = count16(plane_ref, lambda v: v >= cand_b[None], pairs)
            ok = cnt >= want
            return jnp.where(ok, cand, prefix), jnp.where(ok, cnt, n_ge)
        return lax.fori_loop(0, 16, bit_body, (jnp.zeros((1, t), I32), n_all))

    def count(pred):
        def body(cc, cnt):
            start = pl.multiple_of(cc * (2 * t), 2 * t)
            kk = keys_ref[pl.ds(start, 2 * t), :].reshape(2 * groups, SUBLANES, t)
            return cnt + jnp.sum(jnp.where(pred(kk, cc * (2 * t)), 1, 0), axis=0)
        cnt = lax.fori_loop(0, npair, body, jnp.zeros((SUBLANES, t), I32))
        return jnp.sum(cnt, axis=0, keepdims=True)

    def select_threshold(pairs):
        hi_u, n_ge_hi = bisect16(hi_ref, k_top, jnp.full((1, t), nchunk * t, I32), pairs)
        hi_b = jnp.broadcast_to((hi_u - 0x8000).astype(I16), (PACKED_ROWS, t))
        n_gt_hi = count16(hi_ref, lambda v: v > hi_b[None], pairs)
        for cc in range(pairs):
            lo_ref[2 * t * cc:2 * t * (cc + 1), :] = jnp.where(
                packed_rows(hi_ref, cc) == hi_b[None], packed_rows(lo_ref, cc),
                jnp.full((), -0x8000, I16)).reshape(2 * t, t)
        lo_u, n_ge_lo = bisect16(lo_ref, k_top - n_gt_hi, n_ge_hi - n_gt_hi, pairs)
        tau = (hi_u - 0x8000) * 65536 + lo_u
        n_ge = n_gt_hi + n_ge_lo
        tau_ref[...] = jnp.broadcast_to(jnp.maximum(tau, MASK_KEY + 1), (SUBLANES, t))

        tie = jnp.logical_and(n_ge > k_top, tau > MASK_KEY)

        @pl.when(jnp.max(jnp.where(tie, 1, 0)) > 0)
        def _():
            tau_b = jnp.broadcast_to(tau, (SUBLANES, t))
            n_gt = count(lambda kk, base: kk > tau_b[None])
            want = k_top - n_gt
            row = lax.broadcasted_iota(I32, (2 * groups, SUBLANES, t), 0) * SUBLANES + \
                lax.broadcasted_iota(I32, (2 * groups, SUBLANES, t), 1)

            index_bits = (span - 1).bit_length()

            def pos_body(i, pos):
                cand = pos | jnp.left_shift(jnp.int32(1), index_bits - 1 - i)
                cand_b = jnp.broadcast_to(cand, (SUBLANES, t))
                cnt = count(lambda kk, base: jnp.logical_and(kk == tau_b[None],
                                                            row + base < cand_b[None]))
                return jnp.where(cnt < want, cand, pos)

            pos = lax.fori_loop(0, index_bits, pos_body, jnp.zeros((1, t), I32))
            pos_b = jnp.broadcast_to(jnp.where(tie, pos, jnp.int32(2 ** 30)), (SUBLANES, t))

            def demote_body(cc, carry):
                start = pl.multiple_of(cc * (2 * t), 2 * t)
                kk = keys_ref[pl.ds(start, 2 * t), :].reshape(2 * groups, SUBLANES, t)
                drop = jnp.logical_and(kk == tau_b[None], row + cc * (2 * t) > pos_b[None])
                keys_ref[pl.ds(start, 2 * t), :] = jnp.where(drop, MASK_KEY, kk).reshape(2 * t, t)
                return carry

            lax.fori_loop(0, npair, demote_body, 0)

    max_pairs = (keys_ref.shape[0] // t) // 2
    for pairs in range(1, max_pairs + 1):
        pl.when(jnp.logical_and(npair == pairs, nchunk * t > k_top))(
            functools.partial(select_threshold, pairs))

    m_ref[...] = jnp.full(m_ref.shape, -jnp.inf, F32)
    acc_ref[...] = jnp.zeros(acc_ref.shape, F32)
    tau_sel = tau_ref[...]
    ones_rows = jnp.ones((PACKED_ROWS, t), BF16)

    def set_mask(c):
        kk = keys_ref[pl.ds(chunk_start(c), t), :].reshape(groups, SUBLANES, t)
        mask_ref[...] = jnp.where(kk >= tau_sel[None], 0.0, -jnp.inf).reshape(t, t)

    FAR, NEAR, DIAG = None, 1, 0

    def logits_matmul(c, slot, h, kind):
        kvc = kv_ref[0, pl.ds(chunk_start(c), t), :]
        s = _dot(kvc, q_ref[0, h * KV_LORA:(h + 1) * KV_LORA, :]) + mask_ref[...]
        if kind is not FAR:
            s = s + btile_ref[kind, h]
        s2_refs[slot][h] = s
        cmax_refs[slot][h] = jnp.max(s.reshape(groups, SUBLANES, t), axis=0)

    def attend_step(c, cur, nxt, next_kind):
        kvtc = jnp.concatenate([kvt_ref[0, :, pl.ds(chunk_start(c), t)], ones_rows], axis=0)
        if nxt is not None:
            set_mask(c + 1)
        pending = []

        def flush():
            h, alpha, pv = pending.pop(0)
            acc_ref[h] = alpha * acc_ref[h] + pv

        for h in range(N_HEADS):
            if nxt is not None:
                logits_matmul(c + 1, nxt, h, next_kind)
            s = s2_refs[cur][h]
            cmax = jnp.max(cmax_refs[cur][h], axis=0, keepdims=True)
            m_old = m_ref[h]
            m_new = jnp.maximum(m_old, cmax)
            m_safe = jnp.where(m_new == -jnp.inf, 0.0, m_new)
            alpha = jnp.exp2(m_old - m_safe)
            p = jnp.exp2(s - m_safe)
            m_ref[h] = m_new
            pending.append((h, alpha, _dot(kvtc, p.astype(BF16))))
            if len(pending) > PV_DELAY:
                flush()
        while pending:
            flush()

    def first_logits(kind):
        for h in range(N_HEADS):
            logits_matmul(0, 0, h, kind)

    set_mask(0)
    for first_kind, when in ((DIAG, j == 0), (NEAR, j == 1), (FAR, j >= 2)):
        pl.when(when)(functools.partial(first_logits, first_kind))
    chunk_loop(jnp.maximum(j - 2, 0), functools.partial(attend_step, next_kind=FAR))
    pl.when(j >= 2)(lambda: by_parity(
        j - 2, lambda cur, nxt: attend_step(j - 2, cur, nxt, NEAR)))
    pl.when(j >= 1)(lambda: by_parity(
        j - 1, lambda cur, nxt: attend_step(j - 1, cur, nxt, DIAG)))
    by_parity(j, lambda cur, nxt: attend_step(j, cur, None, None))

    ys = []
    for h in range(N_HEADS):
        denom = acc_ref[h, KV_LORA:KV_LORA + 1, :]
        o_t = (acc_ref[h, 0:KV_LORA, :] * (1.0 / denom)).astype(BF16)
        ys.append(_dot_tn(o_t, wuv_ref[h]))
    y = jnp.concatenate(ys, axis=-1)
    o_ref[0] = (y * sga_ref[0]).astype(BF16)


def _sparse_attention(q, qi, wt, kv, kvt, ki, sga, w_uv, layer, btiles, *, t):
    w_uv, w_uv_spec = _layer_operand(w_uv, layer)
    bsz, s, _ = kv.shape
    k_top = min(TOPK_MAX, s // 4)
    assert (s + t) // PACKED_ROWS <= 256, "per-position key counts must stay exact in bf16"
    blk = lambda w: pl.BlockSpec((1, t, w), lambda b, j: (b, j, 0))
    cols = lambda rows: pl.BlockSpec((1, rows, t), lambda b, j: (b, 0, j))
    in_specs = [
        cols(N_HEADS * KV_LORA),
        cols(N_IDX_HEADS * D_IDX),
        cols(N_IDX_HEADS),
        pl.BlockSpec((1, s, KV_LORA), lambda b, j: (b, 0, 0)),
        pl.BlockSpec((1, KV_LORA, s), lambda b, j: (b, 0, 0)),
        pl.BlockSpec((1, s, D_IDX), lambda b, j: (b, 0, 0)),
        blk(D_ATTN),
        w_uv_spec,
        pl.BlockSpec(btiles.shape, lambda b, j: (0, 0, 0, 0)),
    ]
    return pl.pallas_call(
        functools.partial(_attn_kernel, t=t, k_top=k_top),
        name="sparse_attention",
        out_shape=jax.ShapeDtypeStruct((bsz, s, D_ATTN), BF16),
        grid=(bsz, s // t),
        in_specs=in_specs,
        out_specs=blk(D_ATTN),
        scratch_shapes=[
            pltpu.VMEM((s + t, t), I32),
            pltpu.VMEM((D_IDX, N_IDX_HEADS * t), BF16),
            pltpu.VMEM((SUBLANES, t), I32),
            pltpu.VMEM((N_HEADS, 1, t), F32),
            pltpu.VMEM((t, t), F32),
            pltpu.VMEM((N_HEADS, KV_LORA + PACKED_ROWS, t), F32),
            pltpu.VMEM((t, N_IDX_HEADS * t), F32),
            pltpu.VMEM((t, N_IDX_HEADS * t), F32),
            pltpu.VMEM((N_HEADS, t, t), F32),
            pltpu.VMEM((N_HEADS, t, t), F32),
            pltpu.VMEM((s + t, t), I16),
            pltpu.VMEM((s + t, t), I16),
            pltpu.VMEM((N_HEADS, SUBLANES, t), F32),
            pltpu.VMEM((N_HEADS, SUBLANES, t), F32),
        ],
        compiler_params=pltpu.CompilerParams(
            dimension_semantics=("arbitrary", "arbitrary"), vmem_limit_bytes=VMEM_LIMIT),
    )(q, qi, wt, kv, kvt, ki, sga, w_uv, btiles)


def _outproj_kernel(x_ref, yc_ref, ya_ref, wo_ref, mod_ref, gpost_ref, o_ref, *, d_model):
    y = _dot(jnp.concatenate([yc_ref[0], ya_ref[0]], axis=-1), wo_ref[...])
    yn = (y * lax.rsqrt(jnp.mean(y * y, axis=-1, keepdims=True) + EPS)) * gpost_ref[...]
    gate = mod_ref[0, :, 2 * d_model:3 * d_model]
    o_ref[0] = x_ref[0] + gate * yn


def _out_projection(x, yc, ya, w_out, mod, g_post, layer, *, tm):
    bsz, s, d = x.shape
    assert s % tm == 0
    (w_out, w_spec), (mod, mod_spec), (g_post, g_spec) = (
        _layer_operand(w_out, layer), _mod_operand(mod, layer), _layer_operand(g_post, layer))
    return pl.pallas_call(
        functools.partial(_outproj_kernel, d_model=d),
        name="out_projection",
        out_shape=jax.ShapeDtypeStruct((bsz, s, d), F32),
        grid=(bsz, s // tm),
        in_specs=[
            pl.BlockSpec((1, tm, d), lambda b, j: (b, j, 0)),
            pl.BlockSpec((1, tm, D_CONV), lambda b, j: (b, j, 0)),
            pl.BlockSpec((1, tm, D_ATTN), lambda b, j: (b, j, 0)),
            w_spec, mod_spec, g_spec,
        ],
        out_specs=pl.BlockSpec((1, tm, d), lambda b, j: (b, j, 0)),
        compiler_params=pltpu.CompilerParams(
            dimension_semantics=("arbitrary", "arbitrary"), vmem_limit_bytes=VMEM_LIMIT),
    )(x, yc, ya, w_out, mod, g_post)


def _pack_w_in(w_in):
    split = OFF_KW + D_IDX + N_IDX_HEADS
    pad = jnp.zeros(w_in.shape[:-1] + (KW_PAD,), BF16)
    return jnp.concatenate(
        [w_in[..., :split].astype(BF16), pad, w_in[..., split:].astype(BF16)], axis=-1)


def kernel(x, c, w_ada, b_ada, g_pre, w_in, conv_w, conv_b, conv_ln_g, conv_ln_b, w_pw2, q_norm_g,
           w_uq, w_qidx, kv_norm_g, w_uv, rel_bias, w_out, g_post):
    depth = w_ada.shape[0]
    bsz, s, d = x.shape
    tm = min(512, s)
    t = min(ATTN_BLOCK, s)
    assert s % tm == 0 and s % t == 0 and d % LANES == 0
    assert tm % INPROJ_SUB == 0 and INPROJ_SUB % CONV_ROWS == 0

    mod = _modulation(c, w_ada, b_ada)
    btiles = _bias_tiles(rel_bias, t)
    w_packed = _pack_w_in(w_in)
    w_pw2, w_uq, w_qidx, w_uv, w_out = (a.astype(BF16) for a in (w_pw2, w_uq, w_qidx, w_uv, w_out))
    for l in range(depth):
        yc, q, qi, kv, kvt, ki, wt, sga = _in_projection(
            x, mod, l, g_pre, w_packed, conv_w, conv_b, conv_ln_g, conv_ln_b, w_pw2, q_norm_g,
            w_uq, w_qidx, kv_norm_g, tm=tm)
        ya = _sparse_attention(q, qi, wt, kv, kvt, ki, sga, w_uv, l, btiles, t=t)
        x = _out_projection(x, yc, ya, w_out, mod, g_post, l, tm=min(OUTPROJ_ROWS, s))
    return x
```

```python
import functools
import math

import jax
import jax.numpy as jnp
from jax import lax
from jax.experimental import pallas as pl
from jax.experimental.pallas import tpu as pltpu

F32 = jnp.float32
BF16 = jnp.bfloat16
I32 = jnp.int32
I16 = jnp.int16

D_CONV = 512
CONV_WIDTH = 31
N_HEADS = 8
D_HEAD_OUT = 64
D_ATTN = N_HEADS * D_HEAD_OUT
Q_LORA = 256
KV_LORA = 128
N_IDX_HEADS = 8
D_IDX = 64
TOPK_MAX = 256
N_BUCKETS = 32
MAX_DISTANCE = 128
EPS = 1e-6

LANES = 128
SUBLANES = 8
PACKED_ROWS = 2 * SUBLANES

KW_WIDTH = LANES
KW_PAD = KW_WIDTH - D_IDX - N_IDX_HEADS
OFF_UVAL = 0
OFF_UGATE = OFF_UVAL + D_CONV
OFF_GCONV = OFF_UGATE + D_CONV
OFF_CQ = OFF_GCONV + D_CONV
OFF_CKV = OFF_CQ + Q_LORA
OFF_KW = OFF_CKV + KV_LORA
OFF_GATTN = OFF_KW + KW_WIDTH

HALO = 32
CONV_ROWS = 64
INPROJ_SUB = 256
OUTPROJ_ROWS = 1024
ATTN_BLOCK = 256
PV_DELAY = 2
SCORE_ROWS = 32
VMEM_LIMIT = 48 * 1024 * 1024

INT_MIN = -(2 ** 31)
MASK_KEY = INT_MIN + 1
LOG2E = math.log2(math.e)


def _dot(a, b):
    return jnp.dot(a, b, preferred_element_type=F32)


def _dot_tn(a, b):
    return lax.dot_general(a, b, (((0,), (0,)), ((), ())), preferred_element_type=F32)


def _silu(v):
    return v * jax.nn.sigmoid(v)


def _mod_kernel(c_ref, w_ref, b_ref, o_ref):
    c = c_ref[...]
    o_ref[0] = _dot(_silu(c).astype(BF16), w_ref[0].astype(BF16)) + b_ref[0]


def _modulation(c, w_ada, b_ada):
    depth, d, d3 = w_ada.shape
    bsz = c.shape[0]
    tn = 1024
    return pl.pallas_call(
        _mod_kernel,
        name="adaln_mod",
        out_shape=jax.ShapeDtypeStruct((depth, bsz, d3), F32),
        grid=(depth, d3 // tn),
        in_specs=[
            pl.BlockSpec((bsz, d), lambda l, n: (0, 0)),
            pl.BlockSpec((1, d, tn), lambda l, n: (l, 0, n)),
            pl.BlockSpec((1, 1, tn), lambda l, n: (l, 0, n)),
        ],
        out_specs=pl.BlockSpec((1, bsz, tn), lambda l, n: (l, 0, n)),
        compiler_params=pltpu.CompilerParams(
            dimension_semantics=("arbitrary", "arbitrary"), vmem_limit_bytes=VMEM_LIMIT),
    )(c, w_ada, b_ada.reshape(depth, 1, d3))


def _t5_bucket(n):
    max_exact = N_BUCKETS // 2
    n = jnp.maximum(n, 0)
    nf = jnp.maximum(n, 1).astype(F32)
    large = max_exact + (jnp.log(nf / max_exact) / math.log(MAX_DISTANCE / max_exact)
                         * (N_BUCKETS - max_exact)).astype(I32)
    large = jnp.minimum(large, N_BUCKETS - 1)
    return jnp.where(n < max_exact, n, large)


def _bias_kernel(rb_ref, o_ref, *, t):
    sk = lax.broadcasted_iota(I32, (t, t), 0)
    tq = lax.broadcasted_iota(I32, (t, t), 1)
    for kind in range(2):
        bucket = _t5_bucket(tq - sk + kind * t)
        for h in range(N_HEADS):
            val = jnp.zeros((t, t), F32)
            for b in range(N_BUCKETS):
                val = jnp.where(bucket == b, rb_ref[b, h], val)
            o_ref[kind, h] = (val - rb_ref[N_BUCKETS - 1, h]) * LOG2E


def _bias_tiles(rel_bias, t):
    return pl.pallas_call(
        functools.partial(_bias_kernel, t=t),
        name="t5_bias_tiles",
        out_shape=jax.ShapeDtypeStruct((2, N_HEADS, t, t), F32),
        in_specs=[pl.BlockSpec(memory_space=pltpu.SMEM)],
        out_specs=pl.BlockSpec(memory_space=pltpu.VMEM),
    )(rel_bias)


def _inproj_kernel(x_ref, mod_ref, gpre_ref, w_ref, cw_ref, cb_ref, lng_ref, lnb_ref, wpw2_ref,
                   qg_ref, wuq_ref, wqi_ref, kvg_ref,
                   yc_ref, q_ref, qi_ref, kv_ref, kvt_ref, ki_ref, wt_ref, sga_ref,
                   hbuf, abuf, ybuf, shbuf, gcbuf, *, tm, d_model):
    j = pl.program_id(1)

    @pl.when(j == 0)
    def _():
        abuf[0:HALO, :] = jnp.zeros((HALO, D_CONV), F32)

    shift = mod_ref[0, :, 0:d_model]
    scale = mod_ref[0, :, d_model:2 * d_model]
    first = HALO - (CONV_WIDTH - 1)

    for base in range(0, tm, INPROJ_SUB):
        rows = slice(base, base + INPROJ_SUB)
        x = x_ref[0, rows, :]
        ms = jnp.mean(x * x, axis=-1, keepdims=True)
        xn = (x * lax.rsqrt(ms + EPS)) * gpre_ref[...]
        hbuf[rows, :] = (xn * (1.0 + scale) + shift).astype(BF16)
        h = hbuf[rows, :]

        u_val = _dot(h, w_ref[:, OFF_UVAL:OFF_UVAL + D_CONV])
        u_gate = _dot(h, w_ref[:, OFF_UGATE:OFF_UGATE + D_CONV])
        abuf[HALO + base:HALO + base + INPROJ_SUB, :] = u_val * jax.nn.sigmoid(u_gate)

        span = slice(base, base + INPROJ_SUB + HALO - SUBLANES)
        for r in range(1, SUBLANES):
            shbuf[r - 1, span, :] = abuf[base + r:base + r + INPROJ_SUB + HALO - SUBLANES, :]

        gcbuf[rows, :] = _silu(_dot(h, w_ref[:, OFF_GCONV:OFF_GCONV + D_CONV]))

        c_q = _dot(h, w_ref[:, OFF_CQ:OFF_CQ + Q_LORA])
        cq = (c_q * lax.rsqrt(jnp.mean(c_q * c_q, axis=-1, keepdims=True) + EPS)) * qg_ref[...]
        cq = cq.astype(BF16)
        q = _dot(cq, wuq_ref[...]) * (KV_LORA ** -0.5 * LOG2E)
        q_ref[0, :, rows] = jnp.transpose(q).astype(BF16)
        qi_ref[0, :, rows] = jnp.transpose(_dot(cq, wqi_ref[...])).astype(BF16)

        c_kv = _dot(h, w_ref[:, OFF_CKV:OFF_CKV + KV_LORA])
        kvn = (c_kv * lax.rsqrt(jnp.mean(c_kv * c_kv, axis=-1, keepdims=True) + EPS)) * kvg_ref[...]
        kv_ref[0, rows, :] = kvn.astype(BF16)
        kvt_ref[0, :, rows] = jnp.transpose(kvn).astype(BF16)

        kw = _dot(h, w_ref[:, OFF_KW:OFF_KW + KW_WIDTH])
        ki_ref[0, rows, :] = kw[:, 0:D_IDX].astype(BF16)
        kwt = jnp.transpose(kw)
        wt_ref[0, :, rows] = kwt[D_IDX:D_IDX + N_IDX_HEADS, :] * (N_IDX_HEADS ** -0.5 * D_IDX ** -0.5)

        sga_ref[0, rows, :] = _silu(_dot(h, w_ref[:, OFF_GATTN:OFF_GATTN + D_ATTN]))

    for base in range(0, tm, INPROJ_SUB):
        rows = slice(base, base + INPROJ_SUB)
        for r0 in range(base, base + INPROJ_SUB, CONV_ROWS):
            acc = jnp.broadcast_to(cb_ref[...], (CONV_ROWS, D_CONV))
            for k in range(CONV_WIDTH):
                q8, r = divmod(first + k, SUBLANES)
                taps = slice(r0 + q8 * SUBLANES, r0 + q8 * SUBLANES + CONV_ROWS)
                tap = abuf[taps, :] if r == 0 else shbuf[r - 1, taps, :]
                acc = acc + cw_ref[k:k + 1, :] * tap
            mu = jnp.mean(acc, axis=-1, keepdims=True)
            cen = acc - mu
            var = jnp.mean(cen * cen, axis=-1, keepdims=True)
            yn = cen * lax.rsqrt(var + EPS) * lng_ref[...] + lnb_ref[...]
            ybuf[r0:r0 + CONV_ROWS, :] = _silu(yn).astype(BF16)

        yc_ref[0, rows, :] = (_dot(ybuf[rows, :], wpw2_ref[...]) * gcbuf[rows, :]).astype(BF16)

    abuf[0:HALO, :] = abuf[tm:tm + HALO, :]


def _layer_operand(stacked, layer):
    if stacked.ndim == 2:
        stacked = stacked.reshape(stacked.shape[0], 1, stacked.shape[1])
    rest = stacked.shape[1:]
    return stacked, pl.BlockSpec((None,) + rest, lambda b, j: (layer,) + (0,) * len(rest))


def _mod_operand(mod, layer):
    depth, bsz, d3 = mod.shape
    return (mod.reshape(depth, bsz, 1, d3),
            pl.BlockSpec((None, 1, 1, d3), lambda b, j: (layer, b, 0, 0)))


def _in_projection(x, mod, layer, g_pre, w_packed, conv_w, conv_b, ln_g, ln_b, w_pw2, q_g, w_uq,
                   w_qi, kv_g, *, tm):
    bsz, s, d = x.shape
    operands = [_mod_operand(mod, layer)] + [
        _layer_operand(a, layer) for a in (g_pre, w_packed, conv_w, conv_b, ln_g, ln_b, w_pw2,
                                           q_g, w_uq, w_qi, kv_g)]
    args = (x,) + tuple(a for a, _ in operands)
    in_specs = [pl.BlockSpec((1, tm, d), lambda b, j: (b, j, 0))] + [sp for _, sp in operands]
    blk = lambda w: pl.BlockSpec((1, tm, w), lambda b, j: (b, j, 0))
    out_shape = (
        jax.ShapeDtypeStruct((bsz, s, D_CONV), BF16),
        jax.ShapeDtypeStruct((bsz, N_HEADS * KV_LORA, s), BF16),
        jax.ShapeDtypeStruct((bsz, N_IDX_HEADS * D_IDX, s), BF16),
        jax.ShapeDtypeStruct((bsz, s, KV_LORA), BF16),
        jax.ShapeDtypeStruct((bsz, KV_LORA, s), BF16),
        jax.ShapeDtypeStruct((bsz, s, D_IDX), BF16),
        jax.ShapeDtypeStruct((bsz, N_IDX_HEADS, s), F32),
        jax.ShapeDtypeStruct((bsz, s, D_ATTN), F32),
    )
    out_specs = (
        blk(D_CONV),
        pl.BlockSpec((1, N_HEADS * KV_LORA, tm), lambda b, j: (b, 0, j)),
        pl.BlockSpec((1, N_IDX_HEADS * D_IDX, tm), lambda b, j: (b, 0, j)),
        blk(KV_LORA),
        pl.BlockSpec((1, KV_LORA, tm), lambda b, j: (b, 0, j)),
        blk(D_IDX),
        pl.BlockSpec((1, N_IDX_HEADS, tm), lambda b, j: (b, 0, j)),
        blk(D_ATTN),
    )
    return pl.pallas_call(
        functools.partial(_inproj_kernel, tm=tm, d_model=d),
        name="in_projection",
        out_shape=out_shape,
        grid=(bsz, s // tm),
        in_specs=in_specs,
        out_specs=out_specs,
        scratch_shapes=[
            pltpu.VMEM((tm, d), BF16),
            pltpu.VMEM((tm + HALO, D_CONV), F32),
            pltpu.VMEM((tm, D_CONV), BF16),
            pltpu.VMEM((SUBLANES - 1, tm + HALO, D_CONV), F32),
            pltpu.VMEM((tm, D_CONV), F32),
        ],
        compiler_params=pltpu.CompilerParams(
            dimension_semantics=("arbitrary", "arbitrary"), vmem_limit_bytes=VMEM_LIMIT),
    )(*args)


def _sortable(v):
    b = pltpu.bitcast(v, I32)
    return jnp.where(b < 0, INT_MIN - b, b)


def _attn_kernel(q_ref, qi_ref, wt_ref, kv_ref, kvt_ref, ki_ref, sga_ref, wuv_ref, btile_ref,
                 o_ref, keys_ref, qis_ref, tau_ref, m_ref, mask_ref, acc_ref, sa0_ref, sa1_ref,
                 s2a_ref, s2b_ref, hi_ref, lo_ref, cmaxa_ref, cmaxb_ref, nge_ref, *, t, k_top):
    sa_refs = (sa0_ref, sa1_ref)
    s2_refs = (s2a_ref, s2b_ref)
    cmax_refs = (cmaxa_ref, cmaxb_ref)
    j = pl.program_id(1)
    nchunk = j + 1
    npair = (j + 2) // 2
    groups = t // SUBLANES

    def chunk_start(c):
        return pl.multiple_of(c * t, t)

    for h in range(N_IDX_HEADS):
        qis_ref[:, h * t:(h + 1) * t] = qi_ref[0, h * D_IDX:(h + 1) * D_IDX, :]

    def by_parity(c, fn):
        for parity, (cur, nxt) in enumerate(((0, 1), (1, 0))):
            pl.when((c & 1) == parity)(functools.partial(fn, cur, nxt))

    def chunk_loop(n, step):
        def pair_body(cc, carry):
            step(2 * cc, 0, 1)
            step(2 * cc + 1, 1, 0)
            return carry
        lax.fori_loop(0, n // 2, pair_body, 0)
        pl.when(n % 2 == 1)(lambda: step(n - 1, 0, 1))

    def score_matmul(c, dst):
        kc = ki_ref[0, pl.ds(chunk_start(c), t), :]
        dst[...] = _dot(kc, qis_ref[...])

    span = keys_ref.shape[0]
    neg_row = -1 - lax.broadcasted_iota(I32, (SCORE_ROWS, t), 0)

    def score_finish(src, c):
        blocks = []
        for r0 in range(0, t, SCORE_ROWS):
            acc = jnp.zeros((SCORE_ROWS, t), F32)
            for h in range(N_IDX_HEADS):
                s = src[r0:r0 + SCORE_ROWS, h * t:(h + 1) * t]
                acc = acc + wt_ref[0, h:h + 1, :] * jnp.maximum(s, 0.0)
            key = _sortable(acc)
            zero_rank = neg_row - (c * t + r0)
            blocks.append(jnp.where(key == 0, zero_rank, jnp.where(key < 0, key - span, key)))
        return jnp.concatenate(blocks, axis=0)

    def store_keys(c, key):
        keys_ref[pl.ds(chunk_start(c), t), :] = key
        hi_ref[pl.ds(chunk_start(c), t), :] = (key >> 16).astype(I16)
        lo_ref[pl.ds(chunk_start(c), t), :] = ((key & 0xFFFF) - 0x8000).astype(I16)

    def score_step(c, cur, nxt):
        score_matmul(c + 1, sa_refs[nxt])
        store_keys(c, score_finish(sa_refs[cur], c))

    score_matmul(0, sa_refs[0])
    chunk_loop(j, score_step)
    sk = lax.broadcasted_iota(I32, (t, t), 0)
    tq = lax.broadcasted_iota(I32, (t, t), 1)

    by_parity(j, lambda cur, nxt: store_keys(
        j, jnp.where(sk <= tq, score_finish(sa_refs[cur], j), MASK_KEY)))
    store_keys(j + 1, jnp.full((t, t), MASK_KEY, I32))

    tau_ref[...] = jnp.full((SUBLANES, t), MASK_KEY + 1, I32)
    nge_ref[...] = jnp.zeros((SUBLANES, t), I32)

    def packed_rows(ref, c):
        return ref[t * c:t * (c + 1), :].reshape(t // PACKED_ROWS, PACKED_ROWS, t)

    def count16(plane_ref, pred, chunks):
        parts = []
        for c in range(chunks):
            ones = jnp.where(pred(packed_rows(plane_ref, c)), jnp.ones((), BF16),
                             jnp.zeros((), BF16))
            parts += [ones[i] for i in range(t // PACKED_ROWS)]
        while len(parts) > 1:
            parts = [sum(parts[i:i + 2][1:], parts[i]) for i in range(0, len(parts), 2)]
        return jnp.sum(parts[0].astype(F32), axis=0, keepdims=True).astype(I32)

    def bisect16(plane_ref, want, n_all, chunks):
        def bit_body(i, carry):
            prefix, n_ge = carry
            cand = prefix | jnp.left_shift(jnp.int32(1), 15 - i)
            cand_b = jnp.broadcast_to((cand - 0x8000).astype(I16), (PACKED_ROWS, t))
            cnt = count16(plane_ref, lambda v: v >= cand_b[None], chunks)
            ok = cnt >= want
            return jnp.where(ok, cand, prefix), jnp.where(ok, cnt, n_ge)
        return lax.fori_loop(0, 16, bit_body, (jnp.zeros((1, t), I32), n_all))

    def count(pred):
        def body(cc, cnt):
            start = pl.multiple_of(cc * (2 * t), 2 * t)
            kk = keys_ref[pl.ds(start, 2 * t), :].reshape(2 * groups, SUBLANES, t)
            return cnt + jnp.sum(jnp.where(pred(kk, cc * (2 * t)), 1, 0), axis=0)
        cnt = lax.fori_loop(0, npair, body, jnp.zeros((SUBLANES, t), I32))
        return jnp.sum(cnt, axis=0, keepdims=True)

    def select_threshold(chunks):
        hi_u, n_ge_hi = bisect16(hi_ref, k_top, jnp.full((1, t), chunks * t, I32), chunks)
        hi_b = jnp.broadcast_to((hi_u - 0x8000).astype(I16), (PACKED_ROWS, t))
        n_gt_hi = count16(hi_ref, lambda v: v > hi_b[None], chunks)
        for c in range(chunks):
            lo_ref[t * c:t * (c + 1), :] = jnp.where(
                packed_rows(hi_ref, c) == hi_b[None], packed_rows(lo_ref, c),
                jnp.full((), -0x8000, I16)).reshape(t, t)
        lo_u, n_ge_lo = bisect16(lo_ref, k_top - n_gt_hi, n_ge_hi - n_gt_hi, chunks)
        tau = (hi_u - 0x8000) * 65536 + lo_u
        tau_ref[...] = jnp.broadcast_to(jnp.maximum(tau, MASK_KEY + 1), (SUBLANES, t))
        nge_ref[...] = jnp.broadcast_to(n_gt_hi + n_ge_lo, (SUBLANES, t))

    for chunks in range(k_top // t + 1, keys_ref.shape[0] // t):
        pl.when(nchunk == chunks)(functools.partial(select_threshold, chunks))

    def break_ties():
        tau = tau_ref[0:1, :]
        tie = nge_ref[0:1, :] > k_top

        @pl.when(jnp.max(jnp.where(tie, 1, 0)) > 0)
        def _():
            tau_b = jnp.broadcast_to(tau, (SUBLANES, t))
            n_gt = count(lambda kk, base: kk > tau_b[None])
            want = k_top - n_gt
            row = lax.broadcasted_iota(I32, (2 * groups, SUBLANES, t), 0) * SUBLANES + \
                lax.broadcasted_iota(I32, (2 * groups, SUBLANES, t), 1)

            index_bits = (span - 1).bit_length()

            def pos_body(i, pos):
                cand = pos | jnp.left_shift(jnp.int32(1), index_bits - 1 - i)
                cand_b = jnp.broadcast_to(cand, (SUBLANES, t))
                cnt = count(lambda kk, base: jnp.logical_and(kk == tau_b[None],
                                                            row + base < cand_b[None]))
                return jnp.where(cnt < want, cand, pos)

            pos = lax.fori_loop(0, index_bits, pos_body, jnp.zeros((1, t), I32))
            pos_b = jnp.broadcast_to(jnp.where(tie, pos, jnp.int32(2 ** 30)), (SUBLANES, t))

            def demote_body(cc, carry):
                start = pl.multiple_of(cc * (2 * t), 2 * t)
                kk = keys_ref[pl.ds(start, 2 * t), :].reshape(2 * groups, SUBLANES, t)
                drop = jnp.logical_and(kk == tau_b[None], row + cc * (2 * t) > pos_b[None])
                keys_ref[pl.ds(start, 2 * t), :] = jnp.where(drop, MASK_KEY, kk).reshape(2 * t, t)
                return carry

            lax.fori_loop(0, npair, demote_body, 0)

    break_ties()

    m_ref[...] = jnp.full(m_ref.shape, -jnp.inf, F32)
    acc_ref[...] = jnp.zeros(acc_ref.shape, F32)
    tau_sel = tau_ref[...]
    ones_rows = jnp.ones((PACKED_ROWS, t), BF16)

    def set_mask(c):
        kk = keys_ref[pl.ds(chunk_start(c), t), :].reshape(groups, SUBLANES, t)
        mask_ref[...] = jnp.where(kk >= tau_sel[None], 0.0, -jnp.inf).reshape(t, t)

    FAR, NEAR, DIAG = None, 1, 0

    def logits_matmul(c, slot, h, kind):
        kvc = kv_ref[0, pl.ds(chunk_start(c), t), :]
        s = _dot(kvc, q_ref[0, h * KV_LORA:(h + 1) * KV_LORA, :]) + mask_ref[...]
        if kind is not FAR:
            s = s + btile_ref[kind, h]
        s2_refs[slot][h] = s
        cmax_refs[slot][h] = jnp.max(s.reshape(groups, SUBLANES, t), axis=0)

    def attend_step(c, cur, nxt, next_kind):
        kvtc = jnp.concatenate([kvt_ref[0, :, pl.ds(chunk_start(c), t)], ones_rows], axis=0)
        if nxt is not None:
            set_mask(c + 1)
        pending = []

        def flush():
            h, alpha, pv = pending.pop(0)
            acc_ref[h] = alpha * acc_ref[h] + pv

        for h in range(N_HEADS):
            if nxt is not None:
                logits_matmul(c + 1, nxt, h, next_kind)
            s = s2_refs[cur][h]
            cmax = jnp.max(cmax_refs[cur][h], axis=0, keepdims=True)
            m_old = m_ref[h]
            m_new = jnp.maximum(m_old, cmax)
            m_safe = jnp.where(m_new == -jnp.inf, 0.0, m_new)
            alpha = jnp.exp2(m_old - m_safe)
            p = jnp.exp2(s - m_safe)
            m_ref[h] = m_new
            pending.append((h, alpha, _dot(kvtc, p.astype(BF16))))
            if len(pending) > PV_DELAY:
                flush()
        while pending:
            flush()

    def first_logits(kind):
        for h in range(N_HEADS):
            logits_matmul(0, 0, h, kind)

    set_mask(0)
    for first_kind, when in ((DIAG, j == 0), (NEAR, j == 1), (FAR, j >= 2)):
        pl.when(when)(functools.partial(first_logits, first_kind))
    chunk_loop(jnp.maximum(j - 2, 0), functools.partial(attend_step, next_kind=FAR))
    pl.when(j >= 2)(lambda: by_parity(
        j - 2, lambda cur, nxt: attend_step(j - 2, cur, nxt, NEAR)))
    pl.when(j >= 1)(lambda: by_parity(
        j - 1, lambda cur, nxt: attend_step(j - 1, cur, nxt, DIAG)))
    by_parity(j, lambda cur, nxt: attend_step(j, cur, None, None))

    ys = []
    for h in range(N_HEADS):
        denom = acc_ref[h, KV_LORA:KV_LORA + 1, :]
        o_t = (acc_ref[h, 0:KV_LORA, :] * (1.0 / denom)).astype(BF16)
        ys.append(_dot_tn(o_t, wuv_ref[h]))
    y = jnp.concatenate(ys, axis=-1)
    o_ref[0] = (y * sga_ref[0]).astype(BF16)


def _sparse_attention(q, qi, wt, kv, kvt, ki, sga, w_uv, layer, btiles, *, t):
    w_uv, w_uv_spec = _layer_operand(w_uv, layer)
    bsz, s, _ = kv.shape
    k_top = min(TOPK_MAX, s // 4)
    assert (s + t) // PACKED_ROWS <= 256, "per-position key counts must stay exact in bf16"
    blk = lambda w: pl.BlockSpec((1, t, w), lambda b, j: (b, j, 0))
    cols = lambda rows: pl.BlockSpec((1, rows, t), lambda b, j: (b, 0, j))
    in_specs = [
        cols(N_HEADS * KV_LORA),
        cols(N_IDX_HEADS * D_IDX),
        cols(N_IDX_HEADS),
        pl.BlockSpec((1, s, KV_LORA), lambda b, j: (b, 0, 0)),
        pl.BlockSpec((1, KV_LORA, s), lambda b, j: (b, 0, 0)),
        pl.BlockSpec((1, s, D_IDX), lambda b, j: (b, 0, 0)),
        blk(D_ATTN),
        w_uv_spec,
        pl.BlockSpec(btiles.shape, lambda b, j: (0, 0, 0, 0)),
    ]
    return pl.pallas_call(
        functools.partial(_attn_kernel, t=t, k_top=k_top),
        name="sparse_attention",
        out_shape=jax.ShapeDtypeStruct((bsz, s, D_ATTN), BF16),
        grid=(bsz, s // t),
        in_specs=in_specs,
        out_specs=blk(D_ATTN),
        scratch_shapes=[
            pltpu.VMEM((s + t, t), I32),
            pltpu.VMEM((D_IDX, N_IDX_HEADS * t), BF16),
            pltpu.VMEM((SUBLANES, t), I32),
            pltpu.VMEM((N_HEADS, 1, t), F32),
            pltpu.VMEM((t, t), F32),
            pltpu.VMEM((N_HEADS, KV_LORA + PACKED_ROWS, t), F32),
            pltpu.VMEM((t, N_IDX_HEADS * t), F32),
            pltpu.VMEM((t, N_IDX_HEADS * t), F32),
            pltpu.VMEM((N_HEADS, t, t), F32),
            pltpu.VMEM((N_HEADS, t, t), F32),
            pltpu.VMEM((s + t, t), I16),
            pltpu.VMEM((s + t, t), I16),
            pltpu.VMEM((N_HEADS, SUBLANES, t), F32),
            pltpu.VMEM((N_HEADS, SUBLANES, t), F32),
            pltpu.VMEM((SUBLANES, t), I32),
        ],
        compiler_params=pltpu.CompilerParams(
            dimension_semantics=("arbitrary", "arbitrary"), vmem_limit_bytes=VMEM_LIMIT),
    )(q, qi, wt, kv, kvt, ki, sga, w_uv, btiles)


def _outproj_kernel(x_ref, yc_ref, ya_ref, wo_ref, mod_ref, gpost_ref, o_ref, *, d_model):
    y = _dot(jnp.concatenate([yc_ref[0], ya_ref[0]], axis=-1), wo_ref[...])
    yn = (y * lax.rsqrt(jnp.mean(y * y, axis=-1, keepdims=True) + EPS)) * gpost_ref[...]
    gate = mod_ref[0, :, 2 * d_model:3 * d_model]
    o_ref[0] = x_ref[0] + gate * yn


def _out_projection(x, yc, ya, w_out, mod, g_post, layer, *, tm):
    bsz, s, d = x.shape
    assert s % tm == 0
    (w_out, w_spec), (mod, mod_spec), (g_post, g_spec) = (
        _layer_operand(w_out, layer), _mod_operand(mod, layer), _layer_operand(g_post, layer))
    return pl.pallas_call(
        functools.partial(_outproj_kernel, d_model=d),
        name="out_projection",
        out_shape=jax.ShapeDtypeStruct((bsz, s, d), F32),
        grid=(bsz, s // tm),
        in_specs=[
            pl.BlockSpec((1, tm, d), lambda b, j: (b, j, 0)),
            pl.BlockSpec((1, tm, D_CONV), lambda b, j: (b, j, 0)),
            pl.BlockSpec((1, tm, D_ATTN), lambda b, j: (b, j, 0)),
            w_spec, mod_spec, g_spec,
        ],
        out_specs=pl.BlockSpec((1, tm, d), lambda b, j: (b, j, 0)),
        compiler_params=pltpu.CompilerParams(
            dimension_semantics=("arbitrary", "arbitrary"), vmem_limit_bytes=VMEM_LIMIT),
    )(x, yc, ya, w_out, mod, g_post)


def _pack_w_in(w_in):
    split = OFF_KW + D_IDX + N_IDX_HEADS
    pad = jnp.zeros(w_in.shape[:-1] + (KW_PAD,), BF16)
    return jnp.concatenate(
        [w_in[..., :split].astype(BF16), pad, w_in[..., split:].astype(BF16)], axis=-1)


def kernel(x, c, w_ada, b_ada, g_pre, w_in, conv_w, conv_b, conv_ln_g, conv_ln_b, w_pw2, q_norm_g,
           w_uq, w_qidx, kv_norm_g, w_uv, rel_bias, w_out, g_post):
    depth = w_ada.shape[0]
    bsz, s, d = x.shape
    tm = min(512, s)
    t = min(ATTN_BLOCK, s)
    assert s % tm == 0 and s % t == 0 and d % LANES == 0
    assert tm % INPROJ_SUB == 0 and INPROJ_SUB % CONV_ROWS == 0

    mod = _modulation(c, w_ada, b_ada)
    btiles = _bias_tiles(rel_bias, t)
    w_packed = _pack_w_in(w_in)
    w_pw2, w_uq, w_qidx, w_uv, w_out = (a.astype(BF16) for a in (w_pw2, w_uq, w_qidx, w_uv, w_out))
    for l in range(depth):
        yc, q, qi, kv, kvt, ki, wt, sga = _in_projection(
            x, mod, l, g_pre, w_packed, conv_w, conv_b, conv_ln_g, conv_ln_b, w_pw2, q_norm_g,
            w_uq, w_qidx, kv_norm_g, tm=tm)
        ya = _sparse_attention(q, qi, wt, kv, kvt, ki, sga, w_uv, l, btiles, t=t)
        x = _out_projection(x, yc, ya, w_out, mod, g_post, l, tm=min(OUTPROJ_ROWS, s))
    return x
```

```python
import functools
import math

import jax
import jax.numpy as jnp
from jax import lax
from jax.experimental import pallas as pl
from jax.experimental.pallas import tpu as pltpu

F32 = jnp.float32
BF16 = jnp.bfloat16
I32 = jnp.int32
I16 = jnp.int16

D_CONV = 512
CONV_WIDTH = 31
N_HEADS = 8
D_HEAD_OUT = 64
D_ATTN = N_HEADS * D_HEAD_OUT
Q_LORA = 256
KV_LORA = 128
N_IDX_HEADS = 8
D_IDX = 64
TOPK_MAX = 256
N_BUCKETS = 32
MAX_DISTANCE = 128
EPS = 1e-6

LANES = 128
SUBLANES = 8
PACKED_ROWS = 2 * SUBLANES

KW_WIDTH = LANES
KW_PAD = KW_WIDTH - D_IDX - N_IDX_HEADS
OFF_UVAL = 0
OFF_UGATE = OFF_UVAL + D_CONV
OFF_GCONV = OFF_UGATE + D_CONV
OFF_CQ = OFF_GCONV + D_CONV
OFF_CKV = OFF_CQ + Q_LORA
OFF_KW = OFF_CKV + KV_LORA
OFF_GATTN = OFF_KW + KW_WIDTH

HALO = 32
CONV_ROWS = 64
INPROJ_SUB = 256
OUTPROJ_ROWS = 1024
ATTN_BLOCK = 256
PV_DELAY = 2
SCORE_ROWS = 32
VMEM_LIMIT = 48 * 1024 * 1024

INT_MIN = -(2 ** 31)
MASK_KEY = INT_MIN + 1
LOG2E = math.log2(math.e)


def _dot(a, b):
    return jnp.dot(a, b, preferred_element_type=F32)


def _dot_tn(a, b):
    return lax.dot_general(a, b, (((0,), (0,)), ((), ())), preferred_element_type=F32)


def _silu(v):
    return v * jax.nn.sigmoid(v)


def _mod_kernel(c_ref, w_ref, b_ref, o_ref):
    c = c_ref[...]
    o_ref[0] = _dot(_silu(c).astype(BF16), w_ref[0].astype(BF16)) + b_ref[0]


def _modulation(c, w_ada, b_ada):
    depth, d, d3 = w_ada.shape
    bsz = c.shape[0]
    tn = 1024
    return pl.pallas_call(
        _mod_kernel,
        name="adaln_mod",
        out_shape=jax.ShapeDtypeStruct((depth, bsz, d3), F32),
        grid=(depth, d3 // tn),
        in_specs=[
            pl.BlockSpec((bsz, d), lambda l, n: (0, 0)),
            pl.BlockSpec((1, d, tn), lambda l, n: (l, 0, n)),
            pl.BlockSpec((1, 1, tn), lambda l, n: (l, 0, n)),
        ],
        out_specs=pl.BlockSpec((1, bsz, tn), lambda l, n: (l, 0, n)),
        compiler_params=pltpu.CompilerParams(
            dimension_semantics=("arbitrary", "arbitrary"), vmem_limit_bytes=VMEM_LIMIT),
    )(c, w_ada, b_ada.reshape(depth, 1, d3))


def _t5_bucket(n):
    max_exact = N_BUCKETS // 2
    n = jnp.maximum(n, 0)
    nf = jnp.maximum(n, 1).astype(F32)
    large = max_exact + (jnp.log(nf / max_exact) / math.log(MAX_DISTANCE / max_exact)
                         * (N_BUCKETS - max_exact)).astype(I32)
    large = jnp.minimum(large, N_BUCKETS - 1)
    return jnp.where(n < max_exact, n, large)


def _bias_kernel(rb_ref, o_ref, *, t):
    sk = lax.broadcasted_iota(I32, (t, t), 0)
    tq = lax.broadcasted_iota(I32, (t, t), 1)
    for kind in range(2):
        bucket = _t5_bucket(tq - sk + kind * t)
        for h in range(N_HEADS):
            val = jnp.zeros((t, t), F32)
            for b in range(N_BUCKETS):
                val = jnp.where(bucket == b, rb_ref[b, h], val)
            o_ref[kind, h] = (val - rb_ref[N_BUCKETS - 1, h]) * LOG2E


def _bias_tiles(rel_bias, t):
    return pl.pallas_call(
        functools.partial(_bias_kernel, t=t),
        name="t5_bias_tiles",
        out_shape=jax.ShapeDtypeStruct((2, N_HEADS, t, t), F32),
        in_specs=[pl.BlockSpec(memory_space=pltpu.SMEM)],
        out_specs=pl.BlockSpec(memory_space=pltpu.VMEM),
    )(rel_bias)


def _inproj_kernel(x_ref, mod_ref, gpre_ref, w_ref, cw_ref, cb_ref, lng_ref, lnb_ref, wpw2_ref,
                   qg_ref, wuq_ref, wqi_ref, kvg_ref,
                   yc_ref, q_ref, qi_ref, kv_ref, kvt_ref, ki_ref, wt_ref, sga_ref,
                   hbuf, abuf, ybuf, shbuf, gcbuf, *, tm, d_model):
    j = pl.program_id(1)

    @pl.when(j == 0)
    def _():
        abuf[0:HALO, :] = jnp.zeros((HALO, D_CONV), F32)

    shift = mod_ref[0, :, 0:d_model]
    scale = mod_ref[0, :, d_model:2 * d_model]
    first = HALO - (CONV_WIDTH - 1)

    for base in range(0, tm, INPROJ_SUB):
        rows = slice(base, base + INPROJ_SUB)
        x = x_ref[0, rows, :]
        ms = jnp.mean(x * x, axis=-1, keepdims=True)
        xn = (x * lax.rsqrt(ms + EPS)) * gpre_ref[...]
        hbuf[rows, :] = (xn * (1.0 + scale) + shift).astype(BF16)
        h = hbuf[rows, :]

        u_val = _dot(h, w_ref[:, OFF_UVAL:OFF_UVAL + D_CONV])
        u_gate = _dot(h, w_ref[:, OFF_UGATE:OFF_UGATE + D_CONV])
        abuf[HALO + base:HALO + base + INPROJ_SUB, :] = u_val * jax.nn.sigmoid(u_gate)

        span = slice(base, base + INPROJ_SUB + HALO - SUBLANES)
        for r in range(1, SUBLANES):
            shbuf[r - 1, span, :] = abuf[base + r:base + r + INPROJ_SUB + HALO - SUBLANES, :]

        gcbuf[rows, :] = _silu(_dot(h, w_ref[:, OFF_GCONV:OFF_GCONV + D_CONV]))

        c_q = _dot(h, w_ref[:, OFF_CQ:OFF_CQ + Q_LORA])
        cq = (c_q * lax.rsqrt(jnp.mean(c_q * c_q, axis=-1, keepdims=True) + EPS)) * qg_ref[...]
        cq = cq.astype(BF16)
        q = _dot(cq, wuq_ref[...]) * (KV_LORA ** -0.5 * LOG2E)
        q_ref[0, :, rows] = jnp.transpose(q).astype(BF16)
        qi_ref[0, :, rows] = jnp.transpose(_dot(cq, wqi_ref[...])).astype(BF16)

        c_kv = _dot(h, w_ref[:, OFF_CKV:OFF_CKV + KV_LORA])
        kvn = (c_kv * lax.rsqrt(jnp.mean(c_kv * c_kv, axis=-1, keepdims=True) + EPS)) * kvg_ref[...]
        kv_ref[0, rows, :] = kvn.astype(BF16)
        kvt_ref[0, :, rows] = jnp.transpose(kvn).astype(BF16)

        kw = _dot(h, w_ref[:, OFF_KW:OFF_KW + KW_WIDTH])
        ki_ref[0, rows, :] = kw[:, 0:D_IDX].astype(BF16)
        kwt = jnp.transpose(kw)
        wt_ref[0, :, rows] = kwt[D_IDX:D_IDX + N_IDX_HEADS, :] * (N_IDX_HEADS ** -0.5 * D_IDX ** -0.5)

        sga_ref[0, rows, :] = _silu(_dot(h, w_ref[:, OFF_GATTN:OFF_GATTN + D_ATTN]))

    for base in range(0, tm, INPROJ_SUB):
        rows = slice(base, base + INPROJ_SUB)
        for r0 in range(base, base + INPROJ_SUB, CONV_ROWS):
            acc = jnp.broadcast_to(cb_ref[...], (CONV_ROWS, D_CONV))
            for k in range(CONV_WIDTH):
                q8, r = divmod(first + k, SUBLANES)
                taps = slice(r0 + q8 * SUBLANES, r0 + q8 * SUBLANES + CONV_ROWS)
                tap = abuf[taps, :] if r == 0 else shbuf[r - 1, taps, :]
                acc = acc + cw_ref[k:k + 1, :] * tap
            mu = jnp.mean(acc, axis=-1, keepdims=True)
            cen = acc - mu
            var = jnp.mean(cen * cen, axis=-1, keepdims=True)
            yn = cen * lax.rsqrt(var + EPS) * lng_ref[...] + lnb_ref[...]
            ybuf[r0:r0 + CONV_ROWS, :] = _silu(yn).astype(BF16)

        yc_ref[0, rows, :] = (_dot(ybuf[rows, :], wpw2_ref[...]) * gcbuf[rows, :]).astype(BF16)

    abuf[0:HALO, :] = abuf[tm:tm + HALO, :]


def _layer_operand(stacked, layer):
    if stacked.ndim == 2:
        stacked = stacked.reshape(stacked.shape[0], 1, stacked.shape[1])
    rest = stacked.shape[1:]
    return stacked, pl.BlockSpec((None,) + rest, lambda b, j: (layer,) + (0,) * len(rest))


def _mod_operand(mod, layer):
    depth, bsz, d3 = mod.shape
    return (mod.reshape(depth, bsz, 1, d3),
            pl.BlockSpec((None, 1, 1, d3), lambda b, j: (layer, b, 0, 0)))


def _in_projection(x, mod, layer, g_pre, w_packed, conv_w, conv_b, ln_g, ln_b, w_pw2, q_g, w_uq,
                   w_qi, kv_g, *, tm):
    bsz, s, d = x.shape
    operands = [_mod_operand(mod, layer)] + [
        _layer_operand(a, layer) for a in (g_pre, w_packed, conv_w, conv_b, ln_g, ln_b, w_pw2,
                                           q_g, w_uq, w_qi, kv_g)]
    args = (x,) + tuple(a for a, _ in operands)
    in_specs = [pl.BlockSpec((1, tm, d), lambda b, j: (b, j, 0))] + [sp for _, sp in operands]
    blk = lambda w: pl.BlockSpec((1, tm, w), lambda b, j: (b, j, 0))
    out_shape = (
        jax.ShapeDtypeStruct((bsz, s, D_CONV), BF16),
        jax.ShapeDtypeStruct((bsz, N_HEADS * KV_LORA, s), BF16),
        jax.ShapeDtypeStruct((bsz, N_IDX_HEADS * D_IDX, s), BF16),
        jax.ShapeDtypeStruct((bsz, s, KV_LORA), BF16),
        jax.ShapeDtypeStruct((bsz, KV_LORA, s), BF16),
        jax.ShapeDtypeStruct((bsz, s, D_IDX), BF16),
        jax.ShapeDtypeStruct((bsz, N_IDX_HEADS, s), F32),
        jax.ShapeDtypeStruct((bsz, s, D_ATTN), F32),
    )
    out_specs = (
        blk(D_CONV),
        pl.BlockSpec((1, N_HEADS * KV_LORA, tm), lambda b, j: (b, 0, j)),
        pl.BlockSpec((1, N_IDX_HEADS * D_IDX, tm), lambda b, j: (b, 0, j)),
        blk(KV_LORA),
        pl.BlockSpec((1, KV_LORA, tm), lambda b, j: (b, 0, j)),
        blk(D_IDX),
        pl.BlockSpec((1, N_IDX_HEADS, tm), lambda b, j: (b, 0, j)),
        blk(D_ATTN),
    )
    return pl.pallas_call(
        functools.partial(_inproj_kernel, tm=tm, d_model=d),
        name="in_projection",
        out_shape=out_shape,
        grid=(bsz, s // tm),
        in_specs=in_specs,
        out_specs=out_specs,
        scratch_shapes=[
            pltpu.VMEM((tm, d), BF16),
            pltpu.VMEM((tm + HALO, D_CONV), F32),
            pltpu.VMEM((tm, D_CONV), BF16),
            pltpu.VMEM((SUBLANES - 1, tm + HALO, D_CONV), F32),
            pltpu.VMEM((tm, D_CONV), F32),
        ],
        compiler_params=pltpu.CompilerParams(
            dimension_semantics=("arbitrary", "arbitrary"), vmem_limit_bytes=VMEM_LIMIT),
    )(*args)


def _sortable(v):
    b = pltpu.bitcast(v, I32)
    return jnp.where(b < 0, INT_MIN - b, b)


def _attn_kernel(q_ref, qi_ref, wt_ref, kv_ref, kvt_ref, ki_ref, sga_ref, wuv_ref, btile_ref,
                 o_ref, keys_ref, qis_ref, tau_ref, m_ref, mask_ref, acc_ref, sa0_ref, sa1_ref,
                 s2a_ref, s2b_ref, hi_ref, lo_ref, cmaxa_ref, cmaxb_ref, nge_ref, *, t, k_top):
    sa_refs = (sa0_ref, sa1_ref)
    s2_refs = (s2a_ref, s2b_ref)
    cmax_refs = (cmaxa_ref, cmaxb_ref)
    j = pl.program_id(1)
    nchunk = j + 1
    npair = (j + 2) // 2
    groups = t // SUBLANES

    def chunk_start(c):
        return pl.multiple_of(c * t, t)

    for h in range(N_IDX_HEADS):
        qis_ref[:, h * t:(h + 1) * t] = qi_ref[0, h * D_IDX:(h + 1) * D_IDX, :]

    def by_parity(c, fn):
        for parity, (cur, nxt) in enumerate(((0, 1), (1, 0))):
            pl.when((c & 1) == parity)(functools.partial(fn, cur, nxt))

    def chunk_loop(n, step):
        def pair_body(cc, carry):
            step(2 * cc, 0, 1)
            step(2 * cc + 1, 1, 0)
            return carry
        lax.fori_loop(0, n // 2, pair_body, 0)
        pl.when(n % 2 == 1)(lambda: step(n - 1, 0, 1))

    def score_matmul(c, dst):
        kc = ki_ref[0, pl.ds(chunk_start(c), t), :]
        dst[...] = _dot(kc, qis_ref[...])

    span = keys_ref.shape[0]
    neg_row = -1 - lax.broadcasted_iota(I32, (SCORE_ROWS, t), 0)

    def score_finish(src, c):
        blocks = []
        for r0 in range(0, t, SCORE_ROWS):
            acc = jnp.zeros((SCORE_ROWS, t), F32)
            for h in range(N_IDX_HEADS):
                s = src[r0:r0 + SCORE_ROWS, h * t:(h + 1) * t]
                acc = acc + wt_ref[0, h:h + 1, :] * jnp.maximum(s, 0.0)
            key = _sortable(acc)
            zero_rank = neg_row - (c * t + r0)
            blocks.append(jnp.where(key == 0, zero_rank, jnp.where(key < 0, key - span, key)))
        return jnp.concatenate(blocks, axis=0)

    def store_keys(c, key):
        keys_ref[pl.ds(chunk_start(c), t), :] = key
        hi_ref[pl.ds(chunk_start(c), t), :] = (key >> 16).astype(I16)
        lo_ref[pl.ds(chunk_start(c), t), :] = ((key & 0xFFFF) - 0x8000).astype(I16)

    def score_step(c, cur, nxt):
        score_matmul(c + 1, sa_refs[nxt])
        store_keys(c, score_finish(sa_refs[cur], c))

    score_matmul(0, sa_refs[0])
    chunk_loop(j, score_step)
    sk = lax.broadcasted_iota(I32, (t, t), 0)
    tq = lax.broadcasted_iota(I32, (t, t), 1)

    by_parity(j, lambda cur, nxt: store_keys(
        j, jnp.where(sk <= tq, score_finish(sa_refs[cur], j), MASK_KEY)))
    store_keys(j + 1, jnp.full((t, t), MASK_KEY, I32))

    tau_ref[...] = jnp.full((SUBLANES, t), MASK_KEY + 1, I32)
    nge_ref[...] = jnp.zeros((SUBLANES, t), I32)

    def packed_rows(ref, c):
        return ref[t * c:t * (c + 1), :].reshape(t // PACKED_ROWS, PACKED_ROWS, t)

    def count16(plane_ref, pred, chunks):
        parts = []
        for c in range(chunks):
            ones = jnp.where(pred(packed_rows(plane_ref, c)), jnp.ones((), BF16),
                             jnp.zeros((), BF16))
            parts += [ones[i] for i in range(t // PACKED_ROWS)]
        while len(parts) > 1:
            parts = [sum(parts[i:i + 2][1:], parts[i]) for i in range(0, len(parts), 2)]
        return jnp.sum(parts[0].astype(F32), axis=0, keepdims=True).astype(I32)

    def bisect16(plane_ref, want, n_all, chunks):
        def bit_body(i, carry):
            prefix, n_ge = carry
            cand = prefix | jnp.left_shift(jnp.int32(1), 15 - i)
            cand_b = jnp.broadcast_to((cand - 0x8000).astype(I16), (PACKED_ROWS, t))
            cnt = count16(plane_ref, lambda v: v >= cand_b[None], chunks)
            ok = cnt >= want
            return jnp.where(ok, cand, prefix), jnp.where(ok, cnt, n_ge)
        return lax.fori_loop(0, 16, bit_body, (jnp.zeros((1, t), I32), n_all))

    def count(pred):
        def body(cc, cnt):
            start = pl.multiple_of(cc * (2 * t), 2 * t)
            kk = keys_ref[pl.ds(start, 2 * t), :].reshape(2 * groups, SUBLANES, t)
            return cnt + jnp.sum(jnp.where(pred(kk, cc * (2 * t)), 1, 0), axis=0)
        cnt = lax.fori_loop(0, npair, body, jnp.zeros((SUBLANES, t), I32))
        return jnp.sum(cnt, axis=0, keepdims=True)

    def select_threshold(chunks):
        hi_u, n_ge_hi = bisect16(hi_ref, k_top, jnp.full((1, t), chunks * t, I32), chunks)
        hi_b = jnp.broadcast_to((hi_u - 0x8000).astype(I16), (PACKED_ROWS, t))
        n_gt_hi = count16(hi_ref, lambda v: v > hi_b[None], chunks)
        for c in range(chunks):
            lo_ref[t * c:t * (c + 1), :] = jnp.where(
                packed_rows(hi_ref, c) == hi_b[None], packed_rows(lo_ref, c),
                jnp.full((), -0x8000, I16)).reshape(t, t)
        lo_u, n_ge_lo = bisect16(lo_ref, k_top - n_gt_hi, n_ge_hi - n_gt_hi, chunks)
        tau = (hi_u - 0x8000) * 65536 + lo_u
        tau_ref[...] = jnp.broadcast_to(jnp.maximum(tau, MASK_KEY + 1), (SUBLANES, t))
        nge_ref[...] = jnp.broadcast_to(n_gt_hi + n_ge_lo, (SUBLANES, t))

    for chunks in range(k_top // t + 1, keys_ref.shape[0] // t):
        pl.when(nchunk == chunks)(functools.partial(select_threshold, chunks))

    def break_ties():
        tau = tau_ref[0:1, :]
        tie = nge_ref[0:1, :] > k_top

        @pl.when(jnp.max(jnp.where(tie, 1, 0)) > 0)
        def _():
            tau_b = jnp.broadcast_to(tau, (SUBLANES, t))
            n_gt = count(lambda kk, base: kk > tau_b[None])
            want = k_top - n_gt
            row = lax.broadcasted_iota(I32, (2 * groups, SUBLANES, t), 0) * SUBLANES + \
                lax.broadcasted_iota(I32, (2 * groups, SUBLANES, t), 1)

            index_bits = (span - 1).bit_length()

            def pos_body(i, pos):
                cand = pos | jnp.left_shift(jnp.int32(1), index_bits - 1 - i)
                cand_b = jnp.broadcast_to(cand, (SUBLANES, t))
                cnt = count(lambda kk, base: jnp.logical_and(kk == tau_b[None],
                                                            row + base < cand_b[None]))
                return jnp.where(cnt < want, cand, pos)

            pos = lax.fori_loop(0, index_bits, pos_body, jnp.zeros((1, t), I32))
            pos_b = jnp.broadcast_to(jnp.where(tie, pos, jnp.int32(2 ** 30)), (SUBLANES, t))

            def demote_body(cc, carry):
                start = pl.multiple_of(cc * (2 * t), 2 * t)
                kk = keys_ref[pl.ds(start, 2 * t), :].reshape(2 * groups, SUBLANES, t)
                drop = jnp.logical_and(kk == tau_b[None], row + cc * (2 * t) > pos_b[None])
                keys_ref[pl.ds(start, 2 * t), :] = jnp.where(drop, MASK_KEY, kk).reshape(2 * t, t)
                return carry

            lax.fori_loop(0, npair, demote_body, 0)

    break_ties()

    m_ref[...] = jnp.full(m_ref.shape, -jnp.inf, F32)
    acc_ref[...] = jnp.zeros(acc_ref.shape, F32)
    tau_sel = tau_ref[...]
    ones_rows = jnp.ones((PACKED_ROWS, t), BF16)

    def set_mask(c):
        kk = keys_ref[pl.ds(chunk_start(c), t), :].reshape(groups, SUBLANES, t)
        mask_ref[...] = jnp.where(kk >= tau_sel[None], 0.0, -jnp.inf).reshape(t, t)

    FAR, NEAR, DIAG = None, 1, 0

    def logits_matmul(c, slot, h, kind):
        kvc = kv_ref[0, pl.ds(chunk_start(c), t), :]
        s = _dot(kvc, q_ref[0, h * KV_LORA:(h + 1) * KV_LORA, :]) + mask_ref[...]
        if kind is not FAR:
            s = s + btile_ref[kind, h]
        s2_refs[slot][h] = s
        cmax_refs[slot][h] = jnp.max(s.reshape(groups, SUBLANES, t), axis=0)

    def attend_step(c, cur, nxt, next_kind):
        kvtc = jnp.concatenate([kvt_ref[0, :, pl.ds(chunk_start(c), t)], ones_rows], axis=0)
        if nxt is not None:
            set_mask(c + 1)
        pending = []

        def flush():
            h, alpha, pv = pending.pop(0)
            acc_ref[h] = alpha * acc_ref[h] + pv

        for h in range(N_HEADS):
            if nxt is not None:
                logits_matmul(c + 1, nxt, h, next_kind)
            s = s2_refs[cur][h]
            cmax = jnp.max(cmax_refs[cur][h], axis=0, keepdims=True)
            m_old = m_ref[h]
            m_new = jnp.maximum(m_old, cmax)
            m_safe = jnp.where(m_new == -jnp.inf, 0.0, m_new)
            alpha = jnp.exp2(m_old - m_safe)
            p = jnp.exp2(s - m_safe)
            m_ref[h] = m_new
            pending.append((h, alpha, _dot(kvtc, p.astype(BF16))))
            if len(pending) > PV_DELAY:
                flush()
        while pending:
            flush()

    def first_logits(kind):
        for h in range(N_HEADS):
            logits_matmul(0, 0, h, kind)

    set_mask(0)
    for first_kind, when in ((DIAG, j == 0), (NEAR, j == 1), (FAR, j >= 2)):
        pl.when(when)(functools.partial(first_logits, first_kind))
    chunk_loop(jnp.maximum(j - 2, 0), functools.partial(attend_step, next_kind=FAR))
    pl.when(j >= 2)(lambda: by_parity(
        j - 2, lambda cur, nxt: attend_step(j - 2, cur, nxt, NEAR)))
    pl.when(j >= 1)(lambda: by_parity(
        j - 1, lambda cur, nxt: attend_step(j - 1, cur, nxt, DIAG)))
    by_parity(j, lambda cur, nxt: attend_step(j, cur, None, None))

    ys = []
    for h in range(N_HEADS):
        denom = acc_ref[h, KV_LORA:KV_LORA + 1, :]
        o_t = (acc_ref[h, 0:KV_LORA, :] * (1.0 / denom)).astype(BF16)
        ys.append(_dot_tn(o_t, wuv_ref[h]))
    y = jnp.concatenate(ys, axis=-1)
    o_ref[0] = (y * sga_ref[0]).astype(BF16)


def _sparse_attention(q, qi, wt, kv, kvt, ki, sga, w_uv, layer, btiles, *, t):
    w_uv, w_uv_spec = _layer_operand(w_uv, layer)
    bsz, s, _ = kv.shape
    k_top = min(TOPK_MAX, s // 4)
    assert (s + t) // PACKED_ROWS <= 256, "per-position key counts must stay exact in bf16"
    blk = lambda w: pl.BlockSpec((1, t, w), lambda b, j: (b, j, 0))
    cols = lambda rows: pl.BlockSpec((1, rows, t), lambda b, j: (b, 0, j))
    in_specs = [
        cols(N_HEADS * KV_LORA),
        cols(N_IDX_HEADS * D_IDX),
        cols(N_IDX_HEADS),
        pl.BlockSpec((1, s, KV_LORA), lambda b, j: (b, 0, 0)),
        pl.BlockSpec((1, KV_LORA, s), lambda b, j: (b, 0, 0)),
        pl.BlockSpec((1, s, D_IDX), lambda b, j: (b, 0, 0)),
        blk(D_ATTN),
        w_uv_spec,
        pl.BlockSpec(btiles.shape, lambda b, j: (0, 0, 0, 0)),
    ]
    return pl.pallas_call(
        functools.partial(_attn_kernel, t=t, k_top=k_top),
        name="sparse_attention",
        out_shape=jax.ShapeDtypeStruct((bsz, s, D_ATTN), BF16),
        grid=(bsz, s // t),
        in_specs=in_specs,
        out_specs=blk(D_ATTN),
        scratch_shapes=[
            pltpu.VMEM((s + t, t), I32),
            pltpu.VMEM((D_IDX, N_IDX_HEADS * t), BF16),
            pltpu.VMEM((SUBLANES, t), I32),
            pltpu.VMEM((N_HEADS, 1, t), F32),
            pltpu.VMEM((t, t), F32),
            pltpu.VMEM((N_HEADS, KV_LORA + PACKED_ROWS, t), F32),
            pltpu.VMEM((t, N_IDX_HEADS * t), F32),
            pltpu.VMEM((t, N_IDX_HEADS * t), F32),
            pltpu.VMEM((N_HEADS, t, t), F32),
            pltpu.VMEM((N_HEADS, t, t), F32),
            pltpu.VMEM((s + t, t), I16),
            pltpu.VMEM((s + t, t), I16),
            pltpu.VMEM((N_HEADS, SUBLANES, t), F32),
            pltpu.VMEM((N_HEADS, SUBLANES, t), F32),
            pltpu.VMEM((SUBLANES, t), I32),
        ],
        compiler_params=pltpu.CompilerParams(
            dimension_semantics=("arbitrary", "arbitrary"), vmem_limit_bytes=VMEM_LIMIT),
    )(q, qi, wt, kv, kvt, ki, sga, w_uv, btiles)


def _outproj_kernel(x_ref, yc_ref, ya_ref, wo_ref, mod_ref, gpost_ref, o_ref, *, d_model):
    y = _dot(jnp.concatenate([yc_ref[0], ya_ref[0]], axis=-1), wo_ref[...])
    yn = (y * lax.rsqrt(jnp.mean(y * y, axis=-1, keepdims=True) + EPS)) * gpost_ref[...]
    gate = mod_ref[0, :, 2 * d_model:3 * d_model]
    o_ref[0] = x_ref[0] + gate * yn


def _out_projection(x, yc, ya, w_out, mod, g_post, layer, *, tm):
    bsz, s, d = x.shape
    assert s % tm == 0
    (w_out, w_spec), (mod, mod_spec), (g_post, g_spec) = (
        _layer_operand(w_out, layer), _mod_operand(mod, layer), _layer_operand(g_post, layer))
    return pl.pallas_call(
        functools.partial(_outproj_kernel, d_model=d),
        name="out_projection",
        out_shape=jax.ShapeDtypeStruct((bsz, s, d), F32),
        grid=(bsz, s // tm),
        in_specs=[
            pl.BlockSpec((1, tm, d), lambda b, j: (b, j, 0)),
            pl.BlockSpec((1, tm, D_CONV), lambda b, j: (b, j, 0)),
            pl.BlockSpec((1, tm, D_ATTN), lambda b, j: (b, j, 0)),
            w_spec, mod_spec, g_spec,
        ],
        out_specs=pl.BlockSpec((1, tm, d), lambda b, j: (b, j, 0)),
        compiler_params=pltpu.CompilerParams(
            dimension_semantics=("arbitrary", "arbitrary"), vmem_limit_bytes=VMEM_LIMIT),
    )(x, yc, ya, w_out, mod, g_post)


def _pack_w_in(w_in):
    split = OFF_KW + D_IDX + N_IDX_HEADS
    pad = jnp.zeros(w_in.shape[:-1] + (KW_PAD,), BF16)
    return jnp.concatenate(
        [w_in[..., :split].astype(BF16), pad, w_in[..., split:].astype(BF16)], axis=-1)


def kernel(x, c, w_ada, b_ada, g_pre, w_in, conv_w, conv_b, conv_ln_g, conv_ln_b, w_pw2, q_norm_g,
           w_uq, w_qidx, kv_norm_g, w_uv, rel_bias, w_out, g_post):
    depth = w_ada.shape[0]
    bsz, s, d = x.shape
    tm = min(512, s)
    t = min(ATTN_BLOCK, s)
    assert s % tm == 0 and s % t == 0 and d % LANES == 0
    assert tm % INPROJ_SUB == 0 and INPROJ_SUB % CONV_ROWS == 0
    assert w_in.shape[1:] == (d, OFF_GATTN - KW_PAD + D_ATTN) and conv_w.shape[1:] == (CONV_WIDTH, D_CONV)
    assert w_uv.shape[1:] == (N_HEADS, KV_LORA, D_HEAD_OUT) and rel_bias.shape == (N_BUCKETS, N_HEADS)
    assert t + 1 >= MAX_DISTANCE, "chunks before the previous one must all fall in the last T5 bucket"

    mod = _modulation(c, w_ada, b_ada)
    btiles = _bias_tiles(rel_bias, t)
    w_packed = _pack_w_in(w_in)
    w_pw2, w_uq, w_qidx, w_uv, w_out = (a.astype(BF16) for a in (w_pw2, w_uq, w_qidx, w_uv, w_out))
    for l in range(depth):
        yc, q, qi, kv, kvt, ki, wt, sga = _in_projection(
            x, mod, l, g_pre, w_packed, conv_w, conv_b, conv_ln_g, conv_ln_b, w_pw2, q_norm_g,
            w_uq, w_qidx, kv_norm_g, tm=tm)
        ya = _sparse_attention(q, qi, wt, kv, kvt, ki, sga, w_uv, l, btiles, t=t)
        x = _out_projection(x, yc, ya, w_out, mod, g_post, l, tm=min(OUTPROJ_ROWS, s))
    return x
```

```python
import functools
import math

import jax
import jax.numpy as jnp
from jax import lax
from jax.experimental import pallas as pl
from jax.experimental.pallas import tpu as pltpu

F32 = jnp.float32
BF16 = jnp.bfloat16
I32 = jnp.int32
I16 = jnp.int16

D_CONV = 512
CONV_WIDTH = 31
N_HEADS = 8
D_HEAD_OUT = 64
D_ATTN = N_HEADS * D_HEAD_OUT
Q_LORA = 256
KV_LORA = 128
N_IDX_HEADS = 8
D_IDX = 64
TOPK_MAX = 256
N_BUCKETS = 32
MAX_DISTANCE = 128
EPS = 1e-6

LANES = 128
SUBLANES = 8
PACKED_ROWS = 2 * SUBLANES

KW_WIDTH = LANES
KW_PAD = KW_WIDTH - D_IDX - N_IDX_HEADS
OFF_UVAL = 0
OFF_UGATE = OFF_UVAL + D_CONV
OFF_GCONV = OFF_UGATE + D_CONV
OFF_CQ = OFF_GCONV + D_CONV
OFF_CKV = OFF_CQ + Q_LORA
OFF_KW = OFF_CKV + KV_LORA
OFF_GATTN = OFF_KW + KW_WIDTH

HALO = 32
CONV_ROWS = 64
INPROJ_SUB = 256
OUTPROJ_ROWS = 1024
ATTN_BLOCK = 256
PV_DELAY = 2
SCORE_ROWS = 32
VMEM_LIMIT = 48 * 1024 * 1024

INT_MIN = -(2 ** 31)
MASK_KEY = INT_MIN + 1
LOG2E = math.log2(math.e)


def _dot(a, b):
    return jnp.dot(a, b, preferred_element_type=F32)


def _dot_tn(a, b):
    return lax.dot_general(a, b, (((0,), (0,)), ((), ())), preferred_element_type=F32)


def _silu(v):
    return v * jax.nn.sigmoid(v)


def _mod_kernel(c_ref, w_ref, b_ref, o_ref):
    c = c_ref[...]
    o_ref[0] = _dot(_silu(c).astype(BF16), w_ref[0].astype(BF16)) + b_ref[0]


def _modulation(c, w_ada, b_ada):
    depth, d, d3 = w_ada.shape
    bsz = c.shape[0]
    tn = 1024
    return pl.pallas_call(
        _mod_kernel,
        name="adaln_mod",
        out_shape=jax.ShapeDtypeStruct((depth, bsz, d3), F32),
        grid=(depth, d3 // tn),
        in_specs=[
            pl.BlockSpec((bsz, d), lambda l, n: (0, 0)),
            pl.BlockSpec((1, d, tn), lambda l, n: (l, 0, n)),
            pl.BlockSpec((1, 1, tn), lambda l, n: (l, 0, n)),
        ],
        out_specs=pl.BlockSpec((1, bsz, tn), lambda l, n: (l, 0, n)),
        compiler_params=pltpu.CompilerParams(
            dimension_semantics=("arbitrary", "arbitrary"), vmem_limit_bytes=VMEM_LIMIT),
    )(c, w_ada, b_ada.reshape(depth, 1, d3))


def _t5_bucket(n):
    max_exact = N_BUCKETS // 2
    n = jnp.maximum(n, 0)
    nf = jnp.maximum(n, 1).astype(F32)
    large = max_exact + (jnp.log(nf / max_exact) / math.log(MAX_DISTANCE / max_exact)
                         * (N_BUCKETS - max_exact)).astype(I32)
    large = jnp.minimum(large, N_BUCKETS - 1)
    return jnp.where(n < max_exact, n, large)


def _bias_kernel(rb_ref, o_ref, *, t):
    sk = lax.broadcasted_iota(I32, (t, t), 0)
    tq = lax.broadcasted_iota(I32, (t, t), 1)
    for kind in range(2):
        bucket = _t5_bucket(tq - sk + kind * t)
        for h in range(N_HEADS):
            val = jnp.zeros((t, t), F32)
            for b in range(N_BUCKETS):
                val = jnp.where(bucket == b, rb_ref[b, h], val)
            o_ref[kind, h] = (val - rb_ref[N_BUCKETS - 1, h]) * LOG2E


def _bias_tiles(rel_bias, t):
    return pl.pallas_call(
        functools.partial(_bias_kernel, t=t),
        name="t5_bias_tiles",
        out_shape=jax.ShapeDtypeStruct((2, N_HEADS, t, t), F32),
        in_specs=[pl.BlockSpec(memory_space=pltpu.SMEM)],
        out_specs=pl.BlockSpec(memory_space=pltpu.VMEM),
    )(rel_bias)


def _inproj_kernel(x_ref, mod_ref, gpre_ref, w_ref, cw_ref, cb_ref, lng_ref, lnb_ref, wpw2_ref,
                   qg_ref, wuq_ref, wqi_ref, kvg_ref,
                   yc_ref, q_ref, qi_ref, kv_ref, kvt_ref, ki_ref, wt_ref, sga_ref,
                   hbuf, abuf, ybuf, shbuf, gcbuf, *, tm, d_model):
    j = pl.program_id(1)

    @pl.when(j == 0)
    def _():
        abuf[0:HALO, :] = jnp.zeros((HALO, D_CONV), F32)

    shift = mod_ref[0, :, 0:d_model]
    scale = mod_ref[0, :, d_model:2 * d_model]
    first = HALO - (CONV_WIDTH - 1)

    for base in range(0, tm, INPROJ_SUB):
        rows = slice(base, base + INPROJ_SUB)
        x = x_ref[0, rows, :]
        ms = jnp.mean(x * x, axis=-1, keepdims=True)
        xn = (x * lax.rsqrt(ms + EPS)) * gpre_ref[...]
        hbuf[rows, :] = (xn * (1.0 + scale) + shift).astype(BF16)
        h = hbuf[rows, :]

        u_val = _dot(h, w_ref[:, OFF_UVAL:OFF_UVAL + D_CONV])
        u_gate = _dot(h, w_ref[:, OFF_UGATE:OFF_UGATE + D_CONV])
        abuf[HALO + base:HALO + base + INPROJ_SUB, :] = u_val * jax.nn.sigmoid(u_gate)

        span = slice(base, base + INPROJ_SUB + HALO - SUBLANES)
        for r in range(1, SUBLANES):
            shbuf[r - 1, span, :] = abuf[base + r:base + r + INPROJ_SUB + HALO - SUBLANES, :]

        gcbuf[rows, :] = _silu(_dot(h, w_ref[:, OFF_GCONV:OFF_GCONV + D_CONV]))

        c_q = _dot(h, w_ref[:, OFF_CQ:OFF_CQ + Q_LORA])
        cq = (c_q * lax.rsqrt(jnp.mean(c_q * c_q, axis=-1, keepdims=True) + EPS)) * qg_ref[...]
        cq = cq.astype(BF16)
        q = _dot(cq, wuq_ref[...]) * (KV_LORA ** -0.5 * LOG2E)
        q_ref[0, :, rows] = jnp.transpose(q).astype(BF16)
        qi_ref[0, :, rows] = jnp.transpose(_dot(cq, wqi_ref[...])).astype(BF16)

        c_kv = _dot(h, w_ref[:, OFF_CKV:OFF_CKV + KV_LORA])
        kvn = (c_kv * lax.rsqrt(jnp.mean(c_kv * c_kv, axis=-1, keepdims=True) + EPS)) * kvg_ref[...]
        kv_ref[0, rows, :] = kvn.astype(BF16)
        kvt_ref[0, :, rows] = jnp.transpose(kvn).astype(BF16)

        kw = _dot(h, w_ref[:, OFF_KW:OFF_KW + KW_WIDTH])
        ki_ref[0, rows, :] = kw[:, 0:D_IDX].astype(BF16)
        kwt = jnp.transpose(kw)
        wt_ref[0, :, rows] = kwt[D_IDX:D_IDX + N_IDX_HEADS, :] * (N_IDX_HEADS ** -0.5 * D_IDX ** -0.5)

        sga_ref[0, rows, :] = _silu(_dot(h, w_ref[:, OFF_GATTN:OFF_GATTN + D_ATTN]))

    for base in range(0, tm, INPROJ_SUB):
        rows = slice(base, base + INPROJ_SUB)
        for r0 in range(base, base + INPROJ_SUB, CONV_ROWS):
            acc = jnp.broadcast_to(cb_ref[...], (CONV_ROWS, D_CONV))
            for k in range(CONV_WIDTH):
                q8, r = divmod(first + k, SUBLANES)
                taps = slice(r0 + q8 * SUBLANES, r0 + q8 * SUBLANES + CONV_ROWS)
                tap = abuf[taps, :] if r == 0 else shbuf[r - 1, taps, :]
                acc = acc + cw_ref[k:k + 1, :] * tap
            mu = jnp.mean(acc, axis=-1, keepdims=True)
            cen = acc - mu
            var = jnp.mean(cen * cen, axis=-1, keepdims=True)
            yn = cen * lax.rsqrt(var + EPS) * lng_ref[...] + lnb_ref[...]
            ybuf[r0:r0 + CONV_ROWS, :] = _silu(yn).astype(BF16)

        yc_ref[0, rows, :] = (_dot(ybuf[rows, :], wpw2_ref[...]) * gcbuf[rows, :]).astype(BF16)

    abuf[0:HALO, :] = abuf[tm:tm + HALO, :]


def _layer_operand(stacked, layer):
    if stacked.ndim == 2:
        stacked = stacked.reshape(stacked.shape[0], 1, stacked.shape[1])
    rest = stacked.shape[1:]
    return stacked, pl.BlockSpec((None,) + rest, lambda b, j: (layer,) + (0,) * len(rest))


def _mod_operand(mod, layer):
    depth, bsz, d3 = mod.shape
    return (mod.reshape(depth, bsz, 1, d3),
            pl.BlockSpec((None, 1, 1, d3), lambda b, j: (layer, b, 0, 0)))


def _in_projection(x, mod, layer, g_pre, w_packed, conv_w, conv_b, ln_g, ln_b, w_pw2, q_g, w_uq,
                   w_qi, kv_g, *, tm):
    bsz, s, d = x.shape
    operands = [_mod_operand(mod, layer)] + [
        _layer_operand(a, layer) for a in (g_pre, w_packed, conv_w, conv_b, ln_g, ln_b, w_pw2,
                                           q_g, w_uq, w_qi, kv_g)]
    args = (x,) + tuple(a for a, _ in operands)
    in_specs = [pl.BlockSpec((1, tm, d), lambda b, j: (b, j, 0))] + [sp for _, sp in operands]
    blk = lambda w: pl.BlockSpec((1, tm, w), lambda b, j: (b, j, 0))
    out_shape = (
        jax.ShapeDtypeStruct((bsz, s, D_CONV), BF16),
        jax.ShapeDtypeStruct((bsz, N_HEADS * KV_LORA, s), BF16),
        jax.ShapeDtypeStruct((bsz, N_IDX_HEADS * D_IDX, s), BF16),
        jax.ShapeDtypeStruct((bsz, s, KV_LORA), BF16),
        jax.ShapeDtypeStruct((bsz, KV_LORA, s), BF16),
        jax.ShapeDtypeStruct((bsz, s, D_IDX), BF16),
        jax.ShapeDtypeStruct((bsz, N_IDX_HEADS, s), F32),
        jax.ShapeDtypeStruct((bsz, s, D_ATTN), F32),
    )
    out_specs = (
        blk(D_CONV),
        pl.BlockSpec((1, N_HEADS * KV_LORA, tm), lambda b, j: (b, 0, j)),
        pl.BlockSpec((1, N_IDX_HEADS * D_IDX, tm), lambda b, j: (b, 0, j)),
        blk(KV_LORA),
        pl.BlockSpec((1, KV_LORA, tm), lambda b, j: (b, 0, j)),
        blk(D_IDX),
        pl.BlockSpec((1, N_IDX_HEADS, tm), lambda b, j: (b, 0, j)),
        blk(D_ATTN),
    )
    return pl.pallas_call(
        functools.partial(_inproj_kernel, tm=tm, d_model=d),
        name="in_projection",
        out_shape=out_shape,
        grid=(bsz, s // tm),
        in_specs=in_specs,
        out_specs=out_specs,
        scratch_shapes=[
            pltpu.VMEM((tm, d), BF16),
            pltpu.VMEM((tm + HALO, D_CONV), F32),
            pltpu.VMEM((tm, D_CONV), BF16),
            pltpu.VMEM((SUBLANES - 1, tm + HALO, D_CONV), F32),
            pltpu.VMEM((tm, D_CONV), F32),
        ],
        compiler_params=pltpu.CompilerParams(
            dimension_semantics=("arbitrary", "arbitrary"), vmem_limit_bytes=VMEM_LIMIT),
    )(*args)


def _sortable(v):
    b = pltpu.bitcast(v, I32)
    return jnp.where(b < 0, INT_MIN - b, b)


def _attn_kernel(q_ref, qi_ref, wt_ref, kv_ref, kvt_ref, ki_ref, sga_ref, wuv_ref, btile_ref,
                 o_ref, keys_ref, qis_ref, tau_ref, m_ref, mask_ref, acc_ref, sa0_ref, sa1_ref,
                 s2a_ref, s2b_ref, hi_ref, lo_ref, cmaxa_ref, cmaxb_ref, nge_ref, *, t, k_top):
    sa_refs = (sa0_ref, sa1_ref)
    s2_refs = (s2a_ref, s2b_ref)
    cmax_refs = (cmaxa_ref, cmaxb_ref)
    j = pl.program_id(1)
    nchunk = j + 1
    npair = (j + 2) // 2
    groups = t // SUBLANES

    def chunk_start(c):
        return pl.multiple_of(c * t, t)

    for h in range(N_IDX_HEADS):
        qis_ref[:, h * t:(h + 1) * t] = qi_ref[0, h * D_IDX:(h + 1) * D_IDX, :]

    def by_parity(c, fn):
        for parity, (cur, nxt) in enumerate(((0, 1), (1, 0))):
            pl.when((c & 1) == parity)(functools.partial(fn, cur, nxt))

    def chunk_loop(n, step):
        def pair_body(cc, carry):
            step(2 * cc, 0, 1)
            step(2 * cc + 1, 1, 0)
            return carry
        lax.fori_loop(0, n // 2, pair_body, 0)
        pl.when(n % 2 == 1)(lambda: step(n - 1, 0, 1))

    def score_matmul(c, dst):
        kc = ki_ref[0, pl.ds(chunk_start(c), t), :]
        dst[...] = _dot(kc, qis_ref[...])

    span = keys_ref.shape[0]
    neg_row = -1 - lax.broadcasted_iota(I32, (SCORE_ROWS, t), 0)

    def score_finish(src, c):
        blocks = []
        for r0 in range(0, t, SCORE_ROWS):
            acc = jnp.zeros((SCORE_ROWS, t), F32)
            for h in range(N_IDX_HEADS):
                s = src[r0:r0 + SCORE_ROWS, h * t:(h + 1) * t]
                acc = acc + wt_ref[0, h:h + 1, :] * jnp.maximum(s, 0.0)
            key = _sortable(acc)
            zero_rank = neg_row - (c * t + r0)
            blocks.append(jnp.where(key == 0, zero_rank, jnp.where(key < 0, key - span, key)))
        return jnp.concatenate(blocks, axis=0)

    def store_keys(c, key):
        keys_ref[pl.ds(chunk_start(c), t), :] = key
        hi_ref[pl.ds(chunk_start(c), t), :] = (key >> 16).astype(I16)
        lo_ref[pl.ds(chunk_start(c), t), :] = ((key & 0xFFFF) - 0x8000).astype(I16)

    def score_step(c, cur, nxt):
        score_matmul(c + 1, sa_refs[nxt])
        store_keys(c, score_finish(sa_refs[cur], c))

    score_matmul(0, sa_refs[0])
    chunk_loop(j, score_step)
    sk = lax.broadcasted_iota(I32, (t, t), 0)
    tq = lax.broadcasted_iota(I32, (t, t), 1)

    by_parity(j, lambda cur, nxt: store_keys(
        j, jnp.where(sk <= tq, score_finish(sa_refs[cur], j), MASK_KEY)))
    store_keys(j + 1, jnp.full((t, t), MASK_KEY, I32))

    tau_ref[...] = jnp.full((SUBLANES, t), MASK_KEY + 1, I32)
    nge_ref[...] = jnp.zeros((SUBLANES, t), I32)

    def packed_rows(ref, c):
        return ref[t * c:t * (c + 1), :].reshape(t // PACKED_ROWS, PACKED_ROWS, t)

    def count16(plane_ref, pred, chunks):
        parts = []
        for c in range(chunks):
            ones = jnp.where(pred(packed_rows(plane_ref, c)), jnp.ones((), BF16),
                             jnp.zeros((), BF16))
            parts += [ones[i] for i in range(t // PACKED_ROWS)]
        while len(parts) > 1:
            parts = [sum(parts[i:i + 2][1:], parts[i]) for i in range(0, len(parts), 2)]
        return jnp.sum(parts[0].astype(F32), axis=0, keepdims=True)

    def bisect16(plane_ref, want, n_all, chunks):
        def bit_body(i, carry):
            prefix, n_ge = carry
            cand = prefix | jnp.left_shift(jnp.int32(1), 15 - i)
            cand_b = jnp.broadcast_to((cand - 0x8000).astype(I16), (PACKED_ROWS, t))
            cnt = count16(plane_ref, lambda v: v >= cand_b[None], chunks)
            ok = cnt >= want
            return jnp.where(ok, cand, prefix), jnp.where(ok, cnt, n_ge)
        return lax.fori_loop(0, 16, bit_body, (jnp.zeros((1, t), I32), n_all))

    def count(pred):
        def body(cc, cnt):
            start = pl.multiple_of(cc * (2 * t), 2 * t)
            kk = keys_ref[pl.ds(start, 2 * t), :].reshape(2 * groups, SUBLANES, t)
            return cnt + jnp.sum(jnp.where(pred(kk, cc * (2 * t)), 1, 0), axis=0)
        cnt = lax.fori_loop(0, npair, body, jnp.zeros((SUBLANES, t), I32))
        return jnp.sum(cnt, axis=0, keepdims=True)

    def select_threshold(chunks):
        hi_u, n_ge_hi = bisect16(hi_ref, float(k_top), jnp.full((1, t), chunks * t, F32), chunks)
        hi_b = jnp.broadcast_to((hi_u - 0x8000).astype(I16), (PACKED_ROWS, t))
        n_gt_hi = count16(hi_ref, lambda v: v > hi_b[None], chunks)
        for c in range(chunks):
            lo_ref[t * c:t * (c + 1), :] = jnp.where(
                packed_rows(hi_ref, c) == hi_b[None], packed_rows(lo_ref, c),
                jnp.full((), -0x8000, I16)).reshape(t, t)
        lo_u, n_ge_lo = bisect16(lo_ref, k_top - n_gt_hi, n_ge_hi - n_gt_hi, chunks)
        tau = (hi_u - 0x8000) * 65536 + lo_u
        tau_ref[...] = jnp.broadcast_to(jnp.maximum(tau, MASK_KEY + 1), (SUBLANES, t))
        nge_ref[...] = jnp.broadcast_to((n_gt_hi + n_ge_lo).astype(I32), (SUBLANES, t))

    for chunks in range(k_top // t + 1, keys_ref.shape[0] // t):
        pl.when(nchunk == chunks)(functools.partial(select_threshold, chunks))

    def break_ties():
        tau = tau_ref[0:1, :]
        tie = nge_ref[0:1, :] > k_top

        @pl.when(jnp.max(jnp.where(tie, 1, 0)) > 0)
        def _():
            tau_b = jnp.broadcast_to(tau, (SUBLANES, t))
            n_gt = count(lambda kk, base: kk > tau_b[None])
            want = k_top - n_gt
            row = lax.broadcasted_iota(I32, (2 * groups, SUBLANES, t), 0) * SUBLANES + \
                lax.broadcasted_iota(I32, (2 * groups, SUBLANES, t), 1)

            index_bits = (span - 1).bit_length()

            def pos_body(i, pos):
                cand = pos | jnp.left_shift(jnp.int32(1), index_bits - 1 - i)
                cand_b = jnp.broadcast_to(cand, (SUBLANES, t))
                cnt = count(lambda kk, base: jnp.logical_and(kk == tau_b[None],
                                                            row + base < cand_b[None]))
                return jnp.where(cnt < want, cand, pos)

            pos = lax.fori_loop(0, index_bits, pos_body, jnp.zeros((1, t), I32))
            pos_b = jnp.broadcast_to(jnp.where(tie, pos, jnp.int32(2 ** 30)), (SUBLANES, t))

            def demote_body(cc, carry):
                start = pl.multiple_of(cc * (2 * t), 2 * t)
                kk = keys_ref[pl.ds(start, 2 * t), :].reshape(2 * groups, SUBLANES, t)
                drop = jnp.logical_and(kk == tau_b[None], row + cc * (2 * t) > pos_b[None])
                keys_ref[pl.ds(start, 2 * t), :] = jnp.where(drop, MASK_KEY, kk).reshape(2 * t, t)
                return carry

            lax.fori_loop(0, npair, demote_body, 0)

    break_ties()

    m_ref[...] = jnp.full(m_ref.shape, -jnp.inf, F32)
    acc_ref[...] = jnp.zeros(acc_ref.shape, F32)
    tau_sel = tau_ref[...]
    ones_rows = jnp.ones((PACKED_ROWS, t), BF16)

    def set_mask(c):
        kk = keys_ref[pl.ds(chunk_start(c), t), :].reshape(groups, SUBLANES, t)
        mask_ref[...] = jnp.where(kk >= tau_sel[None], 0.0, -jnp.inf).reshape(t, t)

    FAR, NEAR, DIAG = None, 1, 0

    def logits_matmul(c, slot, h, kind):
        kvc = kv_ref[0, pl.ds(chunk_start(c), t), :]
        s = _dot(kvc, q_ref[0, h * KV_LORA:(h + 1) * KV_LORA, :]) + mask_ref[...]
        if kind is not FAR:
            s = s + btile_ref[kind, h]
        s2_refs[slot][h] = s
        cmax_refs[slot][h] = jnp.max(s.reshape(groups, SUBLANES, t), axis=0)

    def attend_step(c, cur, nxt, next_kind):
        kvtc = jnp.concatenate([kvt_ref[0, :, pl.ds(chunk_start(c), t)], ones_rows], axis=0)
        if nxt is not None:
            set_mask(c + 1)
        pending = []

        def flush():
            h, alpha, pv = pending.pop(0)
            acc_ref[h] = alpha * acc_ref[h] + pv

        for h in range(N_HEADS):
            if nxt is not None:
                logits_matmul(c + 1, nxt, h, next_kind)
            s = s2_refs[cur][h]
            cmax = jnp.max(cmax_refs[cur][h], axis=0, keepdims=True)
            m_old = m_ref[h]
            m_new = jnp.maximum(m_old, cmax)
            m_safe = jnp.where(m_new == -jnp.inf, 0.0, m_new)
            alpha = jnp.exp2(m_old - m_safe)
            p = jnp.exp2(s - m_safe)
            m_ref[h] = m_new
            pending.append((h, alpha, _dot(kvtc, p.astype(BF16))))
            if len(pending) > PV_DELAY:
                flush()
        while pending:
            flush()

    def first_logits(kind):
        for h in range(N_HEADS):
            logits_matmul(0, 0, h, kind)

    set_mask(0)
    for first_kind, when in ((DIAG, j == 0), (NEAR, j == 1), (FAR, j >= 2)):
        pl.when(when)(functools.partial(first_logits, first_kind))
    chunk_loop(jnp.maximum(j - 2, 0), functools.partial(attend_step, next_kind=FAR))
    pl.when(j >= 2)(lambda: by_parity(
        j - 2, lambda cur, nxt: attend_step(j - 2, cur, nxt, NEAR)))
    pl.when(j >= 1)(lambda: by_parity(
        j - 1, lambda cur, nxt: attend_step(j - 1, cur, nxt, DIAG)))
    by_parity(j, lambda cur, nxt: attend_step(j, cur, None, None))

    ys = []
    for h in range(N_HEADS):
        denom = acc_ref[h, KV_LORA:KV_LORA + 1, :]
        o_t = (acc_ref[h, 0:KV_LORA, :] * (1.0 / denom)).astype(BF16)
        ys.append(_dot_tn(o_t, wuv_ref[h]))
    y = jnp.concatenate(ys, axis=-1)
    o_ref[0] = (y * sga_ref[0]).astype(BF16)


def _sparse_attention(q, qi, wt, kv, kvt, ki, sga, w_uv, layer, btiles, *, t):
    w_uv, w_uv_spec = _layer_operand(w_uv, layer)
    bsz, s, _ = kv.shape
    k_top = min(TOPK_MAX, s // 4)
    assert (s + t) // PACKED_ROWS <= 256, "per-position key counts must stay exact in bf16"
    blk = lambda w: pl.BlockSpec((1, t, w), lambda b, j: (b, j, 0))
    cols = lambda rows: pl.BlockSpec((1, rows, t), lambda b, j: (b, 0, j))
    in_specs = [
        cols(N_HEADS * KV_LORA),
        cols(N_IDX_HEADS * D_IDX),
        cols(N_IDX_HEADS),
        pl.BlockSpec((1, s, KV_LORA), lambda b, j: (b, 0, 0)),
        pl.BlockSpec((1, KV_LORA, s), lambda b, j: (b, 0, 0)),
        pl.BlockSpec((1, s, D_IDX), lambda b, j: (b, 0, 0)),
        blk(D_ATTN),
        w_uv_spec,
        pl.BlockSpec(btiles.shape, lambda b, j: (0, 0, 0, 0)),
    ]
    return pl.pallas_call(
        functools.partial(_attn_kernel, t=t, k_top=k_top),
        name="sparse_attention",
        out_shape=jax.ShapeDtypeStruct((bsz, s, D_ATTN), BF16),
        grid=(bsz, s // t),
        in_specs=in_specs,
        out_specs=blk(D_ATTN),
        scratch_shapes=[
            pltpu.VMEM((s + t, t), I32),
            pltpu.VMEM((D_IDX, N_IDX_HEADS * t), BF16),
            pltpu.VMEM((SUBLANES, t), I32),
            pltpu.VMEM((N_HEADS, 1, t), F32),
            pltpu.VMEM((t, t), F32),
            pltpu.VMEM((N_HEADS, KV_LORA + PACKED_ROWS, t), F32),
            pltpu.VMEM((t, N_IDX_HEADS * t), F32),
            pltpu.VMEM((t, N_IDX_HEADS * t), F32),
            pltpu.VMEM((N_HEADS, t, t), F32),
            pltpu.VMEM((N_HEADS, t, t), F32),
            pltpu.VMEM((s + t, t), I16),
            pltpu.VMEM((s + t, t), I16),
            pltpu.VMEM((N_HEADS, SUBLANES, t), F32),
            pltpu.VMEM((N_HEADS, SUBLANES, t), F32),
            pltpu.VMEM((SUBLANES, t), I32),
        ],
        compiler_params=pltpu.CompilerParams(
            dimension_semantics=("arbitrary", "arbitrary"), vmem_limit_bytes=VMEM_LIMIT),
    )(q, qi, wt, kv, kvt, ki, sga, w_uv, btiles)


def _outproj_kernel(x_ref, yc_ref, ya_ref, wo_ref, mod_ref, gpost_ref, o_ref, *, d_model):
    y = _dot(jnp.concatenate([yc_ref[0], ya_ref[0]], axis=-1), wo_ref[...])
    yn = (y * lax.rsqrt(jnp.mean(y * y, axis=-1, keepdims=True) + EPS)) * gpost_ref[...]
    gate = mod_ref[0, :, 2 * d_model:3 * d_model]
    o_ref[0] = x_ref[0] + gate * yn


def _out_projection(x, yc, ya, w_out, mod, g_post, layer, *, tm):
    bsz, s, d = x.shape
    assert s % tm == 0
    (w_out, w_spec), (mod, mod_spec), (g_post, g_spec) = (
        _layer_operand(w_out, layer), _mod_operand(mod, layer), _layer_operand(g_post, layer))
    return pl.pallas_call(
        functools.partial(_outproj_kernel, d_model=d),
        name="out_projection",
        out_shape=jax.ShapeDtypeStruct((bsz, s, d), F32),
        grid=(bsz, s // tm),
        in_specs=[
            pl.BlockSpec((1, tm, d), lambda b, j: (b, j, 0)),
            pl.BlockSpec((1, tm, D_CONV), lambda b, j: (b, j, 0)),
            pl.BlockSpec((1, tm, D_ATTN), lambda b, j: (b, j, 0)),
            w_spec, mod_spec, g_spec,
        ],
        out_specs=pl.BlockSpec((1, tm, d), lambda b, j: (b, j, 0)),
        compiler_params=pltpu.CompilerParams(
            dimension_semantics=("arbitrary", "arbitrary"), vmem_limit_bytes=VMEM_LIMIT),
    )(x, yc, ya, w_out, mod, g_post)


def _pack_w_in(w_in):
    split = OFF_KW + D_IDX + N_IDX_HEADS
    pad = jnp.zeros(w_in.shape[:-1] + (KW_PAD,), BF16)
    return jnp.concatenate(
        [w_in[..., :split].astype(BF16), pad, w_in[..., split:].astype(BF16)], axis=-1)


def kernel(x, c, w_ada, b_ada, g_pre, w_in, conv_w, conv_b, conv_ln_g, conv_ln_b, w_pw2, q_norm_g,
           w_uq, w_qidx, kv_norm_g, w_uv, rel_bias, w_out, g_post):
    depth = w_ada.shape[0]
    bsz, s, d = x.shape
    tm = min(512, s)
    t = min(ATTN_BLOCK, s)
    assert s % tm == 0 and s % t == 0 and d % LANES == 0
    assert tm % INPROJ_SUB == 0 and INPROJ_SUB % CONV_ROWS == 0
    assert w_in.shape[1:] == (d, OFF_GATTN - KW_PAD + D_ATTN) and conv_w.shape[1:] == (CONV_WIDTH, D_CONV)
    assert w_uv.shape[1:] == (N_HEADS, KV_LORA, D_HEAD_OUT) and rel_bias.shape == (N_BUCKETS, N_HEADS)
    assert t + 1 >= MAX_DISTANCE, "chunks before the previous one must all fall in the last T5 bucket"

    mod = _modulation(c, w_ada, b_ada)
    btiles = _bias_tiles(rel_bias, t)
    w_packed = _pack_w_in(w_in)
    w_pw2, w_uq, w_qidx, w_uv, w_out = (a.astype(BF16) for a in (w_pw2, w_uq, w_qidx, w_uv, w_out))
    for l in range(depth):
        yc, q, qi, kv, kvt, ki, wt, sga = _in_projection(
            x, mod, l, g_pre, w_packed, conv_w, conv_b, conv_ln_g, conv_ln_b, w_pw2, q_norm_g,
            w_uq, w_qidx, kv_norm_g, tm=tm)
        ya = _sparse_attention(q, qi, wt, kv, kvt, ki, sga, w_uv, l, btiles, t=t)
        x = _out_projection(x, yc, ya, w_out, mod, g_post, l, tm=min(OUTPROJ_ROWS, s))
    return x
```
